```python
import math
import jax, jax.numpy as jnp
from jax import lax
import numpy as np

D_MODEL = 1024
BATCH = 4
SEQ = 4096
DEPTH = 1

DN_HEADS = 4
DN_HEAD_DIM = 128
DN_WIDTH = DN_HEADS * DN_HEAD_DIM
CONV_WIDTH = 4
CHUNK = 64
DA_HEADS = 4
DA_QK_DIM = 64
DA_V_DIM = 2 * DA_QK_DIM
DA_WIDTH = DA_HEADS * DA_V_DIM
DA_QK_WIDTH = DA_HEADS * 2 * DA_QK_DIM
LAYER_INDEX = 0
LAMBDA_INIT = 0.8 - 0.6 * math.exp(-0.3 * LAYER_INDEX)
Q_BLOCK = 128
EPS = 1e-6
NEG_INF = -1e30

IN_SPLITS = (3 * DN_WIDTH, DN_WIDTH, DN_HEADS, DN_HEADS,
             DA_QK_WIDTH, DA_QK_WIDTH, DA_WIDTH, DA_WIDTH,
             D_MODEL, D_MODEL)
N_IN = sum(IN_SPLITS)
IN_SPLIT_IDX = [int(i) for i in np.cumsum(IN_SPLITS)[:-1]]

kernel_name = "hybrid_gdn_diffattn_gated_merge"


def rms_norm(x, w):
    xf = x.astype(jnp.float32)
    y = xf * lax.rsqrt(jnp.mean(xf * xf, axis=-1, keepdims=True) + EPS)
    return (y * w.astype(jnp.float32)).astype(x.dtype)


def l2norm(x):
    xf = x.astype(jnp.float32)
    return xf * lax.rsqrt(jnp.sum(xf * xf, axis=-1, keepdims=True) + EPS)


def causal_depthwise_conv(x, w):
    K = w.shape[0]
    T = x.shape[1]
    xp = jnp.pad(x, ((0, 0), (K - 1, 0), (0, 0)))
    return sum(xp[:, j:j + T] * w[j] for j in range(K))


def chunk_gated_delta_rule(q, k, v, g, beta):
    f32 = jnp.float32
    B, H, T, dk = q.shape
    dv = v.shape[-1]
    n = T // CHUNK
    q = q.astype(f32) * (dk ** -0.5)
    k, v, g, beta = (t.astype(f32) for t in (k, v, g, beta))
    rs = lambda t: t.reshape((B, H, n, CHUNK) + t.shape[3:])
    q, k, v, g, beta = rs(q), rs(k), rs(v), rs(g), rs(beta)
    g = jnp.cumsum(g, axis=-1)
    idx = jnp.arange(CHUNK)
    lower = idx[:, None] >= idx[None, :]
    strict = idx[:, None] > idx[None, :]
    diff = g[..., :, None] - g[..., None, :]
    decay = jnp.where(lower, jnp.exp(jnp.where(lower, diff, 0.0)), 0.0)
    k_beta = k * beta[..., None]
    v_beta = v * beta[..., None]
    L = jnp.where(strict, jnp.einsum('bhnid,bhnjd->bhnij', k_beta, k) * decay, 0.0)
    eye = jnp.eye(CHUNK, dtype=f32)
    t_inv = lax.linalg.triangular_solve(eye + L, jnp.broadcast_to(eye, L.shape),
                                        left_side=True, lower=True, unit_diagonal=True)
    u = jnp.einsum('bhnij,bhnje->bhnie', t_inv, v_beta)
    w = jnp.einsum('bhnij,bhnjd->bhnid', t_inv, k_beta * jnp.exp(g)[..., None])
    attn_intra = jnp.where(lower, jnp.einsum('bhnid,bhnjd->bhnij', q, k) * decay, 0.0)
    g_last = g[..., -1]
    k_decay = k * jnp.exp(g_last[..., None] - g)[..., None]
    q_decay = q * jnp.exp(g)[..., None]

    def step(S, inp):
        w_c, u_c, qd_c, a_c, kd_c, gl_c = inp
        v_new = u_c - jnp.einsum('bhcd,bhde->bhce', w_c, S)
        o_c = (jnp.einsum('bhcd,bhde->bhce', qd_c, S)
               + jnp.einsum('bhij,bhje->bhie', a_c, v_new))
        S = S * jnp.exp(gl_c)[..., None, None] + jnp.einsum('bhcd,bhce->bhde', kd_c, v_new)
        return S, o_c

    xs = tuple(jnp.moveaxis(t, 2, 0) for t in (w, u, q_decay, attn_intra, k_decay, g_last))
    S0 = jnp.zeros((B, H, dk, dv), f32)
    _, o = lax.scan(step, S0, xs)
    return jnp.moveaxis(o, 0, 2).reshape(B, H, T, dv)


def diff_attention(q, k, v, lam):
    f32 = jnp.float32
    B, T, H, _, d = q.shape
    q = q.astype(f32) * (d ** -0.5)
    k = k.astype(f32)
    v = v.astype(f32)
    nb = T // Q_BLOCK
    q_blocks = q.reshape(B, nb, Q_BLOCK, H, 2, d).transpose(1, 0, 2, 3, 4, 5)
    k_pos = jnp.arange(T)

    def attend_block(args):
        q_blk, blk = args
        s = jnp.einsum('bqhcd,bkhcd->bhcqk', q_blk, k)
        q_pos = blk * Q_BLOCK + jnp.arange(Q_BLOCK)
        causal = k_pos[None, :] <= q_pos[:, None]
        p = jax.nn.softmax(jnp.where(causal, s, NEG_INF), axis=-1)
        attn = p[:, :, 0] - lam * p[:, :, 1]
        return jnp.einsum('bhqk,bkhe->bqhe', attn, v)

    o = lax.map(attend_block, (q_blocks, jnp.arange(nb)))
    return o.transpose(1, 0, 2, 3, 4).reshape(B, T, H, v.shape[-1])


def setup_inputs(seed: int = 0) -> dict:
    key = jax.random.key(seed)
    ks = jax.random.split(key, 20)
    nrm = lambda k, shape, s: jax.random.normal(k, shape, jnp.float32) * s
    dt = jnp.exp(jax.random.uniform(ks[5], (DN_HEADS,), jnp.float32,
                                    math.log(1e-3), math.log(1e-1)))
    return {
        "x": nrm(ks[0], (BATCH, SEQ, D_MODEL), 1.0),
        "norm_w": 1.0 + nrm(ks[1], (D_MODEL,), 0.02),
        "w_in": nrm(ks[2], (D_MODEL, N_IN), D_MODEL ** -0.5),
        "conv_w": nrm(ks[3], (CONV_WIDTH, 3 * DN_WIDTH), CONV_WIDTH ** -0.5),
        "a_log": jnp.log(jax.random.uniform(ks[4], (DN_HEADS,), jnp.float32, 1.0, 16.0)),
        "dt_bias": dt + jnp.log(-jnp.expm1(-dt)),
        "dn_norm_w": 1.0 + nrm(ks[6], (DN_HEAD_DIM,), 0.02),
        "q_norm_w": 1.0 + nrm(ks[7], (DA_QK_DIM,), 0.02),
        "k_norm_w": 1.0 + nrm(ks[8], (DA_QK_DIM,), 0.02),
        "lambda_q1": nrm(ks[9], (DA_QK_DIM,), 0.1),
        "lambda_k1": nrm(ks[10], (DA_QK_DIM,), 0.1),
        "lambda_q2": nrm(ks[11], (DA_QK_DIM,), 0.1),
        "lambda_k2": nrm(ks[12], (DA_QK_DIM,), 0.1),
        "da_norm_w": 1.0 + nrm(ks[13], (DA_V_DIM,), 0.02),
        "w_out_a": nrm(ks[14], (DN_WIDTH, D_MODEL), DN_WIDTH ** -0.5),
        "w_out_b": nrm(ks[15], (DA_WIDTH, D_MODEL), DA_WIDTH ** -0.5),
        "w_out": nrm(ks[16], (D_MODEL, D_MODEL), D_MODEL ** -0.5),
    }


def reference(x, norm_w, w_in, conv_w, a_log, dt_bias, dn_norm_w, q_norm_w, k_norm_w,
              lambda_q1, lambda_k1, lambda_q2, lambda_k2, da_norm_w,
              w_out_a, w_out_b, w_out):
    f32 = jnp.float32
    B, T, _ = x.shape
    for _layer in range(DEPTH):
        h = rms_norm(x, norm_w)
        proj = h @ w_in
        (qkv_a, z_a, a_in, b_in, q_b, k_b, v_b, z_b,
         gate_a, gate_b) = jnp.split(proj, IN_SPLIT_IDX, axis=-1)

        qkv_a = jax.nn.silu(causal_depthwise_conv(qkv_a, conv_w))
        q_a, k_a, v_a = jnp.split(qkv_a, 3, axis=-1)
        heads = lambda t: t.reshape(B, T, DN_HEADS, DN_HEAD_DIM).transpose(0, 2, 1, 3)
        q_a = l2norm(heads(q_a))
        k_a = l2norm(heads(k_a))
        v_a = heads(v_a)
        beta = jax.nn.sigmoid(b_in.astype(f32)).transpose(0, 2, 1)
        g = (-jnp.exp(a_log.astype(f32))
             * jax.nn.softplus(a_in.astype(f32) + dt_bias.astype(f32))).transpose(0, 2, 1)
        o_a = chunk_gated_delta_rule(q_a, k_a, v_a, g, beta)
        o_a = rms_norm(o_a.transpose(0, 2, 1, 3), dn_norm_w).astype(x.dtype)
        o_a = o_a.reshape(B, T, DN_WIDTH) * jax.nn.silu(z_a)
        y_a = o_a @ w_out_a

        q_b = rms_norm(q_b.reshape(B, T, DA_HEADS, 2, DA_QK_DIM), q_norm_w)
        k_b = rms_norm(k_b.reshape(B, T, DA_HEADS, 2, DA_QK_DIM), k_norm_w)
        v_b = v_b.reshape(B, T, DA_HEADS, DA_V_DIM)
        lam = (jnp.exp(jnp.sum(lambda_q1.astype(f32) * lambda_k1.astype(f32)))
               - jnp.exp(jnp.sum(lambda_q2.astype(f32) * lambda_k2.astype(f32)))
               + LAMBDA_INIT)
        o_b = diff_attention(q_b, k_b, v_b, lam)
        o_b = (rms_norm(o_b, da_norm_w) * (1.0 - LAMBDA_INIT)).astype(x.dtype)
        o_b = o_b.reshape(B, T, DA_WIDTH) * jax.nn.silu(z_b)
        y_b = o_b @ w_out_b

        y = jax.nn.sigmoid(gate_a) * y_a + jax.nn.sigmoid(gate_b) * y_b
        x = x + y @ w_out
    return x
```

```python
import functools
import math

import jax
import jax.numpy as jnp
from jax import lax
from jax.experimental import pallas as pl
from jax.experimental.pallas import tpu as pltpu

F32 = jnp.float32
BF16 = jnp.bfloat16

D_MODEL = 1024
DN_HEADS = 4
DN_HEAD_DIM = 128
DN_WIDTH = DN_HEADS * DN_HEAD_DIM
CONV_WIDTH = 4
CHUNK = 64
DA_HEADS = 4
DA_QK_DIM = 64
DA_V_DIM = 2 * DA_QK_DIM
DA_WIDTH = DA_HEADS * DA_V_DIM
DA_QK_WIDTH = DA_HEADS * 2 * DA_QK_DIM
LAMBDA_INIT = 0.8 - 0.6 * math.exp(-0.3 * 0)
EPS = 1e-6
NEG_INF = -1e30

LANES = 128
SUBLANES = 8
AB_ROWS = 16
VMEM_LIMIT = 48 * 1024 * 1024

TM_IN = 256
TB_DN = 256
TQ_DA = 512
TK_DA = 512
TM_OUT = 512

NT_DIMS = (((1,), (1,)), ((), ()))
TN_DIMS = (((0,), (0,)), ((), ()))


def _silu(v):
    return v * jax.nn.sigmoid(v)


def _softplus(v):
    return jnp.maximum(v, 0.0) + jnp.log1p(jnp.exp(-jnp.abs(v)))


def _resident(shape):
    zeros = (0,) * len(shape)
    return pl.BlockSpec(shape, lambda *_: zeros, pipeline_mode=pl.Buffered(1))


def _inproj_kernel(x_ref, nw_ref, wqkv_ref, convw_ref, wza_ref, wab_ref, wabt_ref,
                   alog_l_ref, dtb_l_ref, alog_s_ref, dtb_s_ref, qnw_ref, knw_ref,
                   wqb_ref, wkb_ref, wvb_ref, wzb_ref, wga_ref, wgb_ref,
                   qa_ref, ka_ref, va_ref, za_ref, gcol_ref, grow_ref,
                   qb_ref, kb_ref, vb_ref, zb_ref, ga_ref, gb_ref,
                   conv_buf, *, tiles_per_seq):
    tm = x_ref.shape[0]
    i = pl.program_id(0)

    x = x_ref[...]
    ms = jnp.mean(x * x, axis=-1, keepdims=True)
    h = (x * lax.rsqrt(ms + EPS) * nw_ref[...]).astype(BF16)

    @pl.when(i % tiles_per_seq == 0)
    def _():
        conv_buf[0:SUBLANES, :] = jnp.zeros((SUBLANES, 3 * DN_WIDTH), F32)

    head0 = SUBLANES - (CONV_WIDTH - 1)
    for s, out_ref in enumerate((qa_ref, ka_ref, va_ref)):
        c0 = s * DN_WIDTH
        pre = jnp.dot(h, wqkv_ref[:, c0:c0 + DN_WIDTH], preferred_element_type=F32)
        conv_buf[SUBLANES:SUBLANES + tm, c0:c0 + DN_WIDTH] = pre
        acc = convw_ref[CONV_WIDTH - 1:CONV_WIDTH, c0:c0 + DN_WIDTH] * pre
        for j in range(CONV_WIDTH - 1):
            acc = acc + (convw_ref[j:j + 1, c0:c0 + DN_WIDTH]
                         * conv_buf[head0 + j:head0 + j + tm, c0:c0 + DN_WIDTH])
        y = _silu(acc)
        if s == 2:
            out_ref[...] = y.astype(BF16)
        else:
            scale = DN_HEAD_DIM ** -0.5 if s == 0 else 1.0
            for hd in range(DN_HEADS):
                blk = y[:, hd * DN_HEAD_DIM:(hd + 1) * DN_HEAD_DIM]
                ss = jnp.sum(blk * blk, axis=-1, keepdims=True)
                out_ref[:, hd * DN_HEAD_DIM:(hd + 1) * DN_HEAD_DIM] = (
                    blk * (lax.rsqrt(ss + EPS) * scale)).astype(BF16)
    conv_buf[0:SUBLANES, :] = conv_buf[tm:tm + SUBLANES, :]

    za_ref[...] = _silu(jnp.dot(h, wza_ref[...], preferred_element_type=F32)).astype(BF16)

    ab = jnp.dot(h, wab_ref[...], preferred_element_type=F32)
    lane = lax.broadcasted_iota(jnp.int32, ab.shape, 1)
    g_l = -jnp.exp(alog_l_ref[...]) * _softplus(ab + dtb_l_ref[...])
    gcol_ref[...] = jnp.where(lane < DN_HEADS, g_l, jax.nn.sigmoid(ab))
    abt = lax.dot_general(wabt_ref[...], h, NT_DIMS, preferred_element_type=F32)
    row = lax.broadcasted_iota(jnp.int32, abt.shape, 0)
    g_s = -jnp.exp(alog_s_ref[...]) * _softplus(abt + dtb_s_ref[...])
    grow_ref[...] = jnp.where(row < DN_HEADS, g_s, jax.nn.sigmoid(abt))

    lane_b = lax.broadcasted_iota(jnp.int32, (tm, LANES), 1)
    lo = lane_b < DA_QK_DIM
    for w_ref, nrm_ref, out_ref, scale in ((wqb_ref, qnw_ref, qb_ref, DA_QK_DIM ** -0.5),
                                           (wkb_ref, knw_ref, kb_ref, 1.0)):
        p = jnp.dot(h, w_ref[...], preferred_element_type=F32)
        for hd in range(DA_HEADS):
            blk = p[:, hd * LANES:(hd + 1) * LANES]
            sq = blk * blk
            s_lo = jnp.sum(jnp.where(lo, sq, 0.0), axis=-1, keepdims=True)
            s_hi = jnp.sum(jnp.where(lo, 0.0, sq), axis=-1, keepdims=True)
            msq = jnp.where(lo, s_lo, s_hi) * (1.0 / DA_QK_DIM)
            out_ref[:, hd * LANES:(hd + 1) * LANES] = (
                blk * lax.rsqrt(msq + EPS) * nrm_ref[...] * scale).astype(BF16)

    vb_ref[...] = jnp.dot(h, wvb_ref[...], preferred_element_type=F32).astype(BF16)
    zb_ref[...] = _silu(jnp.dot(h, wzb_ref[...], preferred_element_type=F32)).astype(BF16)

    for w_ref, out_ref in ((wga_ref, ga_ref), (wgb_ref, gb_ref)):
        for c0 in range(0, D_MODEL, 512):
            gate = jnp.dot(h, w_ref[:, c0:c0 + 512], preferred_element_type=F32)
            out_ref[:, c0:c0 + 512] = jax.nn.sigmoid(gate).astype(BF16)


def _in_projection(x2, norm_w, w_in, conv_w, a_log, dt_bias, q_norm_w, k_norm_w, seq_len):
    m = x2.shape[0]
    tm = TM_IN
    assert m % tm == 0 and seq_len % tm == 0
    o = 0
    cols = {}
    for name, width in (("qkv", 3 * DN_WIDTH), ("za", DN_WIDTH), ("a", DN_HEADS), ("b", DN_HEADS),
                        ("qb", DA_QK_WIDTH), ("kb", DA_QK_WIDTH), ("vb", DA_WIDTH), ("zb", DA_WIDTH),
                        ("ga", D_MODEL), ("gb", D_MODEL)):
        cols[name] = w_in[:, o:o + width]
        o += width
    assert o == w_in.shape[1]
    wb = {k: v.astype(BF16) for k, v in cols.items()}
    w_ab = jnp.concatenate([cols["a"], cols["b"]], axis=1)
    w_ab_l = jnp.pad(w_ab, ((0, 0), (0, LANES - 2 * DN_HEADS))).astype(BF16)
    w_ab_s = jnp.pad(w_ab.T, ((0, AB_ROWS - 2 * DN_HEADS), (0, 0))).astype(BF16)
    alog_l = jnp.pad(a_log.astype(F32), (0, LANES - DN_HEADS)).reshape(1, LANES)
    dtb_l = jnp.pad(dt_bias.astype(F32), (0, LANES - DN_HEADS)).reshape(1, LANES)
    alog_s = jnp.pad(a_log.astype(F32), (0, AB_ROWS - DN_HEADS)).reshape(AB_ROWS, 1)
    dtb_s = jnp.pad(dt_bias.astype(F32), (0, AB_ROWS - DN_HEADS)).reshape(AB_ROWS, 1)
    qnw = jnp.tile(q_norm_w.astype(F32), 2).reshape(1, LANES)
    knw = jnp.tile(k_norm_w.astype(F32), 2).reshape(1, LANES)

    row_blk = lambda w: pl.BlockSpec((tm, w), lambda i: (i, 0))
    in_specs = [
        row_blk(D_MODEL), _resident((1, D_MODEL)),
        _resident((D_MODEL, 3 * DN_WIDTH)), _resident((CONV_WIDTH, 3 * DN_WIDTH)),
        _resident((D_MODEL, DN_WIDTH)), _resident((D_MODEL, LANES)), _resident((AB_ROWS, D_MODEL)),
        _resident((1, LANES)), _resident((1, LANES)), _resident((AB_ROWS, 1)), _resident((AB_ROWS, 1)),
        _resident((1, LANES)), _resident((1, LANES)),
        _resident((D_MODEL, DA_QK_WIDTH)), _resident((D_MODEL, DA_QK_WIDTH)),
        _resident((D_MODEL, DA_WIDTH)), _resident((D_MODEL, DA_WIDTH)),
        _resident((D_MODEL, D_MODEL)), _resident((D_MODEL, D_MODEL)),
    ]
    out_shapes = [jax.ShapeDtypeStruct((m, DN_WIDTH), BF16)] * 4 + [
        jax.ShapeDtypeStruct((m, LANES), F32), jax.ShapeDtypeStruct((AB_ROWS, m), F32)] + [
        jax.ShapeDtypeStruct((m, DA_WIDTH), BF16)] * 4 + [
        jax.ShapeDtypeStruct((m, D_MODEL), BF16)] * 2
    out_specs = [row_blk(DN_WIDTH)] * 4 + [
        row_blk(LANES), pl.BlockSpec((AB_ROWS, tm), lambda i: (0, i))] + [
        row_blk(DA_WIDTH)] * 4 + [row_blk(D_MODEL)] * 2
    return pl.pallas_call(
        functools.partial(_inproj_kernel, tiles_per_seq=seq_len // tm),
        grid=(m // tm,),
        in_specs=in_specs,
        out_specs=out_specs,
        out_shape=out_shapes,
        scratch_shapes=[pltpu.VMEM((tm + SUBLANES, 3 * DN_WIDTH), F32)],
        compiler_params=pltpu.CompilerParams(
            dimension_semantics=("arbitrary",), vmem_limit_bytes=VMEM_LIMIT),
        name="in_projection",
    )(x2, norm_w.astype(F32).reshape(1, D_MODEL), wb["qkv"], conv_w.astype(F32), wb["za"],
      w_ab_l, w_ab_s, alog_l, dtb_l, alog_s, dtb_s, qnw, knw,
      wb["qb"], wb["kb"], wb["vb"], wb["zb"], wb["ga"], wb["gb"])


def _unit_lower_inverse(l_strict, eye):
    n = l_strict.shape[0]
    inv = eye - l_strict
    power = l_strict
    k = 2
    while k < n:
        power = jnp.dot(power, power, precision=lax.Precision.HIGHEST, preferred_element_type=F32)
        inv = inv + jnp.dot(inv, power, precision=lax.Precision.HIGHEST, preferred_element_type=F32)
        k *= 2
    return inv


def _deltanet_kernel(q_ref, k_ref, v_ref, z_ref, gcol_ref, grow_ref, nw_ref, o_ref, state_ref):
    tb = q_ref.shape[0]
    c = CHUNK

    @pl.when(pl.program_id(1) == 0)
    def _():
        state_ref[...] = jnp.zeros(state_ref.shape, F32)

    ri = lax.broadcasted_iota(jnp.int32, (c, c), 0)
    ci = lax.broadcasted_iota(jnp.int32, (c, c), 1)
    lower = ri >= ci
    strict = ri > ci
    eye = (ri == ci).astype(F32)
    tri = lower.astype(F32)
    tri_t = (ri <= ci).astype(F32)

    for n in range(tb // c):
        r0 = n * c
        gb_col = gcol_ref[r0:r0 + c, :]
        gb_row = grow_ref[:, r0:r0 + c]
        gc_col = jnp.dot(tri, gb_col, precision=lax.Precision.HIGHEST, preferred_element_type=F32)
        gc_row = jnp.dot(gb_row, tri_t, precision=lax.Precision.HIGHEST, preferred_element_type=F32)
        for hd in range(DN_HEADS):
            hs = slice(hd * DN_HEAD_DIM, (hd + 1) * DN_HEAD_DIM)
            g_c = gc_col[:, hd:hd + 1]
            g_r = gc_row[hd:hd + 1, :]
            beta = gb_col[:, DN_HEADS + hd:DN_HEADS + hd + 1]
            g_last = g_c[c - 1:c, :]
            q = q_ref[r0:r0 + c, hs].astype(F32)
            k = k_ref[r0:r0 + c, hs].astype(F32)
            v = v_ref[r0:r0 + c, hs].astype(F32)

            diff = g_c - g_r
            decay = jnp.where(lower, jnp.exp(jnp.where(lower, diff, 0.0)), 0.0)
            k_beta = k * beta
            v_beta = v * beta
            kk = lax.dot_general(k_beta, k, NT_DIMS, preferred_element_type=F32)
            l_strict = jnp.where(strict, kk * decay, 0.0)
            t_inv = _unit_lower_inverse(l_strict, eye)
            u = jnp.dot(t_inv, v_beta, preferred_element_type=F32)
            w = jnp.dot(t_inv, k_beta * jnp.exp(g_c), preferred_element_type=F32)
            attn = jnp.where(lower, lax.dot_general(q, k, NT_DIMS, preferred_element_type=F32) * decay, 0.0)
            k_decay = k * jnp.exp(g_last - g_c)
            q_decay = q * jnp.exp(g_c)

            s = state_ref[hd]
            v_new = u - jnp.dot(w, s, preferred_element_type=F32)
            o = (jnp.dot(q_decay, s, preferred_element_type=F32)
                 + jnp.dot(attn, v_new, preferred_element_type=F32))
            state_ref[hd] = s * jnp.exp(g_last) + lax.dot_general(
                k_decay, v_new, TN_DIMS, preferred_element_type=F32)

            ms = jnp.mean(o * o, axis=-1, keepdims=True)
            o = o * lax.rsqrt(ms + EPS) * nw_ref[...]
            o_ref[r0:r0 + c, hs] = (o * z_ref[r0:r0 + c, hs].astype(F32)).astype(BF16)


def _deltanet(qa, ka, va, za, gcol, grow, dn_norm_w, batch, seq_len):
    tb = TB_DN
    assert seq_len % tb == 0 and tb % CHUNK == 0
    nt = seq_len // tb
    row_blk = lambda w: pl.BlockSpec((tb, w), lambda b, t: (b * nt + t, 0))
    return pl.pallas_call(
        _deltanet_kernel,
        grid=(batch, nt),
        in_specs=[row_blk(DN_WIDTH)] * 4 + [
            row_blk(LANES), pl.BlockSpec((AB_ROWS, tb), lambda b, t: (0, b * nt + t)),
            _resident((1, DN_HEAD_DIM))],
        out_specs=row_blk(DN_WIDTH),
        out_shape=jax.ShapeDtypeStruct((batch * seq_len, DN_WIDTH), BF16),
        scratch_shapes=[pltpu.VMEM((DN_HEADS, DN_HEAD_DIM, DN_HEAD_DIM), F32)],
        compiler_params=pltpu.CompilerParams(
            dimension_semantics=("arbitrary", "arbitrary"), vmem_limit_bytes=VMEM_LIMIT),
        name="gated_deltanet",
    )(qa, ka, va, za, gcol, grow, dn_norm_w.astype(F32).reshape(1, DN_HEAD_DIM))


def _diffattn_kernel(lq1_ref, lk1_ref, lq2_ref, lk2_ref, q_ref, k_ref, v_ref, z_ref, nw_ref,
                     o_ref, m_ref, l_ref, acc_ref):
    tq = q_ref.shape[0]
    tk = TK_DA
    qi = pl.program_id(2)

    lane = lax.broadcasted_iota(jnp.int32, (1, LANES), 1)
    q = q_ref[...]
    zero = jnp.zeros_like(q)
    q_comp = (jnp.where(lane < DA_QK_DIM, q, zero), jnp.where(lane < DA_QK_DIM, zero, q))

    m_ref[...] = jnp.full(m_ref.shape, NEG_INF, F32)
    l_ref[...] = jnp.zeros(l_ref.shape, F32)
    acc_ref[...] = jnp.zeros(acc_ref.shape, F32)

    def attend(j, masked):
        start = pl.multiple_of(j * tk, tk)
        kblk = k_ref[pl.ds(start, tk), :]
        vblk = v_ref[pl.ds(start, tk), :]
        for comp in range(2):
            s = lax.dot_general(q_comp[comp], kblk, NT_DIMS, preferred_element_type=F32)
            if masked:
                rq = lax.broadcasted_iota(jnp.int32, (tq, tk), 0)
                ck = lax.broadcasted_iota(jnp.int32, (tq, tk), 1)
                s = jnp.where(ck <= rq, s, NEG_INF)
            m_prev = m_ref[comp]
            m_new = jnp.maximum(m_prev, jnp.max(s, axis=-1, keepdims=True))
            alpha = jnp.exp(m_prev - m_new)
            p = jnp.exp(s - m_new)
            l_ref[comp] = alpha * l_ref[comp] + jnp.sum(p, axis=-1, keepdims=True)
            acc_ref[comp] = alpha * acc_ref[comp] + jnp.dot(
                p.astype(BF16), vblk, preferred_element_type=F32)
            m_ref[comp] = m_new

    def body(j, carry):
        attend(j, False)
        return carry

    lax.fori_loop(0, qi, body, 0)
    attend(qi, True)

    lam = (jnp.exp(jnp.sum(lq1_ref[...] * lk1_ref[...], axis=-1, keepdims=True))
           - jnp.exp(jnp.sum(lq2_ref[...] * lk2_ref[...], axis=-1, keepdims=True))
           + LAMBDA_INIT)
    o = acc_ref[0] / l_ref[0] - lam * (acc_ref[1] / l_ref[1])
    ms = jnp.mean(o * o, axis=-1, keepdims=True)
    o = o * lax.rsqrt(ms + EPS) * nw_ref[...] * (1.0 - LAMBDA_INIT)
    o_ref[...] = (o * z_ref[...].astype(F32)).astype(BF16)


def _diff_attention(qb, kb, vb, zb, lambda_q1, lambda_k1, lambda_q2, lambda_k2, da_norm_w,
                    batch, seq_len):
    tq = TQ_DA
    assert tq == TK_DA and seq_len % tq == 0
    nq = seq_len // tq
    lam_vec = lambda v: v.astype(F32).reshape(1, DA_QK_DIM)
    q_blk = pl.BlockSpec((tq, LANES), lambda b, h, i: (b * nq + i, h))
    kv_blk = pl.BlockSpec((seq_len, LANES), lambda b, h, i: (b, h))
    return pl.pallas_call(
        _diffattn_kernel,
        grid=(batch, DA_HEADS, nq),
        in_specs=[_resident((1, DA_QK_DIM))] * 4 + [q_blk, kv_blk, kv_blk, q_blk,
                                                    _resident((1, DA_V_DIM))],
        out_specs=q_blk,
        out_shape=jax.ShapeDtypeStruct((batch * seq_len, DA_WIDTH), BF16),
        scratch_shapes=[pltpu.VMEM((2, tq, 1), F32), pltpu.VMEM((2, tq, 1), F32),
                        pltpu.VMEM((2, tq, DA_V_DIM), F32)],
        compiler_params=pltpu.CompilerParams(
            dimension_semantics=("arbitrary", "arbitrary", "arbitrary"),
            vmem_limit_bytes=VMEM_LIMIT),
        name="diff_attention",
    )(lam_vec(lambda_q1), lam_vec(lambda_k1), lam_vec(lambda_q2), lam_vec(lambda_k2),
      qb, kb, vb, zb, da_norm_w.astype(F32).reshape(1, DA_V_DIM))


def _output_kernel(oa_ref, ob_ref, ga_ref, gb_ref, x_ref, woa_ref, wob_ref, wo_ref, out_ref):
    y_a = jnp.dot(oa_ref[...], woa_ref[...], preferred_element_type=F32)
    y_b = jnp.dot(ob_ref[...], wob_ref[...], preferred_element_type=F32)
    y = ga_ref[...].astype(F32) * y_a + gb_ref[...].astype(F32) * y_b
    out_ref[...] = x_ref[...] + jnp.dot(y.astype(BF16), wo_ref[...], preferred_element_type=F32)


def _output_projection(oa, ob, ga, gb, x2, w_out_a, w_out_b, w_out):
    m = x2.shape[0]
    tm = TM_OUT
    assert m % tm == 0
    row_blk = lambda w: pl.BlockSpec((tm, w), lambda i: (i, 0))
    return pl.pallas_call(
        _output_kernel,
        grid=(m // tm,),
        in_specs=[row_blk(DN_WIDTH), row_blk(DA_WIDTH), row_blk(D_MODEL), row_blk(D_MODEL),
                  row_blk(D_MODEL), _resident((DN_WIDTH, D_MODEL)), _resident((DA_WIDTH, D_MODEL)),
                  _resident((D_MODEL, D_MODEL))],
        out_specs=row_blk(D_MODEL),
        out_shape=jax.ShapeDtypeStruct((m, D_MODEL), F32),
        compiler_params=pltpu.CompilerParams(
            dimension_semantics=("arbitrary",), vmem_limit_bytes=VMEM_LIMIT),
        name="output_projection",
    )(oa, ob, ga, gb, x2, w_out_a.astype(BF16), w_out_b.astype(BF16), w_out.astype(BF16))


def kernel(x, norm_w, w_in, conv_w, a_log, dt_bias, dn_norm_w, q_norm_w, k_norm_w,
           lambda_q1, lambda_k1, lambda_q2, lambda_k2, da_norm_w, w_out_a, w_out_b, w_out):
    batch, seq_len, d_model = x.shape
    assert d_model == D_MODEL
    x2 = x.reshape(batch * seq_len, d_model)
    (qa, ka, va, za, gcol, grow, qb, kb, vb, zb, ga, gb) = _in_projection(
        x2, norm_w, w_in, conv_w, a_log, dt_bias, q_norm_w, k_norm_w, seq_len)
    oa = _deltanet(qa, ka, va, za, gcol, grow, dn_norm_w, batch, seq_len)
    ob = _diff_attention(qb, kb, vb, zb, lambda_q1, lambda_k1, lambda_q2, lambda_k2, da_norm_w,
                         batch, seq_len)
    out = _output_projection(oa, ob, ga, gb, x2, w_out_a, w_out_b, w_out)
    return out.reshape(batch, seq_len, d_model)
```

```python
import functools
import math

import jax
import jax.numpy as jnp
from jax import lax
from jax.experimental import pallas as pl
from jax.experimental.pallas import tpu as pltpu

F32 = jnp.float32
BF16 = jnp.bfloat16

D_MODEL = 1024
DN_HEADS = 4
DN_HEAD_DIM = 128
DN_WIDTH = DN_HEADS * DN_HEAD_DIM
CONV_WIDTH = 4
CHUNK = 64
DA_HEADS = 4
DA_QK_DIM = 64
DA_V_DIM = 2 * DA_QK_DIM
DA_WIDTH = DA_HEADS * DA_V_DIM
DA_QK_WIDTH = DA_HEADS * 2 * DA_QK_DIM
LAMBDA_INIT = 0.8 - 0.6 * math.exp(-0.3 * 0)
EPS = 1e-6
NEG_INF = -1e30
LOG2_E = math.log2(math.e)

LANES = 128
SUBLANES = 8
AB_ROWS = 16
VMEM_LIMIT = 48 * 1024 * 1024

TM_IN = 256
TB_DN = 512
TQ_DA = 512
TK_DA = 512
TM_OUT = 512

NT_DIMS = (((1,), (1,)), ((), ()))
TN_DIMS = (((0,), (0,)), ((), ()))


def _silu(v):
    return v * jax.nn.sigmoid(v)


def _softplus(v):
    return jnp.maximum(v, 0.0) + jnp.log1p(jnp.exp(-jnp.abs(v)))


def _resident(shape):
    zeros = (0,) * len(shape)
    return pl.BlockSpec(shape, lambda *_: zeros, pipeline_mode=pl.Buffered(1))


def _inproj_kernel(x_ref, nw_ref, wqkv_ref, convw_ref, wza_ref, wab_ref, wabt_ref,
                   alog_l_ref, dtb_l_ref, alog_s_ref, dtb_s_ref, qnw_ref, knw_ref,
                   wqb_ref, wkb_ref, wvb_ref, wzb_ref, wga_ref, wgb_ref,
                   qa_ref, ka_ref, va_ref, za_ref, gcol_ref, grow_ref,
                   qb_ref, kb_ref, vb_ref, zb_ref, ga_ref, gb_ref,
                   conv_buf, *, tiles_per_seq):
    tm = x_ref.shape[0]
    i = pl.program_id(0)

    x = x_ref[...]
    ms = jnp.mean(x * x, axis=-1, keepdims=True)
    h = (x * lax.rsqrt(ms + EPS) * nw_ref[...]).astype(BF16)

    @pl.when(i % tiles_per_seq == 0)
    def _():
        conv_buf[0:SUBLANES, :] = jnp.zeros((SUBLANES, 3 * DN_WIDTH), F32)

    head0 = SUBLANES - (CONV_WIDTH - 1)
    for s, out_ref in enumerate((qa_ref, ka_ref, va_ref)):
        c0 = s * DN_WIDTH
        pre = jnp.dot(h, wqkv_ref[:, c0:c0 + DN_WIDTH], preferred_element_type=F32)
        conv_buf[SUBLANES:SUBLANES + tm, c0:c0 + DN_WIDTH] = pre
        acc = convw_ref[CONV_WIDTH - 1:CONV_WIDTH, c0:c0 + DN_WIDTH] * pre
        for j in range(CONV_WIDTH - 1):
            acc = acc + (convw_ref[j:j + 1, c0:c0 + DN_WIDTH]
                         * conv_buf[head0 + j:head0 + j + tm, c0:c0 + DN_WIDTH])
        y = _silu(acc)
        if s == 2:
            out_ref[...] = y.astype(BF16)
        else:
            scale = DN_HEAD_DIM ** -0.5 if s == 0 else 1.0
            for hd in range(DN_HEADS):
                blk = y[:, hd * DN_HEAD_DIM:(hd + 1) * DN_HEAD_DIM]
                ss = jnp.sum(blk * blk, axis=-1, keepdims=True)
                out_ref[:, hd * DN_HEAD_DIM:(hd + 1) * DN_HEAD_DIM] = (
                    blk * (lax.rsqrt(ss + EPS) * scale)).astype(BF16)
    conv_buf[0:SUBLANES, :] = conv_buf[tm:tm + SUBLANES, :]

    za_ref[...] = _silu(jnp.dot(h, wza_ref[...], preferred_element_type=F32)).astype(BF16)

    ab = jnp.dot(h, wab_ref[...], preferred_element_type=F32)
    lane = lax.broadcasted_iota(jnp.int32, ab.shape, 1)
    g_l = -jnp.exp(alog_l_ref[...]) * _softplus(ab + dtb_l_ref[...])
    gcol_ref[...] = jnp.where(lane < DN_HEADS, g_l, jax.nn.sigmoid(ab))
    abt = lax.dot_general(wabt_ref[...], h, NT_DIMS, preferred_element_type=F32)
    row = lax.broadcasted_iota(jnp.int32, abt.shape, 0)
    g_s = -jnp.exp(alog_s_ref[...]) * _softplus(abt + dtb_s_ref[...])
    grow_ref[...] = jnp.where(row < DN_HEADS, g_s, jax.nn.sigmoid(abt))

    lane_b = lax.broadcasted_iota(jnp.int32, (tm, LANES), 1)
    lo = lane_b < DA_QK_DIM
    for w_ref, nrm_ref, out_ref, scale in ((wqb_ref, qnw_ref, qb_ref, DA_QK_DIM ** -0.5 * LOG2_E),
                                           (wkb_ref, knw_ref, kb_ref, 1.0)):
        p = jnp.dot(h, w_ref[...], preferred_element_type=F32)
        for hd in range(DA_HEADS):
            blk = p[:, hd * LANES:(hd + 1) * LANES]
            sq = blk * blk
            s_lo = jnp.sum(jnp.where(lo, sq, 0.0), axis=-1, keepdims=True)
            s_hi = jnp.sum(jnp.where(lo, 0.0, sq), axis=-1, keepdims=True)
            msq = jnp.where(lo, s_lo, s_hi) * (1.0 / DA_QK_DIM)
            out_ref[:, hd * LANES:(hd + 1) * LANES] = (
                blk * lax.rsqrt(msq + EPS) * nrm_ref[...] * scale).astype(BF16)

    vb_ref[...] = jnp.dot(h, wvb_ref[...], preferred_element_type=F32).astype(BF16)
    zb_ref[...] = _silu(jnp.dot(h, wzb_ref[...], preferred_element_type=F32)).astype(BF16)

    for w_ref, out_ref in ((wga_ref, ga_ref), (wgb_ref, gb_ref)):
        for c0 in range(0, D_MODEL, 512):
            gate = jnp.dot(h, w_ref[:, c0:c0 + 512], preferred_element_type=F32)
            out_ref[:, c0:c0 + 512] = jax.nn.sigmoid(gate).astype(BF16)


def _in_projection(x2, norm_w, w_in, conv_w, a_log, dt_bias, q_norm_w, k_norm_w, seq_len):
    m = x2.shape[0]
    tm = TM_IN
    assert m % tm == 0 and seq_len % tm == 0
    o = 0
    cols = {}
    for name, width in (("qkv", 3 * DN_WIDTH), ("za", DN_WIDTH), ("a", DN_HEADS), ("b", DN_HEADS),
                        ("qb", DA_QK_WIDTH), ("kb", DA_QK_WIDTH), ("vb", DA_WIDTH), ("zb", DA_WIDTH),
                        ("ga", D_MODEL), ("gb", D_MODEL)):
        cols[name] = w_in[:, o:o + width]
        o += width
    assert o == w_in.shape[1]
    wb = {k: v.astype(BF16) for k, v in cols.items()}
    w_ab = jnp.concatenate([cols["a"], cols["b"]], axis=1)
    w_ab_l = jnp.pad(w_ab, ((0, 0), (0, LANES - 2 * DN_HEADS))).astype(BF16)
    w_ab_s = jnp.pad(w_ab.T, ((0, AB_ROWS - 2 * DN_HEADS), (0, 0))).astype(BF16)
    alog_l = jnp.pad(a_log.astype(F32), (0, LANES - DN_HEADS)).reshape(1, LANES)
    dtb_l = jnp.pad(dt_bias.astype(F32), (0, LANES - DN_HEADS)).reshape(1, LANES)
    alog_s = jnp.pad(a_log.astype(F32), (0, AB_ROWS - DN_HEADS)).reshape(AB_ROWS, 1)
    dtb_s = jnp.pad(dt_bias.astype(F32), (0, AB_ROWS - DN_HEADS)).reshape(AB_ROWS, 1)
    qnw = jnp.tile(q_norm_w.astype(F32), 2).reshape(1, LANES)
    knw = jnp.tile(k_norm_w.astype(F32), 2).reshape(1, LANES)

    row_blk = lambda w: pl.BlockSpec((tm, w), lambda i: (i, 0))
    in_specs = [
        row_blk(D_MODEL), _resident((1, D_MODEL)),
        _resident((D_MODEL, 3 * DN_WIDTH)), _resident((CONV_WIDTH, 3 * DN_WIDTH)),
        _resident((D_MODEL, DN_WIDTH)), _resident((D_MODEL, LANES)), _resident((AB_ROWS, D_MODEL)),
        _resident((1, LANES)), _resident((1, LANES)), _resident((AB_ROWS, 1)), _resident((AB_ROWS, 1)),
        _resident((1, LANES)), _resident((1, LANES)),
        _resident((D_MODEL, DA_QK_WIDTH)), _resident((D_MODEL, DA_QK_WIDTH)),
        _resident((D_MODEL, DA_WIDTH)), _resident((D_MODEL, DA_WIDTH)),
        _resident((D_MODEL, D_MODEL)), _resident((D_MODEL, D_MODEL)),
    ]
    out_shapes = [jax.ShapeDtypeStruct((m, DN_WIDTH), BF16)] * 4 + [
        jax.ShapeDtypeStruct((m, LANES), F32), jax.ShapeDtypeStruct((AB_ROWS, m), F32)] + [
        jax.ShapeDtypeStruct((m, DA_WIDTH), BF16)] * 4 + [
        jax.ShapeDtypeStruct((m, D_MODEL), BF16)] * 2
    out_specs = [row_blk(DN_WIDTH)] * 4 + [
        row_blk(LANES), pl.BlockSpec((AB_ROWS, tm), lambda i: (0, i))] + [
        row_blk(DA_WIDTH)] * 4 + [row_blk(D_MODEL)] * 2
    return pl.pallas_call(
        functools.partial(_inproj_kernel, tiles_per_seq=seq_len // tm),
        grid=(m // tm,),
        in_specs=in_specs,
        out_specs=out_specs,
        out_shape=out_shapes,
        scratch_shapes=[pltpu.VMEM((tm + SUBLANES, 3 * DN_WIDTH), F32)],
        compiler_params=pltpu.CompilerParams(
            dimension_semantics=("arbitrary",), vmem_limit_bytes=VMEM_LIMIT),
        name="in_projection",
    )(x2, norm_w.astype(F32).reshape(1, D_MODEL), wb["qkv"], conv_w.astype(F32), wb["za"],
      w_ab_l, w_ab_s, alog_l, dtb_l, alog_s, dtb_s, qnw, knw,
      wb["qb"], wb["kb"], wb["vb"], wb["zb"], wb["ga"], wb["gb"])


def _bmm(a, b):
    return jnp.einsum('nij,njk->nik', a, b, preferred_element_type=F32)


def _bmm_nt(a, b):
    return jnp.einsum('nid,njd->nij', a, b, preferred_element_type=F32)


def _split_bf16(x, parts):
    out = []
    for _ in range(parts - 1):
        hi = x.astype(BF16)
        out.append(hi)
        x = x - hi.astype(F32)
    out.append(x.astype(BF16))
    return out


def _unit_lower_inverse(l_strict, eye):
    n = l_strict.shape[-1]
    inv = eye - l_strict
    power_b = l_strict.astype(BF16)
    k = 2
    while k < n:
        power_b = _bmm(power_b, power_b).astype(BF16)
        inv = inv + _bmm(inv.astype(BF16), power_b)
        k *= 2
    return inv


def _deltanet_kernel(q_ref, k_ref, v_ref, z_ref, gcol_ref, grow_ref, nw_ref, o_ref,
                     state_ref, oraw_ref):
    tb = q_ref.shape[0]
    c = CHUNK
    nc = tb // c
    nh = DN_HEADS

    @pl.when(pl.program_id(1) == 0)
    def _():
        state_ref[...] = jnp.zeros(state_ref.shape, F32)

    ri = lax.broadcasted_iota(jnp.int32, (c, c), 0)
    ci = lax.broadcasted_iota(jnp.int32, (c, c), 1)
    lower = ri >= ci
    strict = ri > ci
    eye = (ri == ci).astype(F32)
    tri = lower.astype(BF16)
    tri_t = (ri <= ci).astype(BF16)

    def heads_major(ref):
        return jnp.concatenate(
            [ref[:, hd * DN_HEAD_DIM:(hd + 1) * DN_HEAD_DIM].reshape(nc, c, DN_HEAD_DIM)
             for hd in range(nh)], axis=0)

    gb_col = gcol_ref[...].reshape(nc, c, LANES)
    tri_b = jnp.broadcast_to(tri, (nc, c, c))
    gc_col = sum(_bmm(tri_b, part) for part in _split_bf16(gb_col, 3))
    gb_row = grow_ref[...]
    gc_row = sum(jnp.dot(part, tri_t, preferred_element_type=F32)
                 for part in _split_bf16(gb_row.reshape(nc * AB_ROWS, c), 3)
                 ).reshape(nc, AB_ROWS, c)

    def col_form(src, lane):
        return jnp.concatenate(
            [jnp.broadcast_to(src[:, :, lane + hd:lane + hd + 1], (nc, c, LANES))
             for hd in range(nh)], axis=0)

    def row_form(src, row):
        return jnp.concatenate([src[:, row + hd:row + hd + 1, :] for hd in range(nh)], axis=0)

    g_c = col_form(gc_col, 0)
    beta_c = col_form(gb_col, nh)
    g_r = row_form(gc_row, 0)
    beta_r = row_form(gb_row, nh)
    g_last = g_c[:, c - 1:c, :]

    q = heads_major(q_ref)
    k = heads_major(k_ref)
    v = heads_major(v_ref)

    diff = g_c[:, :, :c] - g_r
    decay = jnp.where(lower, jnp.exp(jnp.where(lower, diff, 0.0)), 0.0)
    l_strict = jnp.where(strict, _bmm_nt(k, k) * decay * beta_c[:, :, :c], 0.0)
    attn = jnp.where(lower, _bmm_nt(q, k) * decay, 0.0).astype(BF16)
    t_beta = _unit_lower_inverse(l_strict, eye) * beta_r
    u = _bmm(t_beta.astype(BF16), v)
    w = _bmm((t_beta * jnp.exp(g_r)).astype(BF16), k).astype(BF16)
    q_decay = (q.astype(F32) * jnp.exp(g_c)).astype(BF16)
    k_decay = (k.astype(F32) * jnp.exp(g_last - g_c)).astype(BF16)
    s_decay = jnp.exp(g_last)

    for n in range(nc):
        idx = [hd * nc + n for hd in range(nh)]
        s_old = [state_ref[hd] for hd in range(nh)]
        ws_qs = [jnp.dot(jnp.concatenate([w[b], q_decay[b]], axis=0), s_old[hd].astype(BF16),
                         preferred_element_type=F32) for hd, b in enumerate(idx)]
        v_new = [(u[b] - ws_qs[hd][:c]).astype(BF16) for hd, b in enumerate(idx)]
        for hd, b in enumerate(idx):
            oraw_ref[n * c:(n + 1) * c, hd * DN_HEAD_DIM:(hd + 1) * DN_HEAD_DIM] = (
                ws_qs[hd][c:] + jnp.dot(attn[b], v_new[hd], preferred_element_type=F32))
        for hd, b in enumerate(idx):
            state_ref[hd] = s_old[hd] * s_decay[b] + lax.dot_general(
                k_decay[b], v_new[hd], TN_DIMS, preferred_element_type=F32)

    for hd in range(nh):
        hs = slice(hd * DN_HEAD_DIM, (hd + 1) * DN_HEAD_DIM)
        o = oraw_ref[:, hs]
        ms = jnp.mean(o * o, axis=-1, keepdims=True)
        o = o * lax.rsqrt(ms + EPS) * nw_ref[...]
        o_ref[:, hs] = (o * z_ref[:, hs].astype(F32)).astype(BF16)


def _deltanet(qa, ka, va, za, gcol, grow, dn_norm_w, batch, seq_len):
    tb = TB_DN
    assert seq_len % tb == 0 and tb % CHUNK == 0
    nt = seq_len // tb
    nc = tb // CHUNK
    grow3 = grow.reshape(AB_ROWS, -1, CHUNK).transpose(1, 0, 2)
    row_blk = lambda w: pl.BlockSpec((tb, w), lambda b, t: (b * nt + t, 0))
    return pl.pallas_call(
        _deltanet_kernel,
        grid=(batch, nt),
        in_specs=[row_blk(DN_WIDTH)] * 4 + [
            row_blk(LANES),
            pl.BlockSpec((nc, AB_ROWS, CHUNK), lambda b, t: (b * nt + t, 0, 0)),
            _resident((1, DN_HEAD_DIM))],
        out_specs=row_blk(DN_WIDTH),
        out_shape=jax.ShapeDtypeStruct((batch * seq_len, DN_WIDTH), BF16),
        scratch_shapes=[pltpu.VMEM((DN_HEADS, DN_HEAD_DIM, DN_HEAD_DIM), F32),
                        pltpu.VMEM((tb, DN_WIDTH), F32)],
        compiler_params=pltpu.CompilerParams(
            dimension_semantics=("arbitrary", "arbitrary"), vmem_limit_bytes=VMEM_LIMIT),
        name="gated_deltanet",
    )(qa, ka, va, za, gcol, grow3, dn_norm_w.astype(F32).reshape(1, DN_HEAD_DIM))


def _diffattn_kernel(lq1_ref, lk1_ref, lq2_ref, lk2_ref, q_ref, k_ref, v_ref, z_ref, nw_ref,
                     o_ref, m_ref, l_ref, acc_ref):
    tq = q_ref.shape[0]
    tk = TK_DA
    qi = pl.program_id(2)

    lane = lax.broadcasted_iota(jnp.int32, (1, LANES), 1)
    q = q_ref[...]
    zero = jnp.zeros_like(q)
    q_comp = (jnp.where(lane < DA_QK_DIM, q, zero), jnp.where(lane < DA_QK_DIM, zero, q))

    m_ref[...] = jnp.full(m_ref.shape, NEG_INF, F32)
    l_ref[...] = jnp.zeros(l_ref.shape, F32)
    acc_ref[...] = jnp.zeros(acc_ref.shape, F32)

    def attend(j, masked):
        start = pl.multiple_of(j * tk, tk)
        kblk = k_ref[pl.ds(start, tk), :]
        vblk = v_ref[pl.ds(start, tk), :]
        scores = [lax.dot_general(q_comp[comp], kblk, NT_DIMS, preferred_element_type=F32)
                  for comp in range(2)]
        if masked:
            rq = lax.broadcasted_iota(jnp.int32, (tq, tk), 0)
            ck = lax.broadcasted_iota(jnp.int32, (tq, tk), 1)
            scores = [jnp.where(ck <= rq, s, NEG_INF) for s in scores]
        probs = []
        for comp in range(2):
            s = scores[comp]
            m_prev = m_ref[comp]
            m_new = jnp.maximum(m_prev, jnp.max(s, axis=-1, keepdims=True))
            alpha = jnp.exp2(m_prev - m_new)
            p = jnp.exp2(s - jnp.concatenate([m_new] * (tk // LANES), axis=1))
            l_ref[comp] = alpha * l_ref[comp] + jnp.sum(p, axis=-1, keepdims=True)
            acc_ref[comp] = alpha * acc_ref[comp]
            m_ref[comp] = m_new
            probs.append(p.astype(BF16))
        for comp in range(2):
            acc_ref[comp] += jnp.dot(probs[comp], vblk, preferred_element_type=F32)

    def body(j, carry):
        attend(j, False)
        return carry

    lax.fori_loop(0, qi, body, 0)
    attend(qi, True)

    lam = (jnp.exp(jnp.sum(lq1_ref[...] * lk1_ref[...], axis=-1, keepdims=True))
           - jnp.exp(jnp.sum(lq2_ref[...] * lk2_ref[...], axis=-1, keepdims=True))
           + LAMBDA_INIT)
    o = acc_ref[0] / l_ref[0] - lam * (acc_ref[1] / l_ref[1])
    ms = jnp.mean(o * o, axis=-1, keepdims=True)
    o = o * lax.rsqrt(ms + EPS) * nw_ref[...] * (1.0 - LAMBDA_INIT)
    o_ref[...] = (o * z_ref[...].astype(F32)).astype(BF16)


def _diff_attention(qb, kb, vb, zb, lambda_q1, lambda_k1, lambda_q2, lambda_k2, da_norm_w,
                    batch, seq_len):
    tq = TQ_DA
    assert tq == TK_DA and seq_len % tq == 0
    nq = seq_len // tq
    lam_vec = lambda v: v.astype(F32).reshape(1, DA_QK_DIM)
    q_blk = pl.BlockSpec((tq, LANES), lambda b, h, i: (b * nq + i, h))
    kv_blk = pl.BlockSpec((seq_len, LANES), lambda b, h, i: (b, h))
    return pl.pallas_call(
        _diffattn_kernel,
        grid=(batch, DA_HEADS, nq),
        in_specs=[_resident((1, DA_QK_DIM))] * 4 + [q_blk, kv_blk, kv_blk, q_blk,
                                                    _resident((1, DA_V_DIM))],
        out_specs=q_blk,
        out_shape=jax.ShapeDtypeStruct((batch * seq_len, DA_WIDTH), BF16),
        scratch_shapes=[pltpu.VMEM((2, tq, LANES), F32), pltpu.VMEM((2, tq, LANES), F32),
                        pltpu.VMEM((2, tq, DA_V_DIM), F32)],
        compiler_params=pltpu.CompilerParams(
            dimension_semantics=("arbitrary", "arbitrary", "arbitrary"),
            vmem_limit_bytes=VMEM_LIMIT),
        name="diff_attention",
    )(lam_vec(lambda_q1), lam_vec(lambda_k1), lam_vec(lambda_q2), lam_vec(lambda_k2),
      qb, kb, vb, zb, da_norm_w.astype(F32).reshape(1, DA_V_DIM))


def _output_kernel(oa_ref, ob_ref, ga_ref, gb_ref, x_ref, woa_ref, wob_ref, wo_ref, out_ref):
    y_a = jnp.dot(oa_ref[...], woa_ref[...], preferred_element_type=F32)
    y_b = jnp.dot(ob_ref[...], wob_ref[...], preferred_element_type=F32)
    y = ga_ref[...].astype(F32) * y_a + gb_ref[...].astype(F32) * y_b
    out_ref[...] = x_ref[...] + jnp.dot(y.astype(BF16), wo_ref[...], preferred_element_type=F32)


def _output_projection(oa, ob, ga, gb, x2, w_out_a, w_out_b, w_out):
    m = x2.shape[0]
    tm = TM_OUT
    assert m % tm == 0
    row_blk = lambda w: pl.BlockSpec((tm, w), lambda i: (i, 0))
    return pl.pallas_call(
        _output_kernel,
        grid=(m // tm,),
        in_specs=[row_blk(DN_WIDTH), row_blk(DA_WIDTH), row_blk(D_MODEL), row_blk(D_MODEL),
                  row_blk(D_MODEL), _resident((DN_WIDTH, D_MODEL)), _resident((DA_WIDTH, D_MODEL)),
                  _resident((D_MODEL, D_MODEL))],
        out_specs=row_blk(D_MODEL),
        out_shape=jax.ShapeDtypeStruct((m, D_MODEL), F32),
        compiler_params=pltpu.CompilerParams(
            dimension_semantics=("arbitrary",), vmem_limit_bytes=VMEM_LIMIT),
        name="output_projection",
    )(oa, ob, ga, gb, x2, w_out_a.astype(BF16), w_out_b.astype(BF16), w_out.astype(BF16))


def kernel(x, norm_w, w_in, conv_w, a_log, dt_bias, dn_norm_w, q_norm_w, k_norm_w,
           lambda_q1, lambda_k1, lambda_q2, lambda_k2, da_norm_w, w_out_a, w_out_b, w_out):
    batch, seq_len, d_model = x.shape
    assert d_model == D_MODEL
    x2 = x.reshape(batch * seq_len, d_model)
    (qa, ka, va, za, gcol, grow, qb, kb, vb, zb, ga, gb) = _in_projection(
        x2, norm_w, w_in, conv_w, a_log, dt_bias, q_norm_w, k_norm_w, seq_len)
    oa = _deltanet(qa, ka, va, za, gcol, grow, dn_norm_w, batch, seq_len)
    ob = _diff_attention(qb, kb, vb, zb, lambda_q1, lambda_k1, lambda_q2, lambda_k2, da_norm_w,
                         batch, seq_len)
    out = _output_projection(oa, ob, ga, gb, x2, w_out_a, w_out_b, w_out)
    return out.reshape(batch, seq_len, d_model)
```

```python
import functools
import math

import jax
import jax.numpy as jnp
from jax import lax
from jax.experimental import pallas as pl
from jax.experimental.pallas import tpu as pltpu

F32 = jnp.float32
BF16 = jnp.bfloat16

D_MODEL = 1024
DN_HEADS = 4
DN_HEAD_DIM = 128
DN_WIDTH = DN_HEADS * DN_HEAD_DIM
CONV_WIDTH = 4
CHUNK = 64
DA_HEADS = 4
DA_QK_DIM = 64
DA_V_DIM = 2 * DA_QK_DIM
DA_WIDTH = DA_HEADS * DA_V_DIM
DA_QK_WIDTH = DA_HEADS * 2 * DA_QK_DIM
LAMBDA_INIT = 0.8 - 0.6 * math.exp(-0.3 * 0)
EPS = 1e-6
NEG_INF = -1e30
LOG2_E = math.log2(math.e)

LANES = 128
MXU_COLS = 256
SUBLANES = 8
BF16_ROWS = 16
AB_ROWS = BF16_ROWS
VMEM_LIMIT = 48 * 1024 * 1024

TM_IN = 256
TB_DN = 512
TQ_DA = 512
TK_DA = 512
TM_OUT = 512

NT_DIMS = (((1,), (1,)), ((), ()))
TN_DIMS = (((0,), (0,)), ((), ()))


def _silu(v):
    return v * jax.nn.sigmoid(v)


def _softplus(v):
    return jnp.maximum(v, 0.0) + jnp.log1p(jnp.exp(-jnp.abs(v)))


def _resident(shape):
    zeros = (0,) * len(shape)
    return pl.BlockSpec(shape, lambda *_: zeros, pipeline_mode=pl.Buffered(1))


def _inproj_kernel(x_ref, nw_ref, wqkv_ref, convw_ref, wza_ref, wab_ref, wabt_ref,
                   alog_l_ref, dtb_l_ref, alog_s_ref, dtb_s_ref, qnw_ref, knw_ref,
                   wqb_ref, wkb_ref, wvb_ref, wzb_ref, wga_ref, wgb_ref,
                   qa_ref, ka_ref, va_ref, za_ref, gcol_ref, grow_ref,
                   qb_ref, kb_ref, vbt_ref, zb_ref, ga_ref, gb_ref,
                   conv_buf, *, tiles_per_seq):
    tm = x_ref.shape[0]
    i = pl.program_id(0)

    x = x_ref[...]
    ms = jnp.mean(x * x, axis=-1, keepdims=True)
    h = (x * lax.rsqrt(ms + EPS) * nw_ref[...]).astype(BF16)

    @pl.when(i % tiles_per_seq == 0)
    def _():
        conv_buf[0:SUBLANES, :] = jnp.zeros((SUBLANES, 3 * DN_WIDTH), F32)

    head0 = SUBLANES - (CONV_WIDTH - 1)
    for s, out_ref in enumerate((qa_ref, ka_ref, va_ref)):
        c0 = s * DN_WIDTH
        pre = jnp.dot(h, wqkv_ref[:, c0:c0 + DN_WIDTH], preferred_element_type=F32)
        conv_buf[SUBLANES:SUBLANES + tm, c0:c0 + DN_WIDTH] = pre
        acc = convw_ref[CONV_WIDTH - 1:CONV_WIDTH, c0:c0 + DN_WIDTH] * pre
        for j in range(CONV_WIDTH - 1):
            acc = acc + (convw_ref[j:j + 1, c0:c0 + DN_WIDTH]
                         * conv_buf[head0 + j:head0 + j + tm, c0:c0 + DN_WIDTH])
        y = _silu(acc)
        if s == 2:
            out_ref[...] = y.astype(BF16)
        else:
            scale = DN_HEAD_DIM ** -0.5 if s == 0 else 1.0
            for hd in range(DN_HEADS):
                blk = y[:, hd * DN_HEAD_DIM:(hd + 1) * DN_HEAD_DIM]
                ss = jnp.sum(blk * blk, axis=-1, keepdims=True)
                out_ref[:, hd * DN_HEAD_DIM:(hd + 1) * DN_HEAD_DIM] = (
                    blk * (lax.rsqrt(ss + EPS) * scale)).astype(BF16)
    conv_buf[0:SUBLANES, :] = conv_buf[tm:tm + SUBLANES, :]

    za_ref[...] = _silu(jnp.dot(h, wza_ref[...], preferred_element_type=F32)).astype(BF16)

    ab = jnp.dot(h, wab_ref[...], preferred_element_type=F32)
    lane = lax.broadcasted_iota(jnp.int32, ab.shape, 1)
    g_l = -jnp.exp(alog_l_ref[...]) * _softplus(ab + dtb_l_ref[...])
    gcol_ref[...] = jnp.where(lane < DN_HEADS, g_l, jax.nn.sigmoid(ab))
    abt = lax.dot_general(wabt_ref[...], h, NT_DIMS, preferred_element_type=F32)
    row = lax.broadcasted_iota(jnp.int32, abt.shape, 0)
    g_s = -jnp.exp(alog_s_ref[...]) * _softplus(abt + dtb_s_ref[...])
    grow_ref[...] = jnp.where(row < DN_HEADS, g_s, jax.nn.sigmoid(abt))

    lane_b = lax.broadcasted_iota(jnp.int32, (tm, LANES), 1)
    lo = lane_b < DA_QK_DIM
    for w_ref, nrm_ref, out_ref, scale in ((wqb_ref, qnw_ref, qb_ref, DA_QK_DIM ** -0.5 * LOG2_E),
                                           (wkb_ref, knw_ref, kb_ref, 1.0)):
        p = jnp.dot(h, w_ref[...], preferred_element_type=F32)
        for hd in range(DA_HEADS):
            blk = p[:, hd * LANES:(hd + 1) * LANES]
            sq = blk * blk
            s_lo = jnp.sum(jnp.where(lo, sq, 0.0), axis=-1, keepdims=True)
            s_hi = jnp.sum(jnp.where(lo, 0.0, sq), axis=-1, keepdims=True)
            msq = jnp.where(lo, s_lo, s_hi) * (1.0 / DA_QK_DIM)
            out_ref[:, hd * LANES:(hd + 1) * LANES] = (
                blk * lax.rsqrt(msq + EPS) * nrm_ref[...] * scale).astype(BF16)

    vbt_ref[...] = lax.dot_general(wvb_ref[...], h, NT_DIMS,
                                   preferred_element_type=F32).astype(BF16)
    zb_ref[...] = _silu(jnp.dot(h, wzb_ref[...], preferred_element_type=F32)).astype(BF16)

    for w_ref, out_ref in ((wga_ref, ga_ref), (wgb_ref, gb_ref)):
        for c0 in range(0, D_MODEL, 512):
            gate = jnp.dot(h, w_ref[:, c0:c0 + 512], preferred_element_type=F32)
            out_ref[:, c0:c0 + 512] = jax.nn.sigmoid(gate).astype(BF16)


def _in_projection(x2, norm_w, w_in, conv_w, a_log, dt_bias, q_norm_w, k_norm_w, seq_len):
    m = x2.shape[0]
    tm = TM_IN
    assert m % tm == 0 and seq_len % tm == 0
    o = 0
    cols = {}
    for name, width in (("qkv", 3 * DN_WIDTH), ("za", DN_WIDTH), ("a", DN_HEADS), ("b", DN_HEADS),
                        ("qb", DA_QK_WIDTH), ("kb", DA_QK_WIDTH), ("vb", DA_WIDTH), ("zb", DA_WIDTH),
                        ("ga", D_MODEL), ("gb", D_MODEL)):
        cols[name] = w_in[:, o:o + width]
        o += width
    assert o == w_in.shape[1]
    wb = {k: v.astype(BF16) for k, v in cols.items()}
    w_ab = jnp.concatenate([cols["a"], cols["b"]], axis=1)
    w_ab_l = jnp.pad(w_ab, ((0, 0), (0, LANES - 2 * DN_HEADS))).astype(BF16)
    w_ab_s = jnp.pad(w_ab.T, ((0, AB_ROWS - 2 * DN_HEADS), (0, 0))).astype(BF16)
    alog_l = jnp.pad(a_log.astype(F32), (0, LANES - DN_HEADS)).reshape(1, LANES)
    dtb_l = jnp.pad(dt_bias.astype(F32), (0, LANES - DN_HEADS)).reshape(1, LANES)
    alog_s = jnp.pad(a_log.astype(F32), (0, AB_ROWS - DN_HEADS)).reshape(AB_ROWS, 1)
    dtb_s = jnp.pad(dt_bias.astype(F32), (0, AB_ROWS - DN_HEADS)).reshape(AB_ROWS, 1)
    qnw = jnp.tile(q_norm_w.astype(F32), 2).reshape(1, LANES)
    knw = jnp.tile(k_norm_w.astype(F32), 2).reshape(1, LANES)

    row_blk = lambda w: pl.BlockSpec((tm, w), lambda i: (i, 0))
    in_specs = [
        row_blk(D_MODEL), _resident((1, D_MODEL)),
        _resident((D_MODEL, 3 * DN_WIDTH)), _resident((CONV_WIDTH, 3 * DN_WIDTH)),
        _resident((D_MODEL, DN_WIDTH)), _resident((D_MODEL, LANES)), _resident((AB_ROWS, D_MODEL)),
        _resident((1, LANES)), _resident((1, LANES)), _resident((AB_ROWS, 1)), _resident((AB_ROWS, 1)),
        _resident((1, LANES)), _resident((1, LANES)),
        _resident((D_MODEL, DA_QK_WIDTH)), _resident((D_MODEL, DA_QK_WIDTH)),
        _resident((DA_WIDTH, D_MODEL)), _resident((D_MODEL, DA_WIDTH)),
        _resident((D_MODEL, D_MODEL)), _resident((D_MODEL, D_MODEL)),
    ]
    out_shapes = [jax.ShapeDtypeStruct((m, DN_WIDTH), BF16)] * 4 + [
        jax.ShapeDtypeStruct((m, LANES), F32), jax.ShapeDtypeStruct((AB_ROWS, m), F32)] + [
        jax.ShapeDtypeStruct((m, DA_WIDTH), BF16)] * 2 + [
        jax.ShapeDtypeStruct((DA_WIDTH, m), BF16), jax.ShapeDtypeStruct((m, DA_WIDTH), BF16)] + [
        jax.ShapeDtypeStruct((m, D_MODEL), BF16)] * 2
    out_specs = [row_blk(DN_WIDTH)] * 4 + [
        row_blk(LANES), pl.BlockSpec((AB_ROWS, tm), lambda i: (0, i))] + [
        row_blk(DA_WIDTH)] * 2 + [
        pl.BlockSpec((DA_WIDTH, tm), lambda i: (0, i)), row_blk(DA_WIDTH)] + [
        row_blk(D_MODEL)] * 2
    return pl.pallas_call(
        functools.partial(_inproj_kernel, tiles_per_seq=seq_len // tm),
        grid=(m // tm,),
        in_specs=in_specs,
        out_specs=out_specs,
        out_shape=out_shapes,
        scratch_shapes=[pltpu.VMEM((tm + SUBLANES, 3 * DN_WIDTH), F32)],
        compiler_params=pltpu.CompilerParams(
            dimension_semantics=("arbitrary",), vmem_limit_bytes=VMEM_LIMIT),
        name="in_projection",
    )(x2, norm_w.astype(F32).reshape(1, D_MODEL), wb["qkv"], conv_w.astype(F32), wb["za"],
      w_ab_l, w_ab_s, alog_l, dtb_l, alog_s, dtb_s, qnw, knw,
      wb["qb"], wb["kb"], wb["vb"].T, wb["zb"], wb["ga"], wb["gb"])


def _bmm(a, b):
    return jnp.einsum('nij,njk->nik', a, b, preferred_element_type=F32)


def _bmm_nt(a, b):
    return jnp.einsum('nid,njd->nij', a, b, preferred_element_type=F32)


def _split_bf16(x, parts):
    out = []
    for _ in range(parts - 1):
        hi = x.astype(BF16)
        out.append(hi)
        x = x - hi.astype(F32)
    out.append(x.astype(BF16))
    return out


def _unit_lower_inverse(l_strict, eye):
    n = l_strict.shape[-1]
    inv = eye - l_strict
    power_b = l_strict.astype(BF16)
    k = 2
    while k < n:
        power_b = _bmm(power_b, power_b).astype(BF16)
        inv = inv + _bmm(inv.astype(BF16), power_b)
        k *= 2
    return inv


def _deltanet_kernel(q_ref, k_ref, v_ref, z_ref, gcol_ref, grow_ref, nw_ref, o_ref,
                     state_ref, oraw_ref):
    tb = q_ref.shape[0]
    c = CHUNK
    nc = tb // c
    nh = DN_HEADS

    @pl.when(pl.program_id(1) == 0)
    def _():
        state_ref[...] = jnp.zeros(state_ref.shape, F32)

    ri = lax.broadcasted_iota(jnp.int32, (c, c), 0)
    ci = lax.broadcasted_iota(jnp.int32, (c, c), 1)
    lower = ri >= ci
    strict = ri > ci
    eye = (ri == ci).astype(F32)
    tri = lower.astype(BF16)
    tri_t = (ri <= ci).astype(BF16)

    def heads_major(ref):
        return jnp.concatenate(
            [ref[:, hd * DN_HEAD_DIM:(hd + 1) * DN_HEAD_DIM].reshape(nc, c, DN_HEAD_DIM)
             for hd in range(nh)], axis=0)

    gb_col = gcol_ref[...].reshape(nc, c, LANES)
    tri_b = jnp.broadcast_to(tri, (nc, c, c))
    gc_col = sum(_bmm(tri_b, part) for part in _split_bf16(gb_col, 3))
    gb_row = grow_ref[...]
    gc_row = sum(jnp.dot(part, tri_t, preferred_element_type=F32)
                 for part in _split_bf16(gb_row.reshape(nc * AB_ROWS, c), 3)
                 ).reshape(nc, AB_ROWS, c)

    def col_form(src, lane):
        return jnp.concatenate(
            [jnp.broadcast_to(src[:, :, lane + hd:lane + hd + 1], (nc, c, LANES))
             for hd in range(nh)], axis=0)

    def row_form(src, row):
        return jnp.concatenate([src[:, row + hd:row + hd + 1, :] for hd in range(nh)], axis=0)

    g_c = col_form(gc_col, 0)
    beta_c = col_form(gb_col, nh)
    g_r = row_form(gc_row, 0)
    beta_r = row_form(gb_row, nh)
    g_last = g_c[:, c - 1:c, :]

    q = heads_major(q_ref)
    k = heads_major(k_ref)
    v = heads_major(v_ref)

    diff = g_c[:, :, :c] - g_r
    decay = jnp.where(lower, jnp.exp(jnp.where(lower, diff, 0.0)), 0.0)
    l_strict = jnp.where(strict, _bmm_nt(k, k) * decay * beta_c[:, :, :c], 0.0)
    attn = jnp.where(lower, _bmm_nt(q, k) * decay, 0.0).astype(BF16)
    t_beta = _unit_lower_inverse(l_strict, eye) * beta_r
    u = _bmm(t_beta.astype(BF16), v)
    w = _bmm((t_beta * jnp.exp(g_r)).astype(BF16), k).astype(BF16)
    q_decay = (q.astype(F32) * jnp.exp(g_c)).astype(BF16)
    k_decay = (k.astype(F32) * jnp.exp(g_last - g_c)).astype(BF16)
    s_decay = jnp.exp(g_last)

    for n in range(nc):
        idx = [hd * nc + n for hd in range(nh)]
        s_old = [state_ref[hd] for hd in range(nh)]
        ws_qs = [jnp.dot(jnp.concatenate([w[b], q_decay[b]], axis=0), s_old[hd].astype(BF16),
                         preferred_element_type=F32) for hd, b in enumerate(idx)]
        v_new = [(u[b] - ws_qs[hd][:c]).astype(BF16) for hd, b in enumerate(idx)]
        for hd, b in enumerate(idx):
            oraw_ref[n * c:(n + 1) * c, hd * DN_HEAD_DIM:(hd + 1) * DN_HEAD_DIM] = (
                ws_qs[hd][c:] + jnp.dot(attn[b], v_new[hd], preferred_element_type=F32))
        for hd, b in enumerate(idx):
            state_ref[hd] = s_old[hd] * s_decay[b] + lax.dot_general(
                k_decay[b], v_new[hd], TN_DIMS, preferred_element_type=F32)

    for hd in range(nh):
        hs = slice(hd * DN_HEAD_DIM, (hd + 1) * DN_HEAD_DIM)
        o = oraw_ref[:, hs]
        ms = jnp.mean(o * o, axis=-1, keepdims=True)
        o = o * lax.rsqrt(ms + EPS) * nw_ref[...]
        o_ref[:, hs] = (o * z_ref[:, hs].astype(F32)).astype(BF16)


def _deltanet(qa, ka, va, za, gcol, grow, dn_norm_w, batch, seq_len):
    tb = TB_DN
    assert seq_len % tb == 0 and tb % CHUNK == 0
    nt = seq_len // tb
    nc = tb // CHUNK
    grow3 = grow.reshape(AB_ROWS, -1, CHUNK).transpose(1, 0, 2)
    row_blk = lambda w: pl.BlockSpec((tb, w), lambda b, t: (b * nt + t, 0))
    return pl.pallas_call(
        _deltanet_kernel,
        grid=(batch, nt),
        in_specs=[row_blk(DN_WIDTH)] * 4 + [
            row_blk(LANES),
            pl.BlockSpec((nc, AB_ROWS, CHUNK), lambda b, t: (b * nt + t, 0, 0)),
            _resident((1, DN_HEAD_DIM))],
        out_specs=row_blk(DN_WIDTH),
        out_shape=jax.ShapeDtypeStruct((batch * seq_len, DN_WIDTH), BF16),
        scratch_shapes=[pltpu.VMEM((DN_HEADS, DN_HEAD_DIM, DN_HEAD_DIM), F32),
                        pltpu.VMEM((tb, DN_WIDTH), F32)],
        compiler_params=pltpu.CompilerParams(
            dimension_semantics=("arbitrary", "arbitrary"), vmem_limit_bytes=VMEM_LIMIT),
        name="gated_deltanet",
    )(qa, ka, va, za, gcol, grow3, dn_norm_w.astype(F32).reshape(1, DN_HEAD_DIM))


def _diffattn_kernel(lq1_ref, lk1_ref, lq2_ref, lk2_ref, q_ref, k_ref, vt_ref, z_ref, nw_ref,
                     o_ref, m_ref, acc_ref, sa_ref, sb_ref, pa_ref, pb_ref, aa_ref, ab_ref):
    tq = q_ref.shape[0]
    tk = TK_DA
    qi = pl.program_id(2)

    lane = lax.broadcasted_iota(jnp.int32, (1, LANES), 1)
    q = q_ref[...]
    zero = jnp.zeros_like(q)
    q_comp = (jnp.where(lane < DA_QK_DIM, q, zero), jnp.where(lane < DA_QK_DIM, zero, q))

    m_ref[...] = jnp.full(m_ref.shape, NEG_INF, F32)
    acc_ref[...] = jnp.zeros(acc_ref.shape, F32)

    ones_rows = jnp.ones((BF16_ROWS, tk), BF16)

    def softmax_piece(s_ref, p_ref, alpha_ref, comp, g, masked):
        s = s_ref[comp, g]
        if masked:
            rk = lax.broadcasted_iota(jnp.int32, (tk, LANES), 0)
            cq = lax.broadcasted_iota(jnp.int32, (tk, LANES), 1) + g * LANES
            s = jnp.where(rk <= cq, s, NEG_INF)
        m_prev = m_ref[comp, g]
        m_cur = jnp.max(jnp.max(s.reshape(SUBLANES, tk // SUBLANES, LANES), axis=0),
                        axis=0, keepdims=True)
        m_new = jnp.maximum(m_prev, m_cur)
        alpha_ref[comp, g] = jnp.exp2(m_prev - m_new)
        p_ref[comp, g] = jnp.exp2(s - m_new).astype(BF16)
        m_ref[comp, g] = m_new

    groups_per_tile = MXU_COLS // LANES

    def step(prev=None, nxt=None, cur=None, masked=False):
        if prev is not None:
            p_prev, alpha_prev, j = prev
            start = pl.multiple_of(j * tk, tk)
            vtblk = jnp.concatenate([vt_ref[:, pl.ds(start, tk)], ones_rows], axis=0)
        if nxt is not None:
            s_next, j = nxt
            kblk = k_ref[pl.ds(pl.multiple_of(j * tk, tk), tk), :]
        for comp in range(2):
            for t in range(tq // MXU_COLS):
                groups = range(t * groups_per_tile, (t + 1) * groups_per_tile)
                if prev is not None:
                    pv = jnp.dot(vtblk, jnp.concatenate([p_prev[comp, g] for g in groups], axis=1),
                                 preferred_element_type=F32)
                    for n, g in enumerate(groups):
                        acc_ref[comp, g] = (alpha_prev[comp, g] * acc_ref[comp, g]
                                            + pv[:, n * LANES:(n + 1) * LANES])
                if nxt is not None:
                    s_t = lax.dot_general(
                        kblk, q_comp[comp][t * MXU_COLS:(t + 1) * MXU_COLS, :], NT_DIMS,
                        preferred_element_type=F32)
                    for n, g in enumerate(groups):
                        s_next[comp, g] = s_t[:, n * LANES:(n + 1) * LANES]
                if cur is not None:
                    for g in groups:
                        softmax_piece(*cur, comp, g, masked)

    buf_a = (sa_ref, pa_ref, aa_ref)
    buf_b = (sb_ref, pb_ref, ab_ref)
    step(nxt=(sb_ref, qi))
    step(nxt=(sa_ref, 0), cur=buf_b, masked=True)

    def pair(i, carry):
        j0 = 2 * i
        step(prev=(pb_ref, ab_ref, jnp.where(i == 0, qi, j0 - 1)), nxt=(sb_ref, j0 + 1), cur=buf_a)
        step(prev=(pa_ref, aa_ref, j0), nxt=(sa_ref, jnp.minimum(j0 + 2, qi - 1)), cur=buf_b)
        return carry

    lax.fori_loop(0, qi // 2, pair, 0)

    @pl.when(qi % 2 == 1)
    def _():
        step(prev=(pb_ref, ab_ref, jnp.where(qi == 1, qi, qi - 2)), cur=buf_a)
        step(prev=(pa_ref, aa_ref, qi - 1))

    @pl.when(qi % 2 == 0)
    def _():
        step(prev=(pb_ref, ab_ref, jnp.where(qi == 0, qi, qi - 1)))

    lam = (jnp.exp(jnp.sum(lq1_ref[...] * lk1_ref[...], axis=-1, keepdims=True))
           - jnp.exp(jnp.sum(lq2_ref[...] * lk2_ref[...], axis=-1, keepdims=True))
           + LAMBDA_INIT)
    dv = DA_V_DIM
    for g in range(tq // LANES):
        rows = slice(g * LANES, (g + 1) * LANES)
        o = (acc_ref[0, g, 0:dv, :] / acc_ref[0, g, dv:dv + 1, :]
             - lam * (acc_ref[1, g, 0:dv, :] / acc_ref[1, g, dv:dv + 1, :])).T
        ms = jnp.mean(o * o, axis=-1, keepdims=True)
        o = o * lax.rsqrt(ms + EPS) * nw_ref[...] * (1.0 - LAMBDA_INIT)
        o_ref[rows, :] = (o * z_ref[rows, :].astype(F32)).astype(BF16)


def _diff_attention(qb, kb, vbt, zb, lambda_q1, lambda_k1, lambda_q2, lambda_k2, da_norm_w,
                    batch, seq_len):
    tq = TQ_DA
    assert tq == TK_DA and seq_len % tq == 0
    nq = seq_len // tq
    ng = tq // LANES
    lam_vec = lambda v: v.astype(F32).reshape(1, DA_QK_DIM)
    q_blk = pl.BlockSpec((tq, LANES), lambda b, h, i: (b * nq + i, h))
    k_blk = pl.BlockSpec((seq_len, LANES), lambda b, h, i: (b, h))
    vt_blk = pl.BlockSpec((DA_V_DIM, seq_len), lambda b, h, i: (h, b))
    return pl.pallas_call(
        _diffattn_kernel,
        grid=(batch, DA_HEADS, nq),
        in_specs=[_resident((1, DA_QK_DIM))] * 4 + [q_blk, k_blk, vt_blk, q_blk,
                                                    _resident((1, DA_V_DIM))],
        out_specs=q_blk,
        out_shape=jax.ShapeDtypeStruct((batch * seq_len, DA_WIDTH), BF16),
        scratch_shapes=[pltpu.VMEM((2, ng, 1, LANES), F32),
                        pltpu.VMEM((2, ng, DA_V_DIM + BF16_ROWS, LANES), F32),
                        pltpu.VMEM((2, ng, TK_DA, LANES), F32), pltpu.VMEM((2, ng, TK_DA, LANES), F32),
                        pltpu.VMEM((2, ng, TK_DA, LANES), BF16), pltpu.VMEM((2, ng, TK_DA, LANES), BF16),
                        pltpu.VMEM((2, ng, 1, LANES), F32), pltpu.VMEM((2, ng, 1, LANES), F32)],
        compiler_params=pltpu.CompilerParams(
            dimension_semantics=("arbitrary", "arbitrary", "arbitrary"),
            vmem_limit_bytes=VMEM_LIMIT),
        name="diff_attention",
    )(lam_vec(lambda_q1), lam_vec(lambda_k1), lam_vec(lambda_q2), lam_vec(lambda_k2),
      qb, kb, vbt, zb, da_norm_w.astype(F32).reshape(1, DA_V_DIM))


def _output_kernel(oa_ref, ob_ref, ga_ref, gb_ref, x_ref, woa_ref, wob_ref, wo_ref, out_ref):
    y_a = jnp.dot(oa_ref[...], woa_ref[...], preferred_element_type=F32)
    y_b = jnp.dot(ob_ref[...], wob_ref[...], preferred_element_type=F32)
    y = ga_ref[...].astype(F32) * y_a + gb_ref[...].astype(F32) * y_b
    out_ref[...] = x_ref[...] + jnp.dot(y.astype(BF16), wo_ref[...], preferred_element_type=F32)


def _output_projection(oa, ob, ga, gb, x2, w_out_a, w_out_b, w_out):
    m = x2.shape[0]
    tm = TM_OUT
    assert m % tm == 0
    row_blk = lambda w: pl.BlockSpec((tm, w), lambda i: (i, 0))
    return pl.pallas_call(
        _output_kernel,
        grid=(m // tm,),
        in_specs=[row_blk(DN_WIDTH), row_blk(DA_WIDTH), row_blk(D_MODEL), row_blk(D_MODEL),
                  row_blk(D_MODEL), _resident((DN_WIDTH, D_MODEL)), _resident((DA_WIDTH, D_MODEL)),
                  _resident((D_MODEL, D_MODEL))],
        out_specs=row_blk(D_MODEL),
        out_shape=jax.ShapeDtypeStruct((m, D_MODEL), F32),
        compiler_params=pltpu.CompilerParams(
            dimension_semantics=("arbitrary",), vmem_limit_bytes=VMEM_LIMIT),
        name="output_projection",
    )(oa, ob, ga, gb, x2, w_out_a.astype(BF16), w_out_b.astype(BF16), w_out.astype(BF16))


def kernel(x, norm_w, w_in, conv_w, a_log, dt_bias, dn_norm_w, q_norm_w, k_norm_w,
           lambda_q1, lambda_k1, lambda_q2, lambda_k2, da_norm_w, w_out_a, w_out_b, w_out):
    batch, seq_len, d_model = x.shape
    assert d_model == D_MODEL
    x2 = x.reshape(batch * seq_len, d_model)
    (qa, ka, va, za, gcol, grow, qb, kb, vbt, zb, ga, gb) = _in_projection(
        x2, norm_w, w_in, conv_w, a_log, dt_bias, q_norm_w, k_norm_w, seq_len)
    oa = _deltanet(qa, ka, va, za, gcol, grow, dn_norm_w, batch, seq_len)
    ob = _diff_attention(qb, kb, vbt, zb, lambda_q1, lambda_k1, lambda_q2, lambda_k2, da_norm_w,
                         batch, seq_len)
    out = _output_projection(oa, ob, ga, gb, x2, w_out_a, w_out_b, w_out)
    return out.reshape(batch, seq_len, d_model)
```

```python
import functools
import math

import jax
import jax.numpy as jnp
from jax import lax
from jax.experimental import pallas as pl
from jax.experimental.pallas import tpu as pltpu

F32 = jnp.float32
BF16 = jnp.bfloat16

D_MODEL = 1024
DN_HEADS = 4
DN_HEAD_DIM = 128
DN_WIDTH = DN_HEADS * DN_HEAD_DIM
CONV_WIDTH = 4
CHUNK = 64
DA_HEADS = 4
DA_QK_DIM = 64
DA_V_DIM = 2 * DA_QK_DIM
DA_WIDTH = DA_HEADS * DA_V_DIM
DA_QK_WIDTH = DA_HEADS * 2 * DA_QK_DIM
LAMBDA_INIT = 0.8 - 0.6 * math.exp(-0.3 * 0)
EPS = 1e-6
NEG_INF = -1e30
LOG2_E = math.log2(math.e)

LANES = 128
MXU_COLS = 256
SUBLANES = 8
BF16_ROWS = 16
AB_ROWS = BF16_ROWS
VMEM_LIMIT = 48 * 1024 * 1024

TM_IN = 512
TB_DN = 512
TQ_DA = 512
TK_DA = 512
TM_OUT = 512

NT_DIMS = (((1,), (1,)), ((), ()))
TN_DIMS = (((0,), (0,)), ((), ()))


def _sigmoid(v):
    return 0.5 + 0.5 * jnp.tanh(0.5 * v)


def _silu(v):
    h = 0.5 * v
    return h + h * jnp.tanh(h)


def _softplus(v):
    return jnp.maximum(v, 0.0) + jnp.log1p(jnp.exp(-jnp.abs(v)))


def _resident(shape):
    zeros = (0,) * len(shape)
    return pl.BlockSpec(shape, lambda *_: zeros, pipeline_mode=pl.Buffered(1))


def _inproj_kernel(x_ref, nw_ref, wqkv_ref, convw_ref, wza_ref, wab_ref, wabt_ref,
                   alog_l_ref, dtb_l_ref, alog_s_ref, dtb_s_ref, qnw_ref, knw_ref,
                   wqb_ref, wkb_ref, wvb_ref, wzb_ref, wga_ref, wgb_ref,
                   qa_ref, ka_ref, va_ref, za_ref, gcol_ref, grow_ref,
                   qb_ref, kb_ref, vb_ref, zb_ref, ga_ref, gb_ref,
                   conv_buf, *, tiles_per_seq):
    tm = x_ref.shape[0]
    half = tm // 2
    i = pl.program_id(0)
    head0 = SUBLANES - (CONV_WIDTH - 1)
    slab = 512

    @pl.when(i % tiles_per_seq == 0)
    def _():
        conv_buf[0:SUBLANES, :] = jnp.zeros((SUBLANES, 3 * DN_WIDTH), F32)

    def normed(r0):
        x = x_ref[r0:r0 + half, :]
        ms = jnp.mean(x * x, axis=-1, keepdims=True)
        return (x * lax.rsqrt(ms + EPS) * nw_ref[...]).astype(BF16)

    def tasks(h, r0):
        rows = slice(r0, r0 + half)

        def conv_qkv(s, out_ref):
            cs = slice(s * DN_WIDTH, (s + 1) * DN_WIDTH)

            res = []

            def matmul():
                res.append(jnp.dot(h, wqkv_ref[:, cs], preferred_element_type=F32))
                conv_buf[SUBLANES + r0:SUBLANES + r0 + half, cs] = res[0]

            def epilogue():
                acc = convw_ref[CONV_WIDTH - 1:CONV_WIDTH, cs] * res[0]
                for j in range(CONV_WIDTH - 1):
                    acc = acc + (convw_ref[j:j + 1, cs]
                                 * conv_buf[head0 + j + r0:head0 + j + r0 + half, cs])
                y = _silu(acc)
                if s == 2:
                    out_ref[rows, :] = y.astype(BF16)
                    return
                scale = DN_HEAD_DIM ** -0.5 if s == 0 else 1.0
                for hd in range(DN_HEADS):
                    hs = slice(hd * DN_HEAD_DIM, (hd + 1) * DN_HEAD_DIM)
                    blk = y[:, hs]
                    ss = jnp.sum(blk * blk, axis=-1, keepdims=True)
                    out_ref[rows, hs] = (blk * (lax.rsqrt(ss + EPS) * scale)).astype(BF16)
            return matmul, [epilogue]

        def plain(w_ref, c0, out_ref, act):
            res = []

            def matmul():
                res.append(jnp.dot(h, w_ref[:, c0:c0 + slab], preferred_element_type=F32))

            def chunk(n):
                def run():
                    cs = slice(n * LANES, (n + 1) * LANES)
                    out_ref[rows, c0 + n * LANES:c0 + (n + 1) * LANES] = (
                        act(res[0][:, cs]).astype(BF16))
                return run
            return matmul, [chunk(n) for n in range(slab // LANES)]

        def qk_norm(w_ref, nrm_ref, out_ref, scale):
            res = []

            def matmul():
                res.append(jnp.dot(h, w_ref[...], preferred_element_type=F32))

            def chunk(hd):
                def run():
                    lo = lax.broadcasted_iota(jnp.int32, (half, LANES), 1) < DA_QK_DIM
                    blk = res[0][:, hd * LANES:(hd + 1) * LANES]
                    sq = blk * blk
                    s_lo = jnp.sum(jnp.where(lo, sq, 0.0), axis=-1, keepdims=True)
                    s_hi = jnp.sum(jnp.where(lo, 0.0, sq), axis=-1, keepdims=True)
                    msq = jnp.where(lo, s_lo, s_hi) * (1.0 / DA_QK_DIM)
                    out_ref[rows, hd * LANES:(hd + 1) * LANES] = (
                        blk * lax.rsqrt(msq + EPS) * nrm_ref[...] * scale).astype(BF16)
                return run
            return matmul, [chunk(hd) for hd in range(DA_HEADS)]

        def decay_beta():
            def matmul():
                ab = jnp.dot(h, wab_ref[...], preferred_element_type=F32)
                lane = lax.broadcasted_iota(jnp.int32, ab.shape, 1)
                g_l = -jnp.exp(alog_l_ref[...]) * _softplus(ab + dtb_l_ref[...])
                gcol_ref[rows, :] = jnp.where(lane < DN_HEADS, g_l, _sigmoid(ab))
                abt = lax.dot_general(wabt_ref[...], h, NT_DIMS,
                                      preferred_element_type=F32)
                row = lax.broadcasted_iota(jnp.int32, abt.shape, 0)
                g_s = -jnp.exp(alog_s_ref[...]) * _softplus(abt + dtb_s_ref[...])
                grow_ref[:, rows] = jnp.where(row < DN_HEADS, g_s, _sigmoid(abt))
            return matmul, []

        ident = lambda v: v
        return [conv_qkv(0, qa_ref), plain(wga_ref, 0, ga_ref, _sigmoid),
                conv_qkv(1, ka_ref), plain(wga_ref, slab, ga_ref, _sigmoid),
                conv_qkv(2, va_ref), plain(wgb_ref, 0, gb_ref, _sigmoid),
                qk_norm(wqb_ref, qnw_ref, qb_ref, DA_QK_DIM ** -0.5 * LOG2_E),
                plain(wgb_ref, slab, gb_ref, _sigmoid),
                qk_norm(wkb_ref, knw_ref, kb_ref, 1.0), plain(wvb_ref, 0, vb_ref, ident),
                plain(wza_ref, 0, za_ref, _silu), plain(wzb_ref, 0, zb_ref, _silu),
                decay_beta()]

    def run_tasks(task_list, extra=None):
        for n, (matmul, chunks) in enumerate(task_list):
            matmul()
            for fn in chunks:
                fn()
            if extra is not None and n == 8:
                extra()

    h_second = []
    run_tasks(tasks(normed(0), 0), extra=lambda: h_second.append(normed(half)))
    run_tasks(tasks(h_second[0], half))
    conv_buf[0:SUBLANES, :] = conv_buf[tm:tm + SUBLANES, :]


def _in_projection(x2, norm_w, w_in, conv_w, a_log, dt_bias, q_norm_w, k_norm_w, seq_len):
    m = x2.shape[0]
    tm = TM_IN
    assert m % tm == 0 and seq_len % tm == 0
    o = 0
    cols = {}
    for name, width in (("qkv", 3 * DN_WIDTH), ("za", DN_WIDTH), ("a", DN_HEADS), ("b", DN_HEADS),
                        ("qb", DA_QK_WIDTH), ("kb", DA_QK_WIDTH), ("vb", DA_WIDTH), ("zb", DA_WIDTH),
                        ("ga", D_MODEL), ("gb", D_MODEL)):
        cols[name] = w_in[:, o:o + width]
        o += width
    assert o == w_in.shape[1]
    wb = {k: v.astype(BF16) for k, v in cols.items()}
    w_ab = jnp.concatenate([cols["a"], cols["b"]], axis=1)
    w_ab_l = jnp.pad(w_ab, ((0, 0), (0, LANES - 2 * DN_HEADS))).astype(BF16)
    w_ab_s = jnp.pad(w_ab.T, ((0, AB_ROWS - 2 * DN_HEADS), (0, 0))).astype(BF16)
    alog_l = jnp.pad(a_log.astype(F32), (0, LANES - DN_HEADS)).reshape(1, LANES)
    dtb_l = jnp.pad(dt_bias.astype(F32), (0, LANES - DN_HEADS)).reshape(1, LANES)
    alog_s = jnp.pad(a_log.astype(F32), (0, AB_ROWS - DN_HEADS)).reshape(AB_ROWS, 1)
    dtb_s = jnp.pad(dt_bias.astype(F32), (0, AB_ROWS - DN_HEADS)).reshape(AB_ROWS, 1)
    qnw = jnp.tile(q_norm_w.astype(F32), 2).reshape(1, LANES)
    knw = jnp.tile(k_norm_w.astype(F32), 2).reshape(1, LANES)

    row_blk = lambda w: pl.BlockSpec((tm, w), lambda i: (i, 0))
    in_specs = [
        row_blk(D_MODEL), _resident((1, D_MODEL)),
        _resident((D_MODEL, 3 * DN_WIDTH)), _resident((CONV_WIDTH, 3 * DN_WIDTH)),
        _resident((D_MODEL, DN_WIDTH)), _resident((D_MODEL, LANES)), _resident((AB_ROWS, D_MODEL)),
        _resident((1, LANES)), _resident((1, LANES)), _resident((AB_ROWS, 1)), _resident((AB_ROWS, 1)),
        _resident((1, LANES)), _resident((1, LANES)),
        _resident((D_MODEL, DA_QK_WIDTH)), _resident((D_MODEL, DA_QK_WIDTH)),
        _resident((D_MODEL, DA_WIDTH)), _resident((D_MODEL, DA_WIDTH)),
        _resident((D_MODEL, D_MODEL)), _resident((D_MODEL, D_MODEL)),
    ]
    out_shapes = [jax.ShapeDtypeStruct((m, DN_WIDTH), BF16)] * 4 + [
        jax.ShapeDtypeStruct((m, LANES), F32), jax.ShapeDtypeStruct((AB_ROWS, m), F32)] + [
        jax.ShapeDtypeStruct((m, DA_WIDTH), BF16)] * 4 + [
        jax.ShapeDtypeStruct((m, D_MODEL), BF16)] * 2
    out_specs = [row_blk(DN_WIDTH)] * 4 + [
        row_blk(LANES), pl.BlockSpec((AB_ROWS, tm), lambda i: (0, i))] + [
        row_blk(DA_WIDTH)] * 4 + [row_blk(D_MODEL)] * 2
    return pl.pallas_call(
        functools.partial(_inproj_kernel, tiles_per_seq=seq_len // tm),
        grid=(m // tm,),
        in_specs=in_specs,
        out_specs=out_specs,
        out_shape=out_shapes,
        scratch_shapes=[pltpu.VMEM((tm + SUBLANES, 3 * DN_WIDTH), F32)],
        compiler_params=pltpu.CompilerParams(
            dimension_semantics=("arbitrary",), vmem_limit_bytes=VMEM_LIMIT),
        name="in_projection",
    )(x2, norm_w.astype(F32).reshape(1, D_MODEL), wb["qkv"], conv_w.astype(F32), wb["za"],
      w_ab_l, w_ab_s, alog_l, dtb_l, alog_s, dtb_s, qnw, knw,
      wb["qb"], wb["kb"], wb["vb"], wb["zb"], wb["ga"], wb["gb"])


def _bmm(a, b):
    return jnp.einsum('nij,njk->nik', a, b, preferred_element_type=F32)


def _bmm_nt(a, b):
    return jnp.einsum('nid,njd->nij', a, b, preferred_element_type=F32)


def _split_bf16(x, parts):
    out = []
    for _ in range(parts - 1):
        hi = x.astype(BF16)
        out.append(hi)
        x = x - hi.astype(F32)
    out.append(x.astype(BF16))
    return out


def _unit_lower_inverse(l_strict, eye):
    n = l_strict.shape[-1]
    inv = eye - l_strict
    power_b = l_strict.astype(BF16)
    k = 2
    while k < n:
        power_b = _bmm(power_b, power_b).astype(BF16)
        inv = inv + _bmm(inv.astype(BF16), power_b)
        k *= 2
    return inv


def _deltanet_kernel(q_ref, k_ref, v_ref, z_ref, gcol_ref, grow_ref, nw_ref, o_ref,
                     state_ref, oraw_ref):
    tb = q_ref.shape[0]
    c = CHUNK
    nc = tb // c
    nh = DN_HEADS

    @pl.when(pl.program_id(1) == 0)
    def _():
        state_ref[...] = jnp.zeros(state_ref.shape, F32)

    ri = lax.broadcasted_iota(jnp.int32, (c, c), 0)
    ci = lax.broadcasted_iota(jnp.int32, (c, c), 1)
    lower = ri >= ci
    strict = ri > ci
    eye = (ri == ci).astype(F32)
    tri = lower.astype(BF16)
    tri_t = (ri <= ci).astype(BF16)

    def heads_major(ref):
        return jnp.concatenate(
            [ref[:, hd * DN_HEAD_DIM:(hd + 1) * DN_HEAD_DIM].reshape(nc, c, DN_HEAD_DIM)
             for hd in range(nh)], axis=0)

    gb_col = gcol_ref[...].reshape(nc, c, LANES)
    tri_b = jnp.broadcast_to(tri, (nc, c, c))
    gc_col = sum(_bmm(tri_b, part) for part in _split_bf16(gb_col, 3))
    gb_row = grow_ref[...]
    gc_row = sum(jnp.dot(part, tri_t, preferred_element_type=F32)
                 for part in _split_bf16(gb_row.reshape(nc * AB_ROWS, c), 3)
                 ).reshape(nc, AB_ROWS, c)

    def col_form(src, lane):
        return jnp.concatenate(
            [jnp.broadcast_to(src[:, :, lane + hd:lane + hd + 1], (nc, c, LANES))
             for hd in range(nh)], axis=0)

    def row_form(src, row):
        return jnp.concatenate([src[:, row + hd:row + hd + 1, :] for hd in range(nh)], axis=0)

    g_c = col_form(gc_col, 0)
    beta_c = col_form(gb_col, nh)
    g_r = row_form(gc_row, 0)
    beta_r = row_form(gb_row, nh)
    g_last = g_c[:, c - 1:c, :]

    q = heads_major(q_ref)
    k = heads_major(k_ref)
    v = heads_major(v_ref)

    diff = g_c[:, :, :c] - g_r
    decay = jnp.where(lower, jnp.exp(jnp.where(lower, diff, 0.0)), 0.0)
    l_strict = jnp.where(strict, _bmm_nt(k, k) * decay * beta_c[:, :, :c], 0.0)
    attn = jnp.where(lower, _bmm_nt(q, k) * decay, 0.0).astype(BF16)
    t_beta = _unit_lower_inverse(l_strict, eye) * beta_r
    u = _bmm(t_beta.astype(BF16), v)
    w = _bmm((t_beta * jnp.exp(g_r)).astype(BF16), k).astype(BF16)
    q_decay = (q.astype(F32) * jnp.exp(g_c)).astype(BF16)
    k_decay = (k.astype(F32) * jnp.exp(g_last - g_c)).astype(BF16)
    s_decay = jnp.exp(g_last)

    for n in range(nc):
        idx = [hd * nc + n for hd in range(nh)]
        s_old = [state_ref[hd] for hd in range(nh)]
        ws_qs = [jnp.dot(jnp.concatenate([w[b], q_decay[b]], axis=0), s_old[hd].astype(BF16),
                         preferred_element_type=F32) for hd, b in enumerate(idx)]
        v_new = [(u[b] - ws_qs[hd][:c]).astype(BF16) for hd, b in enumerate(idx)]
        for hd, b in enumerate(idx):
            oraw_ref[n * c:(n + 1) * c, hd * DN_HEAD_DIM:(hd + 1) * DN_HEAD_DIM] = (
                ws_qs[hd][c:] + jnp.dot(attn[b], v_new[hd], preferred_element_type=F32))
        for hd, b in enumerate(idx):
            state_ref[hd] = s_old[hd] * s_decay[b] + lax.dot_general(
                k_decay[b], v_new[hd], TN_DIMS, preferred_element_type=F32)

    for hd in range(nh):
        hs = slice(hd * DN_HEAD_DIM, (hd + 1) * DN_HEAD_DIM)
        o = oraw_ref[:, hs]
        ms = jnp.mean(o * o, axis=-1, keepdims=True)
        o = o * lax.rsqrt(ms + EPS) * nw_ref[...]
        o_ref[:, hs] = (o * z_ref[:, hs].astype(F32)).astype(BF16)


def _deltanet(qa, ka, va, za, gcol, grow, dn_norm_w, batch, seq_len):
    tb = TB_DN
    assert seq_len % tb == 0 and tb % CHUNK == 0
    nt = seq_len // tb
    nc = tb // CHUNK
    grow3 = grow.reshape(AB_ROWS, -1, CHUNK).transpose(1, 0, 2)
    row_blk = lambda w: pl.BlockSpec((tb, w), lambda b, t: (b * nt + t, 0))
    return pl.pallas_call(
        _deltanet_kernel,
        grid=(batch, nt),
        in_specs=[row_blk(DN_WIDTH)] * 4 + [
            row_blk(LANES),
            pl.BlockSpec((nc, AB_ROWS, CHUNK), lambda b, t: (b * nt + t, 0, 0)),
            _resident((1, DN_HEAD_DIM))],
        out_specs=row_blk(DN_WIDTH),
        out_shape=jax.ShapeDtypeStruct((batch * seq_len, DN_WIDTH), BF16),
        scratch_shapes=[pltpu.VMEM((DN_HEADS, DN_HEAD_DIM, DN_HEAD_DIM), F32),
                        pltpu.VMEM((tb, DN_WIDTH), F32)],
        compiler_params=pltpu.CompilerParams(
            dimension_semantics=("arbitrary", "arbitrary"), vmem_limit_bytes=VMEM_LIMIT),
        name="gated_deltanet",
    )(qa, ka, va, za, gcol, grow3, dn_norm_w.astype(F32).reshape(1, DN_HEAD_DIM))


def _diffattn_kernel(lq1_ref, lk1_ref, lq2_ref, lk2_ref, q_ref, k_ref, v_ref, z_ref, nw_ref,
                     o_ref, m_ref, l_ref, acc_ref):
    tq = q_ref.shape[0]
    tk = TK_DA
    qi = pl.program_id(2)

    lane = lax.broadcasted_iota(jnp.int32, (1, LANES), 1)
    q = q_ref[...]
    zero = jnp.zeros_like(q)
    q_comp = (jnp.where(lane < DA_QK_DIM, q, zero), jnp.where(lane < DA_QK_DIM, zero, q))

    m_ref[...] = jnp.full(m_ref.shape, NEG_INF, F32)
    l_ref[...] = jnp.zeros(l_ref.shape, F32)
    acc_ref[...] = jnp.zeros(acc_ref.shape, F32)

    def attend(j, masked):
        start = pl.multiple_of(j * tk, tk)
        kblk = k_ref[pl.ds(start, tk), :]
        vblk = v_ref[pl.ds(start, tk), :]
        scores = [lax.dot_general(q_comp[comp], kblk, NT_DIMS, preferred_element_type=F32)
                  for comp in range(2)]
        if masked:
            rq = lax.broadcasted_iota(jnp.int32, (tq, tk), 0)
            ck = lax.broadcasted_iota(jnp.int32, (tq, tk), 1)
            scores = [jnp.where(ck <= rq, s, NEG_INF) for s in scores]
        probs = []
        for comp in range(2):
            s = scores[comp]
            m_prev = m_ref[comp]
            m_new = jnp.maximum(m_prev, jnp.max(s, axis=-1, keepdims=True))
            alpha = jnp.exp2(m_prev - m_new)
            p = jnp.exp2(s - jnp.concatenate([m_new] * (tk // LANES), axis=1))
            l_ref[comp] = alpha * l_ref[comp] + jnp.sum(p, axis=-1, keepdims=True)
            acc_ref[comp] = alpha * acc_ref[comp]
            m_ref[comp] = m_new
            probs.append(p.astype(BF16))
        for comp in range(2):
            acc_ref[comp] += jnp.dot(probs[comp], vblk, preferred_element_type=F32)

    def body(j, carry):
        attend(j, False)
        return carry

    lax.fori_loop(0, qi, body, 0)
    attend(qi, True)


    lam = (jnp.exp(jnp.sum(lq1_ref[...] * lk1_ref[...], axis=-1, keepdims=True))
           - jnp.exp(jnp.sum(lq2_ref[...] * lk2_ref[...], axis=-1, keepdims=True))
           + LAMBDA_INIT)
    o = acc_ref[0] / l_ref[0] - lam * (acc_ref[1] / l_ref[1])
    ms = jnp.mean(o * o, axis=-1, keepdims=True)
    o = o * lax.rsqrt(ms + EPS) * nw_ref[...] * (1.0 - LAMBDA_INIT)
    o_ref[...] = (o * z_ref[...].astype(F32)).astype(BF16)


def _diff_attention(qb, kb, vb, zb, lambda_q1, lambda_k1, lambda_q2, lambda_k2, da_norm_w,
                    batch, seq_len):
    tq = TQ_DA
    assert tq == TK_DA and seq_len % tq == 0
    nq = seq_len // tq
    lam_vec = lambda v: v.astype(F32).reshape(1, DA_QK_DIM)
    q_blk = pl.BlockSpec((tq, LANES), lambda b, h, i: (b * nq + i, h))
    kv_blk = pl.BlockSpec((seq_len, LANES), lambda b, h, i: (b, h))
    return pl.pallas_call(
        _diffattn_kernel,
        grid=(batch, DA_HEADS, nq),
        in_specs=[_resident((1, DA_QK_DIM))] * 4 + [q_blk, kv_blk, kv_blk, q_blk,
                                                    _resident((1, DA_V_DIM))],
        out_specs=q_blk,
        out_shape=jax.ShapeDtypeStruct((batch * seq_len, DA_WIDTH), BF16),
        scratch_shapes=[pltpu.VMEM((2, tq, LANES), F32), pltpu.VMEM((2, tq, LANES), F32),
                        pltpu.VMEM((2, tq, DA_V_DIM), F32)],
        compiler_params=pltpu.CompilerParams(
            dimension_semantics=("arbitrary", "arbitrary", "arbitrary"),
            vmem_limit_bytes=VMEM_LIMIT),
        name="diff_attention",
    )(lam_vec(lambda_q1), lam_vec(lambda_k1), lam_vec(lambda_q2), lam_vec(lambda_k2),
      qb, kb, vb, zb, da_norm_w.astype(F32).reshape(1, DA_V_DIM))


def _output_kernel(oa_ref, ob_ref, ga_ref, gb_ref, x_ref, woa_ref, wob_ref, wo_ref, out_ref):
    y_a = jnp.dot(oa_ref[...], woa_ref[...], preferred_element_type=F32)
    y_b = jnp.dot(ob_ref[...], wob_ref[...], preferred_element_type=F32)
    y = ga_ref[...].astype(F32) * y_a + gb_ref[...].astype(F32) * y_b
    out_ref[...] = x_ref[...] + jnp.dot(y.astype(BF16), wo_ref[...], preferred_element_type=F32)


def _output_projection(oa, ob, ga, gb, x2, w_out_a, w_out_b, w_out):
    m = x2.shape[0]
    tm = TM_OUT
    assert m % tm == 0
    row_blk = lambda w: pl.BlockSpec((tm, w), lambda i: (i, 0))
    return pl.pallas_call(
        _output_kernel,
        grid=(m // tm,),
        in_specs=[row_blk(DN_WIDTH), row_blk(DA_WIDTH), row_blk(D_MODEL), row_blk(D_MODEL),
                  row_blk(D_MODEL), _resident((DN_WIDTH, D_MODEL)), _resident((DA_WIDTH, D_MODEL)),
                  _resident((D_MODEL, D_MODEL))],
        out_specs=row_blk(D_MODEL),
        out_shape=jax.ShapeDtypeStruct((m, D_MODEL), F32),
        compiler_params=pltpu.CompilerParams(
            dimension_semantics=("arbitrary",), vmem_limit_bytes=VMEM_LIMIT),
        name="output_projection",
    )(oa, ob, ga, gb, x2, w_out_a.astype(BF16), w_out_b.astype(BF16), w_out.astype(BF16))


def kernel(x, norm_w, w_in, conv_w, a_log, dt_bias, dn_norm_w, q_norm_w, k_norm_w,
           lambda_q1, lambda_k1, lambda_q2, lambda_k2, da_norm_w, w_out_a, w_out_b, w_out):
    batch, seq_len, d_model = x.shape
    assert d_model == D_MODEL
    x2 = x.reshape(batch * seq_len, d_model)
    (qa, ka, va, za, gcol, grow, qb, kb, vb, zb, ga, gb) = _in_projection(
        x2, norm_w, w_in, conv_w, a_log, dt_bias, q_norm_w, k_norm_w, seq_len)
    oa = _deltanet(qa, ka, va, za, gcol, grow, dn_norm_w, batch, seq_len)
    ob = _diff_attention(qb, kb, vb, zb, lambda_q1, lambda_k1, lambda_q2, lambda_k2, da_norm_w,
                         batch, seq_len)
    out = _output_projection(oa, ob, ga, gb, x2, w_out_a, w_out_b, w_out)
    return out.reshape(batch, seq_len, d_model)
```

```python
import functools
import math

import jax
import jax.numpy as jnp
from jax import lax
from jax.experimental import pallas as pl
from jax.experimental.pallas import tpu as pltpu

F32 = jnp.float32
BF16 = jnp.bfloat16

D_MODEL = 1024
DN_HEADS = 4
DN_HEAD_DIM = 128
DN_WIDTH = DN_HEADS * DN_HEAD_DIM
CONV_WIDTH = 4
CHUNK = 64
DA_HEADS = 4
DA_QK_DIM = 64
DA_V_DIM = 2 * DA_QK_DIM
DA_WIDTH = DA_HEADS * DA_V_DIM
DA_QK_WIDTH = DA_HEADS * 2 * DA_QK_DIM
LAMBDA_INIT = 0.8 - 0.6 * math.exp(-0.3 * 0)
EPS = 1e-6
NEG_INF = -1e30
LOG2_E = math.log2(math.e)

LANES = 128
MXU_COLS = 256
SUBLANES = 8
BF16_ROWS = 16
AB_ROWS = BF16_ROWS
VMEM_LIMIT = 48 * 1024 * 1024

TM_IN = 512
TB_DN = 512
TQ_DA = 512
TK_DA = 512
TM_OUT = 512

NT_DIMS = (((1,), (1,)), ((), ()))
TN_DIMS = (((0,), (0,)), ((), ()))


def _sigmoid(v):
    return 0.5 + 0.5 * jnp.tanh(0.5 * v)


def _silu(v):
    h = 0.5 * v
    return h + h * jnp.tanh(h)


def _softplus(v):
    return jnp.maximum(v, 0.0) + jnp.log1p(jnp.exp(-jnp.abs(v)))


def _resident(shape):
    zeros = (0,) * len(shape)
    return pl.BlockSpec(shape, lambda *_: zeros, pipeline_mode=pl.Buffered(1))


def _inproj_kernel(x_ref, nw_ref, wqkv_ref, convw_ref, wza_ref, wab_ref, wabt_ref,
                   alog_l_ref, dtb_l_ref, alog_s_ref, dtb_s_ref, qnw_ref, knw_ref,
                   wqb_ref, wkb_ref, wvb_ref, wzb_ref, wga_ref, wgb_ref,
                   qa_ref, ka_ref, va_ref, za_ref, gcol_ref, grow_ref,
                   qb_ref, kb_ref, vb_ref, zb_ref, ga_ref, gb_ref,
                   conv_buf, *, tiles_per_seq):
    tm = x_ref.shape[0]
    half = tm // 2
    i = pl.program_id(0)
    head0 = SUBLANES - (CONV_WIDTH - 1)
    slab = 512

    @pl.when(i % tiles_per_seq == 0)
    def _():
        conv_buf[0:SUBLANES, :] = jnp.zeros((SUBLANES, 3 * DN_WIDTH), F32)

    def normed(r0):
        x = x_ref[r0:r0 + half, :]
        ms = jnp.mean(x * x, axis=-1, keepdims=True)
        return (x * lax.rsqrt(ms + EPS) * nw_ref[...]).astype(BF16)

    def tasks(h, r0):
        rows = slice(r0, r0 + half)

        def conv_qkv(s, out_ref):
            cs = slice(s * DN_WIDTH, (s + 1) * DN_WIDTH)

            res = []

            def matmul():
                res.append(jnp.dot(h, wqkv_ref[:, cs], preferred_element_type=F32))
                conv_buf[SUBLANES + r0:SUBLANES + r0 + half, cs] = res[0]

            def epilogue():
                acc = convw_ref[CONV_WIDTH - 1:CONV_WIDTH, cs] * res[0]
                for j in range(CONV_WIDTH - 1):
                    acc = acc + (convw_ref[j:j + 1, cs]
                                 * conv_buf[head0 + j + r0:head0 + j + r0 + half, cs])
                y = _silu(acc)
                if s == 2:
                    out_ref[rows, :] = y.astype(BF16)
                    return
                scale = DN_HEAD_DIM ** -0.5 if s == 0 else 1.0
                for hd in range(DN_HEADS):
                    hs = slice(hd * DN_HEAD_DIM, (hd + 1) * DN_HEAD_DIM)
                    blk = y[:, hs]
                    ss = jnp.sum(blk * blk, axis=-1, keepdims=True)
                    out_ref[rows, hs] = (blk * (lax.rsqrt(ss + EPS) * scale)).astype(BF16)
            return matmul, [epilogue]

        def plain(w_ref, c0, out_ref, act):
            res = []

            def matmul():
                res.append(jnp.dot(h, w_ref[:, c0:c0 + slab], preferred_element_type=F32))

            def chunk(n):
                def run():
                    cs = slice(n * LANES, (n + 1) * LANES)
                    out_ref[rows, c0 + n * LANES:c0 + (n + 1) * LANES] = (
                        act(res[0][:, cs]).astype(BF16))
                return run
            return matmul, [chunk(n) for n in range(slab // LANES)]

        def qk_norm(w_ref, nrm_ref, out_ref, scale):
            res = []

            def matmul():
                res.append(jnp.dot(h, w_ref[...], preferred_element_type=F32))

            def chunk(hd):
                def run():
                    lo = lax.broadcasted_iota(jnp.int32, (half, LANES), 1) < DA_QK_DIM
                    blk = res[0][:, hd * LANES:(hd + 1) * LANES]
                    sq = blk * blk
                    s_lo = jnp.sum(jnp.where(lo, sq, 0.0), axis=-1, keepdims=True)
                    s_hi = jnp.sum(jnp.where(lo, 0.0, sq), axis=-1, keepdims=True)
                    msq = jnp.where(lo, s_lo, s_hi) * (1.0 / DA_QK_DIM)
                    out_ref[rows, hd * LANES:(hd + 1) * LANES] = (
                        blk * lax.rsqrt(msq + EPS) * nrm_ref[...] * scale).astype(BF16)
                return run
            return matmul, [chunk(hd) for hd in range(DA_HEADS)]

        def decay_beta():
            def matmul():
                ab = jnp.dot(h, wab_ref[...], preferred_element_type=F32)
                lane = lax.broadcasted_iota(jnp.int32, ab.shape, 1)
                g_l = -jnp.exp(alog_l_ref[...]) * _softplus(ab + dtb_l_ref[...])
                gcol_ref[rows, :] = jnp.where(lane < DN_HEADS, g_l, _sigmoid(ab))
                abt = lax.dot_general(wabt_ref[...], h, NT_DIMS,
                                      preferred_element_type=F32)
                row = lax.broadcasted_iota(jnp.int32, abt.shape, 0)
                g_s = -jnp.exp(alog_s_ref[...]) * _softplus(abt + dtb_s_ref[...])
                grow_ref[:, rows] = jnp.where(row < DN_HEADS, g_s, _sigmoid(abt))
            return matmul, []

        ident = lambda v: v
        return [conv_qkv(0, qa_ref), plain(wga_ref, 0, ga_ref, _sigmoid),
                conv_qkv(1, ka_ref), plain(wga_ref, slab, ga_ref, _sigmoid),
                conv_qkv(2, va_ref), plain(wgb_ref, 0, gb_ref, _sigmoid),
                qk_norm(wqb_ref, qnw_ref, qb_ref, DA_QK_DIM ** -0.5 * LOG2_E),
                plain(wgb_ref, slab, gb_ref, _sigmoid),
                qk_norm(wkb_ref, knw_ref, kb_ref, 1.0), plain(wvb_ref, 0, vb_ref, ident),
                plain(wza_ref, 0, za_ref, _silu), plain(wzb_ref, 0, zb_ref, _silu),
                decay_beta()]

    def run_tasks(task_list, extra=None):
        for n, (matmul, chunks) in enumerate(task_list):
            matmul()
            for fn in chunks:
                fn()
            if extra is not None and n == 8:
                extra()

    h_second = []
    run_tasks(tasks(normed(0), 0), extra=lambda: h_second.append(normed(half)))
    run_tasks(tasks(h_second[0], half))
    conv_buf[0:SUBLANES, :] = conv_buf[tm:tm + SUBLANES, :]


def _in_projection(x2, norm_w, w_in, conv_w, a_log, dt_bias, q_norm_w, k_norm_w, seq_len):
    m = x2.shape[0]
    tm = TM_IN
    assert m % tm == 0 and seq_len % tm == 0
    o = 0
    cols = {}
    for name, width in (("qkv", 3 * DN_WIDTH), ("za", DN_WIDTH), ("a", DN_HEADS), ("b", DN_HEADS),
                        ("qb", DA_QK_WIDTH), ("kb", DA_QK_WIDTH), ("vb", DA_WIDTH), ("zb", DA_WIDTH),
                        ("ga", D_MODEL), ("gb", D_MODEL)):
        cols[name] = w_in[:, o:o + width]
        o += width
    assert o == w_in.shape[1]
    wb = {k: v.astype(BF16) for k, v in cols.items()}
    w_ab = jnp.concatenate([cols["a"], cols["b"]], axis=1)
    w_ab_l = jnp.pad(w_ab, ((0, 0), (0, LANES - 2 * DN_HEADS))).astype(BF16)
    w_ab_s = jnp.pad(w_ab.T, ((0, AB_ROWS - 2 * DN_HEADS), (0, 0))).astype(BF16)
    alog_l = jnp.pad(a_log.astype(F32), (0, LANES - DN_HEADS)).reshape(1, LANES)
    dtb_l = jnp.pad(dt_bias.astype(F32), (0, LANES - DN_HEADS)).reshape(1, LANES)
    alog_s = jnp.pad(a_log.astype(F32), (0, AB_ROWS - DN_HEADS)).reshape(AB_ROWS, 1)
    dtb_s = jnp.pad(dt_bias.astype(F32), (0, AB_ROWS - DN_HEADS)).reshape(AB_ROWS, 1)
    qnw = jnp.tile(q_norm_w.astype(F32), 2).reshape(1, LANES)
    knw = jnp.tile(k_norm_w.astype(F32), 2).reshape(1, LANES)

    row_blk = lambda w: pl.BlockSpec((tm, w), lambda i: (i, 0))
    in_specs = [
        row_blk(D_MODEL), _resident((1, D_MODEL)),
        _resident((D_MODEL, 3 * DN_WIDTH)), _resident((CONV_WIDTH, 3 * DN_WIDTH)),
        _resident((D_MODEL, DN_WIDTH)), _resident((D_MODEL, LANES)), _resident((AB_ROWS, D_MODEL)),
        _resident((1, LANES)), _resident((1, LANES)), _resident((AB_ROWS, 1)), _resident((AB_ROWS, 1)),
        _resident((1, LANES)), _resident((1, LANES)),
        _resident((D_MODEL, DA_QK_WIDTH)), _resident((D_MODEL, DA_QK_WIDTH)),
        _resident((D_MODEL, DA_WIDTH)), _resident((D_MODEL, DA_WIDTH)),
        _resident((D_MODEL, D_MODEL)), _resident((D_MODEL, D_MODEL)),
    ]
    out_shapes = [jax.ShapeDtypeStruct((m, DN_WIDTH), BF16)] * 4 + [
        jax.ShapeDtypeStruct((m, LANES), F32), jax.ShapeDtypeStruct((AB_ROWS, m), F32)] + [
        jax.ShapeDtypeStruct((m, DA_WIDTH), BF16)] * 4 + [
        jax.ShapeDtypeStruct((m, D_MODEL), BF16)] * 2
    out_specs = [row_blk(DN_WIDTH)] * 4 + [
        row_blk(LANES), pl.BlockSpec((AB_ROWS, tm), lambda i: (0, i))] + [
        row_blk(DA_WIDTH)] * 4 + [row_blk(D_MODEL)] * 2
    return pl.pallas_call(
        functools.partial(_inproj_kernel, tiles_per_seq=seq_len // tm),
        grid=(m // tm,),
        in_specs=in_specs,
        out_specs=out_specs,
        out_shape=out_shapes,
        scratch_shapes=[pltpu.VMEM((tm + SUBLANES, 3 * DN_WIDTH), F32)],
        compiler_params=pltpu.CompilerParams(
            dimension_semantics=("arbitrary",), vmem_limit_bytes=VMEM_LIMIT),
        name="in_projection",
    )(x2, norm_w.astype(F32).reshape(1, D_MODEL), wb["qkv"], conv_w.astype(F32), wb["za"],
      w_ab_l, w_ab_s, alog_l, dtb_l, alog_s, dtb_s, qnw, knw,
      wb["qb"], wb["kb"], wb["vb"], wb["zb"], wb["ga"], wb["gb"])


def _bmm(a, b):
    return jnp.einsum('nij,njk->nik', a, b, preferred_element_type=F32)


def _bmm_nt(a, b):
    return jnp.einsum('nid,njd->nij', a, b, preferred_element_type=F32)


def _split_bf16(x, parts):
    out = []
    for _ in range(parts - 1):
        hi = x.astype(BF16)
        out.append(hi)
        x = x - hi.astype(F32)
    out.append(x.astype(BF16))
    return out


def _deltanet_kernel(q_ref, k_ref, v_ref, z_ref, gcol_ref, grow_ref, nw_ref, o_ref,
                     state_ref, oraw_ref, lhs_ref, attn_ref, u_ref, kdec_ref, sdec_ref, *, nt):
    tb = q_ref.shape[0]
    c = CHUNK
    nc = tb // c
    nh = DN_HEADS
    t = pl.program_id(1)

    ri = lax.broadcasted_iota(jnp.int32, (c, c), 0)
    ci = lax.broadcasted_iota(jnp.int32, (c, c), 1)
    lower = ri >= ci
    strict = ri > ci
    eye = (ri == ci).astype(F32)
    tri = lower.astype(BF16)
    tri_t = (ri <= ci).astype(BF16)

    def heads_major(ref):
        return jnp.concatenate(
            [ref[:, hd * DN_HEAD_DIM:(hd + 1) * DN_HEAD_DIM].reshape(nc, c, DN_HEAD_DIM)
             for hd in range(nh)], axis=0)

    def phase1_stages():
        e = {}

        def decays():
            gb_col = gcol_ref[...].reshape(nc, c, LANES)
            tri_b = jnp.broadcast_to(tri, (nc, c, c))
            gc_col = sum(_bmm(tri_b, part) for part in _split_bf16(gb_col, 3))
            gb_row = grow_ref[...]
            gc_row = sum(jnp.dot(part, tri_t, preferred_element_type=F32)
                         for part in _split_bf16(gb_row.reshape(nc * AB_ROWS, c), 3)
                         ).reshape(nc, AB_ROWS, c)

            def col_form(src, lane):
                return jnp.concatenate(
                    [jnp.broadcast_to(src[:, :, lane + hd:lane + hd + 1], (nc, c, LANES))
                     for hd in range(nh)], axis=0)

            def row_form(src, row):
                return jnp.concatenate(
                    [src[:, row + hd:row + hd + 1, :] for hd in range(nh)], axis=0)

            e["g_c"] = col_form(gc_col, 0)
            e["beta_c"] = col_form(gb_col, nh)
            e["g_r"] = row_form(gc_row, 0)
            e["beta_r"] = row_form(gb_row, nh)
            e["g_last"] = e["g_c"][:, c - 1:c, :]
            diff = e["g_c"][:, :, :c] - e["g_r"]
            e["decay"] = jnp.where(lower, jnp.exp(jnp.where(lower, diff, 0.0)), 0.0)

        def gram_kk():
            e["q"], e["k"], e["v"] = heads_major(q_ref), heads_major(k_ref), heads_major(v_ref)
            l_strict = jnp.where(
                strict, _bmm_nt(e["k"], e["k"]) * e["decay"] * e["beta_c"][:, :, :c], 0.0)
            e["inv"] = eye - l_strict
            e["power"] = l_strict.astype(BF16)

        def gram_qk():
            e["attn"] = jnp.where(lower, _bmm_nt(e["q"], e["k"]) * e["decay"], 0.0).astype(BF16)

        def next_power():
            e["power"] = _bmm(e["power"], e["power"]).astype(BF16)

        def apply_power():
            e["inv"] = e["inv"] + _bmm(e["inv"].astype(BF16), e["power"])

        def solve_u():
            e["t_beta"] = e["inv"] * e["beta_r"]
            e["u"] = _bmm(e["t_beta"].astype(BF16), e["v"])

        def solve_w():
            e["w"] = _bmm((e["t_beta"] * jnp.exp(e["g_r"])).astype(BF16), e["k"]).astype(BF16)

        def commit():
            q_decay = (e["q"].astype(F32) * jnp.exp(e["g_c"])).astype(BF16)
            lhs_ref[...] = jnp.concatenate([e["w"], q_decay], axis=1)
            attn_ref[...] = e["attn"]
            u_ref[...] = e["u"]
            kdec_ref[...] = (e["k"].astype(F32) * jnp.exp(e["g_last"] - e["g_c"])).astype(BF16)
            sdec_ref[...] = jnp.exp(e["g_last"])

        levels = [next_power, apply_power] * (c.bit_length() - 2)
        return [decays, gram_kk, gram_qk] + levels + [solve_u, solve_w], commit

    def phase2_stages():
        def chunk_steps(n):
            idx = [hd * nc + n for hd in range(nh)]
            e = {}

            def through_state():
                e["s_old"] = [state_ref[hd] for hd in range(nh)]
                e["ws_qs"] = [jnp.dot(lhs_ref[b], e["s_old"][hd].astype(BF16),
                                      preferred_element_type=F32) for hd, b in enumerate(idx)]

            def update():
                v_new = [(u_ref[b] - e["ws_qs"][hd][:c]).astype(BF16) for hd, b in enumerate(idx)]
                for hd, b in enumerate(idx):
                    oraw_ref[n * c:(n + 1) * c, hd * DN_HEAD_DIM:(hd + 1) * DN_HEAD_DIM] = (
                        e["ws_qs"][hd][c:]
                        + jnp.dot(attn_ref[b], v_new[hd], preferred_element_type=F32))
                for hd, b in enumerate(idx):
                    state_ref[hd] = e["s_old"][hd] * sdec_ref[b] + lax.dot_general(
                        kdec_ref[b], v_new[hd], TN_DIMS, preferred_element_type=F32)
            return [through_state, update]

        def gated_norm():
            for hd in range(nh):
                hs = slice(hd * DN_HEAD_DIM, (hd + 1) * DN_HEAD_DIM)
                o = oraw_ref[:, hs]
                ms = jnp.mean(o * o, axis=-1, keepdims=True)
                o = o * lax.rsqrt(ms + EPS) * nw_ref[...]
                o_ref[:, hs] = (o * z_ref[:, hs].astype(F32)).astype(BF16)

        return [fn for n in range(nc) for fn in chunk_steps(n)] + [gated_norm]

    @pl.when(t == 0)
    def _():
        state_ref[...] = jnp.zeros(state_ref.shape, F32)
        stages, commit = phase1_stages()
        for fn in stages:
            fn()
        commit()

    @pl.when((t > 0) & (t < nt))
    def _():
        stages, commit = phase1_stages()
        others = phase2_stages()
        done = 0
        for n, fn in enumerate(stages):
            fn()
            upto = -(-(n + 1) * len(others) // len(stages))
            for other in others[done:upto]:
                other()
            done = upto
        commit()

    @pl.when(t == nt)
    def _():
        for fn in phase2_stages():
            fn()


def _deltanet(qa, ka, va, za, gcol, grow, dn_norm_w, batch, seq_len):
    tb = TB_DN
    assert seq_len % tb == 0 and tb % CHUNK == 0
    nt = seq_len // tb
    nc = tb // CHUNK
    nb = DN_HEADS * nc
    grow3 = grow.reshape(AB_ROWS, -1, CHUNK).transpose(1, 0, 2)
    cur = lambda w: pl.BlockSpec((tb, w), lambda b, t: (b * nt + jnp.minimum(t, nt - 1), 0))
    prev = lambda w: pl.BlockSpec((tb, w), lambda b, t: (b * nt + jnp.maximum(t - 1, 0), 0))
    return pl.pallas_call(
        functools.partial(_deltanet_kernel, nt=nt),
        grid=(batch, nt + 1),
        in_specs=[cur(DN_WIDTH)] * 3 + [
            prev(DN_WIDTH), cur(LANES),
            pl.BlockSpec((nc, AB_ROWS, CHUNK),
                         lambda b, t: (b * nt + jnp.minimum(t, nt - 1), 0, 0)),
            _resident((1, DN_HEAD_DIM))],
        out_specs=prev(DN_WIDTH),
        out_shape=jax.ShapeDtypeStruct((batch * seq_len, DN_WIDTH), BF16),
        scratch_shapes=[pltpu.VMEM((DN_HEADS, DN_HEAD_DIM, DN_HEAD_DIM), F32),
                        pltpu.VMEM((tb, DN_WIDTH), F32),
                        pltpu.VMEM((nb, 2 * CHUNK, DN_HEAD_DIM), BF16),
                        pltpu.VMEM((nb, CHUNK, CHUNK), BF16),
                        pltpu.VMEM((nb, CHUNK, DN_HEAD_DIM), F32),
                        pltpu.VMEM((nb, CHUNK, DN_HEAD_DIM), BF16),
                        pltpu.VMEM((nb, 1, DN_HEAD_DIM), F32)],
        compiler_params=pltpu.CompilerParams(
            dimension_semantics=("arbitrary", "arbitrary"), vmem_limit_bytes=VMEM_LIMIT),
        name="gated_deltanet",
    )(qa, ka, va, za, gcol, grow3, dn_norm_w.astype(F32).reshape(1, DN_HEAD_DIM))


def _diffattn_kernel(lq1_ref, lk1_ref, lq2_ref, lk2_ref, q_ref, k_ref, v_ref, z_ref, nw_ref,
                     o_ref, m_ref, l_ref, acc_ref):
    tq = q_ref.shape[0]
    tk = TK_DA
    qi = pl.program_id(2)

    lane = lax.broadcasted_iota(jnp.int32, (1, LANES), 1)
    q = q_ref[...]
    zero = jnp.zeros_like(q)
    q_comp = (jnp.where(lane < DA_QK_DIM, q, zero), jnp.where(lane < DA_QK_DIM, zero, q))

    m_ref[...] = jnp.full(m_ref.shape, NEG_INF, F32)
    l_ref[...] = jnp.zeros(l_ref.shape, F32)
    acc_ref[...] = jnp.zeros(acc_ref.shape, F32)

    def attend(j, masked):
        start = pl.multiple_of(j * tk, tk)
        kblk = k_ref[pl.ds(start, tk), :]
        vblk = v_ref[pl.ds(start, tk), :]
        scores = [lax.dot_general(q_comp[comp], kblk, NT_DIMS, preferred_element_type=F32)
                  for comp in range(2)]
        if masked:
            rq = lax.broadcasted_iota(jnp.int32, (tq, tk), 0)
            ck = lax.broadcasted_iota(jnp.int32, (tq, tk), 1)
            scores = [jnp.where(ck <= rq, s, NEG_INF) for s in scores]
        probs = []
        for comp in range(2):
            s = scores[comp]
            m_prev = m_ref[comp]
            m_new = jnp.maximum(m_prev, jnp.max(s, axis=-1, keepdims=True))
            alpha = jnp.exp2(m_prev - m_new)
            p = jnp.exp2(s - jnp.concatenate([m_new] * (tk // LANES), axis=1))
            l_ref[comp] = alpha * l_ref[comp] + jnp.sum(p, axis=-1, keepdims=True)
            acc_ref[comp] = alpha * acc_ref[comp]
            m_ref[comp] = m_new
            probs.append(p.astype(BF16))
        for comp in range(2):
            acc_ref[comp] += jnp.dot(probs[comp], vblk, preferred_element_type=F32)

    def body(j, carry):
        attend(j, False)
        return carry

    lax.fori_loop(0, qi, body, 0)
    attend(qi, True)


    lam = (jnp.exp(jnp.sum(lq1_ref[...] * lk1_ref[...], axis=-1, keepdims=True))
           - jnp.exp(jnp.sum(lq2_ref[...] * lk2_ref[...], axis=-1, keepdims=True))
           + LAMBDA_INIT)
    o = acc_ref[0] / l_ref[0] - lam * (acc_ref[1] / l_ref[1])
    ms = jnp.mean(o * o, axis=-1, keepdims=True)
    o = o * lax.rsqrt(ms + EPS) * nw_ref[...] * (1.0 - LAMBDA_INIT)
    o_ref[...] = (o * z_ref[...].astype(F32)).astype(BF16)


def _diff_attention(qb, kb, vb, zb, lambda_q1, lambda_k1, lambda_q2, lambda_k2, da_norm_w,
                    batch, seq_len):
    tq = TQ_DA
    assert tq == TK_DA and seq_len % tq == 0
    nq = seq_len // tq
    lam_vec = lambda v: v.astype(F32).reshape(1, DA_QK_DIM)
    q_blk = pl.BlockSpec((tq, LANES), lambda b, h, i: (b * nq + i, h))
    kv_blk = pl.BlockSpec((seq_len, LANES), lambda b, h, i: (b, h))
    return pl.pallas_call(
        _diffattn_kernel,
        grid=(batch, DA_HEADS, nq),
        in_specs=[_resident((1, DA_QK_DIM))] * 4 + [q_blk, kv_blk, kv_blk, q_blk,
                                                    _resident((1, DA_V_DIM))],
        out_specs=q_blk,
        out_shape=jax.ShapeDtypeStruct((batch * seq_len, DA_WIDTH), BF16),
        scratch_shapes=[pltpu.VMEM((2, tq, LANES), F32), pltpu.VMEM((2, tq, LANES), F32),
                        pltpu.VMEM((2, tq, DA_V_DIM), F32)],
        compiler_params=pltpu.CompilerParams(
            dimension_semantics=("arbitrary", "arbitrary", "arbitrary"),
            vmem_limit_bytes=VMEM_LIMIT),
        name="diff_attention",
    )(lam_vec(lambda_q1), lam_vec(lambda_k1), lam_vec(lambda_q2), lam_vec(lambda_k2),
      qb, kb, vb, zb, da_norm_w.astype(F32).reshape(1, DA_V_DIM))


def _output_kernel(oa_ref, ob_ref, ga_ref, gb_ref, x_ref, woa_ref, wob_ref, wo_ref, out_ref):
    y_a = jnp.dot(oa_ref[...], woa_ref[...], preferred_element_type=F32)
    y_b = jnp.dot(ob_ref[...], wob_ref[...], preferred_element_type=F32)
    y = ga_ref[...].astype(F32) * y_a + gb_ref[...].astype(F32) * y_b
    out_ref[...] = x_ref[...] + jnp.dot(y.astype(BF16), wo_ref[...], preferred_element_type=F32)


def _output_projection(oa, ob, ga, gb, x2, w_out_a, w_out_b, w_out):
    m = x2.shape[0]
    tm = TM_OUT
    assert m % tm == 0
    row_blk = lambda w: pl.BlockSpec((tm, w), lambda i: (i, 0))
    return pl.pallas_call(
        _output_kernel,
        grid=(m // tm,),
        in_specs=[row_blk(DN_WIDTH), row_blk(DA_WIDTH), row_blk(D_MODEL), row_blk(D_MODEL),
                  row_blk(D_MODEL), _resident((DN_WIDTH, D_MODEL)), _resident((DA_WIDTH, D_MODEL)),
                  _resident((D_MODEL, D_MODEL))],
        out_specs=row_blk(D_MODEL),
        out_shape=jax.ShapeDtypeStruct((m, D_MODEL), F32),
        compiler_params=pltpu.CompilerParams(
            dimension_semantics=("arbitrary",), vmem_limit_bytes=VMEM_LIMIT),
        name="output_projection",
    )(oa, ob, ga, gb, x2, w_out_a.astype(BF16), w_out_b.astype(BF16), w_out.astype(BF16))


def kernel(x, norm_w, w_in, conv_w, a_log, dt_bias, dn_norm_w, q_norm_w, k_norm_w,
           lambda_q1, lambda_k1, lambda_q2, lambda_k2, da_norm_w, w_out_a, w_out_b, w_out):
    batch, seq_len, d_model = x.shape
    assert d_model == D_MODEL
    x2 = x.reshape(batch * seq_len, d_model)
    (qa, ka, va, za, gcol, grow, qb, kb, vb, zb, ga, gb) = _in_projection(
        x2, norm_w, w_in, conv_w, a_log, dt_bias, q_norm_w, k_norm_w, seq_len)
    oa = _deltanet(qa, ka, va, za, gcol, grow, dn_norm_w, batch, seq_len)
    ob = _diff_attention(qb, kb, vb, zb, lambda_q1, lambda_k1, lambda_q2, lambda_k2, da_norm_w,
                         batch, seq_len)
    out = _output_projection(oa, ob, ga, gb, x2, w_out_a, w_out_b, w_out)
    return out.reshape(batch, seq_len, d_model)
```

```python
import functools
import math

import jax
import jax.numpy as jnp
from jax import lax
from jax.experimental import pallas as pl
from jax.experimental.pallas import tpu as pltpu

F32 = jnp.float32
BF16 = jnp.bfloat16

D_MODEL = 1024
DN_HEADS = 4
DN_HEAD_DIM = 128
DN_WIDTH = DN_HEADS * DN_HEAD_DIM
CONV_WIDTH = 4
CHUNK = 64
DA_HEADS = 4
DA_QK_DIM = 64
DA_V_DIM = 2 * DA_QK_DIM
DA_WIDTH = DA_HEADS * DA_V_DIM
DA_QK_WIDTH = DA_HEADS * 2 * DA_QK_DIM
LAMBDA_INIT = 0.8 - 0.6 * math.exp(-0.3 * 0)
EPS = 1e-6
NEG_INF = -1e30
LOG2_E = math.log2(math.e)

LANES = 128
MXU_COLS = 256
SUBLANES = 8
BF16_ROWS = 16
AB_ROWS = BF16_ROWS
VMEM_LIMIT = 48 * 1024 * 1024

TM_IN = 512
TB_DN = 512
TQ_DA = 512
TK_DA = 512
TM_OUT = 512

NT_DIMS = (((1,), (1,)), ((), ()))
TN_DIMS = (((0,), (0,)), ((), ()))


def _sigmoid(v):
    return 0.5 + 0.5 * jnp.tanh(0.5 * v)


def _silu(v):
    h = 0.5 * v
    return h + h * jnp.tanh(h)


def _softplus(v):
    return jnp.maximum(v, 0.0) + jnp.log1p(jnp.exp(-jnp.abs(v)))


def _resident(shape):
    zeros = (0,) * len(shape)
    return pl.BlockSpec(shape, lambda *_: zeros, pipeline_mode=pl.Buffered(1))


def _inproj_kernel(x_ref, nw_ref, wqkv_ref, convw_ref, wza_ref, wab_ref, wabt_ref,
                   alog_l_ref, dtb_l_ref, alog_s_ref, dtb_s_ref, qnw_ref, knw_ref,
                   wqb_ref, wkb_ref, wvb_ref, wzb_ref, wga_ref, wgb_ref,
                   qa_ref, ka_ref, va_ref, za_ref, gcol_ref, grow_ref,
                   qb_ref, kb_ref, vb_ref, zb_ref, ga_ref, gb_ref,
                   conv_buf, *, tiles_per_seq):
    tm = x_ref.shape[0]
    half = tm // 2
    i = pl.program_id(0)
    head0 = SUBLANES - (CONV_WIDTH - 1)
    slab = 512

    @pl.when(i % tiles_per_seq == 0)
    def _():
        conv_buf[0:SUBLANES, :] = jnp.zeros((SUBLANES, 3 * DN_WIDTH), F32)

    def normed(r0):
        x = x_ref[r0:r0 + half, :]
        ms = jnp.mean(x * x, axis=-1, keepdims=True)
        return (x * lax.rsqrt(ms + EPS) * nw_ref[...]).astype(BF16)

    def tasks(h, r0):
        rows = slice(r0, r0 + half)

        def conv_qkv(s, out_ref):
            cs = slice(s * DN_WIDTH, (s + 1) * DN_WIDTH)

            res = []

            def matmul():
                res.append(jnp.dot(h, wqkv_ref[:, cs], preferred_element_type=F32))
                conv_buf[SUBLANES + r0:SUBLANES + r0 + half, cs] = res[0]

            def epilogue():
                acc = convw_ref[CONV_WIDTH - 1:CONV_WIDTH, cs] * res[0]
                for j in range(CONV_WIDTH - 1):
                    acc = acc + (convw_ref[j:j + 1, cs]
                                 * conv_buf[head0 + j + r0:head0 + j + r0 + half, cs])
                y = _silu(acc)
                if s == 2:
                    out_ref[rows, :] = y.astype(BF16)
                    return
                scale = DN_HEAD_DIM ** -0.5 if s == 0 else 1.0
                for hd in range(DN_HEADS):
                    hs = slice(hd * DN_HEAD_DIM, (hd + 1) * DN_HEAD_DIM)
                    blk = y[:, hs]
                    ss = jnp.sum(blk * blk, axis=-1, keepdims=True)
                    out_ref[rows, hs] = (blk * (lax.rsqrt(ss + EPS) * scale)).astype(BF16)
            return matmul, [epilogue]

        def plain(w_ref, c0, out_ref, act):
            res = []

            def matmul():
                res.append(jnp.dot(h, w_ref[:, c0:c0 + slab], preferred_element_type=F32))

            def chunk(n):
                def run():
                    cs = slice(n * LANES, (n + 1) * LANES)
                    out_ref[rows, c0 + n * LANES:c0 + (n + 1) * LANES] = (
                        act(res[0][:, cs]).astype(BF16))
                return run
            return matmul, [chunk(n) for n in range(slab // LANES)]

        def qk_norm(w_ref, nrm_ref, out_ref, scale):
            res = []

            def matmul():
                res.append(jnp.dot(h, w_ref[...], preferred_element_type=F32))

            def chunk(hd):
                def run():
                    lo = lax.broadcasted_iota(jnp.int32, (half, LANES), 1) < DA_QK_DIM
                    blk = res[0][:, hd * LANES:(hd + 1) * LANES]
                    sq = blk * blk
                    s_lo = jnp.sum(jnp.where(lo, sq, 0.0), axis=-1, keepdims=True)
                    s_hi = jnp.sum(jnp.where(lo, 0.0, sq), axis=-1, keepdims=True)
                    msq = jnp.where(lo, s_lo, s_hi) * (1.0 / DA_QK_DIM)
                    out_ref[rows, hd * LANES:(hd + 1) * LANES] = (
                        blk * lax.rsqrt(msq + EPS) * nrm_ref[...] * scale).astype(BF16)
                return run
            return matmul, [chunk(hd) for hd in range(DA_HEADS)]

        def decay_beta():
            def matmul():
                ab = jnp.dot(h, wab_ref[...], preferred_element_type=F32)
                lane = lax.broadcasted_iota(jnp.int32, ab.shape, 1)
                g_l = -jnp.exp(alog_l_ref[...]) * _softplus(ab + dtb_l_ref[...])
                gcol_ref[rows, :] = jnp.where(lane < DN_HEADS, g_l, _sigmoid(ab))
                abt = lax.dot_general(wabt_ref[...], h, NT_DIMS,
                                      preferred_element_type=F32)
                row = lax.broadcasted_iota(jnp.int32, abt.shape, 0)
                g_s = -jnp.exp(alog_s_ref[...]) * _softplus(abt + dtb_s_ref[...])
                grow_ref[:, rows] = jnp.where(row < DN_HEADS, g_s, _sigmoid(abt))
            return matmul, []

        ident = lambda v: v
        return [conv_qkv(0, qa_ref), plain(wga_ref, 0, ga_ref, _sigmoid),
                conv_qkv(1, ka_ref), plain(wga_ref, slab, ga_ref, _sigmoid),
                conv_qkv(2, va_ref), plain(wgb_ref, 0, gb_ref, _sigmoid),
                qk_norm(wqb_ref, qnw_ref, qb_ref, DA_QK_DIM ** -0.5 * LOG2_E),
                plain(wgb_ref, slab, gb_ref, _sigmoid),
                qk_norm(wkb_ref, knw_ref, kb_ref, 1.0), plain(wvb_ref, 0, vb_ref, ident),
                plain(wza_ref, 0, za_ref, _silu), plain(wzb_ref, 0, zb_ref, _silu),
                decay_beta()]

    def run_tasks(task_list, extra=None):
        for n, (matmul, chunks) in enumerate(task_list):
            matmul()
            for fn in chunks:
                fn()
            if extra is not None and n == 8:
                extra()

    h_second = []
    run_tasks(tasks(normed(0), 0), extra=lambda: h_second.append(normed(half)))
    run_tasks(tasks(h_second[0], half))
    conv_buf[0:SUBLANES, :] = conv_buf[tm:tm + SUBLANES, :]


def _in_projection(x2, norm_w, w_in, conv_w, a_log, dt_bias, q_norm_w, k_norm_w, seq_len):
    m = x2.shape[0]
    tm = TM_IN
    assert m % tm == 0 and seq_len % tm == 0
    o = 0
    cols = {}
    for name, width in (("qkv", 3 * DN_WIDTH), ("za", DN_WIDTH), ("a", DN_HEADS), ("b", DN_HEADS),
                        ("qb", DA_QK_WIDTH), ("kb", DA_QK_WIDTH), ("vb", DA_WIDTH), ("zb", DA_WIDTH),
                        ("ga", D_MODEL), ("gb", D_MODEL)):
        cols[name] = w_in[:, o:o + width]
        o += width
    assert o == w_in.shape[1]
    wb = {k: v.astype(BF16) for k, v in cols.items()}
    w_ab = jnp.concatenate([cols["a"], cols["b"]], axis=1)
    w_ab_l = jnp.pad(w_ab, ((0, 0), (0, LANES - 2 * DN_HEADS))).astype(BF16)
    w_ab_s = jnp.pad(w_ab.T, ((0, AB_ROWS - 2 * DN_HEADS), (0, 0))).astype(BF16)
    alog_l = jnp.pad(a_log.astype(F32), (0, LANES - DN_HEADS)).reshape(1, LANES)
    dtb_l = jnp.pad(dt_bias.astype(F32), (0, LANES - DN_HEADS)).reshape(1, LANES)
    alog_s = jnp.pad(a_log.astype(F32), (0, AB_ROWS - DN_HEADS)).reshape(AB_ROWS, 1)
    dtb_s = jnp.pad(dt_bias.astype(F32), (0, AB_ROWS - DN_HEADS)).reshape(AB_ROWS, 1)
    qnw = jnp.tile(q_norm_w.astype(F32), 2).reshape(1, LANES)
    knw = jnp.tile(k_norm_w.astype(F32), 2).reshape(1, LANES)

    row_blk = lambda w: pl.BlockSpec((tm, w), lambda i: (i, 0))
    in_specs = [
        row_blk(D_MODEL), _resident((1, D_MODEL)),
        _resident((D_MODEL, 3 * DN_WIDTH)), _resident((CONV_WIDTH, 3 * DN_WIDTH)),
        _resident((D_MODEL, DN_WIDTH)), _resident((D_MODEL, LANES)), _resident((AB_ROWS, D_MODEL)),
        _resident((1, LANES)), _resident((1, LANES)), _resident((AB_ROWS, 1)), _resident((AB_ROWS, 1)),
        _resident((1, LANES)), _resident((1, LANES)),
        _resident((D_MODEL, DA_QK_WIDTH)), _resident((D_MODEL, DA_QK_WIDTH)),
        _resident((D_MODEL, DA_WIDTH)), _resident((D_MODEL, DA_WIDTH)),
        _resident((D_MODEL, D_MODEL)), _resident((D_MODEL, D_MODEL)),
    ]
    out_shapes = [jax.ShapeDtypeStruct((m, DN_WIDTH), BF16)] * 4 + [
        jax.ShapeDtypeStruct((m, LANES), F32), jax.ShapeDtypeStruct((AB_ROWS, m), F32)] + [
        jax.ShapeDtypeStruct((m, DA_WIDTH), BF16)] * 4 + [
        jax.ShapeDtypeStruct((m, D_MODEL), BF16)] * 2
    out_specs = [row_blk(DN_WIDTH)] * 4 + [
        row_blk(LANES), pl.BlockSpec((AB_ROWS, tm), lambda i: (0, i))] + [
        row_blk(DA_WIDTH)] * 4 + [row_blk(D_MODEL)] * 2
    return pl.pallas_call(
        functools.partial(_inproj_kernel, tiles_per_seq=seq_len // tm),
        grid=(m // tm,),
        in_specs=in_specs,
        out_specs=out_specs,
        out_shape=out_shapes,
        scratch_shapes=[pltpu.VMEM((tm + SUBLANES, 3 * DN_WIDTH), F32)],
        compiler_params=pltpu.CompilerParams(
            dimension_semantics=("arbitrary",), vmem_limit_bytes=VMEM_LIMIT),
        name="in_projection",
    )(x2, norm_w.astype(F32).reshape(1, D_MODEL), wb["qkv"], conv_w.astype(F32), wb["za"],
      w_ab_l, w_ab_s, alog_l, dtb_l, alog_s, dtb_s, qnw, knw,
      wb["qb"], wb["kb"], wb["vb"], wb["zb"], wb["ga"], wb["gb"])


def _bmm(a, b):
    return jnp.einsum('nij,njk->nik', a, b, preferred_element_type=F32)


def _bmm_nt(a, b):
    return jnp.einsum('nid,njd->nij', a, b, preferred_element_type=F32)


def _split_bf16(x, parts):
    out = []
    for _ in range(parts - 1):
        hi = x.astype(BF16)
        out.append(hi)
        x = x - hi.astype(F32)
    out.append(x.astype(BF16))
    return out


def _deltanet_kernel(q_ref, k_ref, v_ref, z_ref, gcol_ref, grow_ref, nw_ref, o_ref,
                     state_ref, oraw_ref, lhs_ref, attn_ref, u_ref, kdec_ref, sdec_ref, *, nt):
    tb = q_ref.shape[0]
    c = CHUNK
    nc = tb // c
    nh = DN_HEADS
    t = pl.program_id(1)

    ri = lax.broadcasted_iota(jnp.int32, (c, c), 0)
    ci = lax.broadcasted_iota(jnp.int32, (c, c), 1)
    lower = ri >= ci
    strict = ri > ci
    eye = (ri == ci).astype(F32)
    tri = lower.astype(BF16)
    tri_t = (ri <= ci).astype(BF16)

    def heads_major(ref):
        return jnp.concatenate(
            [ref[:, hd * DN_HEAD_DIM:(hd + 1) * DN_HEAD_DIM].reshape(nc, c, DN_HEAD_DIM)
             for hd in range(nh)], axis=0)

    def phase1_stages():
        e = {}

        def decays():
            gb_col = gcol_ref[...].reshape(nc, c, LANES)
            tri_b = jnp.broadcast_to(tri, (nc, c, c))
            gc_col = sum(_bmm(tri_b, part) for part in _split_bf16(gb_col, 3))
            gb_row = grow_ref[...]
            gc_row = sum(jnp.dot(part, tri_t, preferred_element_type=F32)
                         for part in _split_bf16(gb_row.reshape(nc * AB_ROWS, c), 3)
                         ).reshape(nc, AB_ROWS, c)

            def col_form(src, lane):
                return jnp.concatenate(
                    [jnp.broadcast_to(src[:, :, lane + hd:lane + hd + 1], (nc, c, LANES))
                     for hd in range(nh)], axis=0)

            def row_form(src, row):
                return jnp.concatenate(
                    [src[:, row + hd:row + hd + 1, :] for hd in range(nh)], axis=0)

            e["g_c"] = col_form(gc_col, 0)
            e["beta_c"] = col_form(gb_col, nh)
            e["g_r"] = row_form(gc_row, 0)
            e["beta_r"] = row_form(gb_row, nh)
            e["g_last"] = e["g_c"][:, c - 1:c, :]
            diff = e["g_c"][:, :, :c] - e["g_r"]
            e["decay"] = jnp.where(lower, jnp.exp(jnp.where(lower, diff, 0.0)), 0.0)

        def gram_kk():
            e["q"], e["k"], e["v"] = heads_major(q_ref), heads_major(k_ref), heads_major(v_ref)
            l_strict = jnp.where(
                strict, _bmm_nt(e["k"], e["k"]) * e["decay"] * e["beta_c"][:, :, :c], 0.0)
            e["inv"] = eye - l_strict
            e["power"] = l_strict.astype(BF16)

        def gram_qk():
            e["attn"] = jnp.where(lower, _bmm_nt(e["q"], e["k"]) * e["decay"], 0.0).astype(BF16)

        def next_power():
            e["power"] = _bmm(e["power"], e["power"]).astype(BF16)

        def apply_power():
            e["inv"] = e["inv"] + _bmm(e["inv"].astype(BF16), e["power"])

        def solve_u():
            e["t_beta"] = e["inv"] * e["beta_r"]
            e["u"] = _bmm(e["t_beta"].astype(BF16), e["v"])

        def solve_w():
            e["w"] = _bmm((e["t_beta"] * jnp.exp(e["g_r"])).astype(BF16), e["k"]).astype(BF16)

        def commit():
            q_decay = (e["q"].astype(F32) * jnp.exp(e["g_c"])).astype(BF16)
            lhs_ref[...] = jnp.concatenate([e["w"], q_decay], axis=1)
            attn_ref[...] = e["attn"]
            u_ref[...] = e["u"]
            kdec_ref[...] = (e["k"].astype(F32) * jnp.exp(e["g_last"] - e["g_c"])).astype(BF16)
            sdec_ref[...] = jnp.exp(e["g_last"])

        levels = [next_power, apply_power] * (c.bit_length() - 2)
        return [decays, gram_kk, gram_qk] + levels + [solve_u, solve_w], commit

    def phase2_stages():
        def chunk_steps(n):
            idx = [hd * nc + n for hd in range(nh)]
            e = {}

            def through_state():
                e["s_old"] = [state_ref[hd] for hd in range(nh)]
                e["ws_qs"] = [jnp.dot(lhs_ref[b], e["s_old"][hd].astype(BF16),
                                      preferred_element_type=F32) for hd, b in enumerate(idx)]

            def update():
                v_new = [(u_ref[b] - e["ws_qs"][hd][:c]).astype(BF16) for hd, b in enumerate(idx)]
                for hd, b in enumerate(idx):
                    oraw_ref[n * c:(n + 1) * c, hd * DN_HEAD_DIM:(hd + 1) * DN_HEAD_DIM] = (
                        e["ws_qs"][hd][c:]
                        + jnp.dot(attn_ref[b], v_new[hd], preferred_element_type=F32))
                for hd, b in enumerate(idx):
                    state_ref[hd] = e["s_old"][hd] * sdec_ref[b] + lax.dot_general(
                        kdec_ref[b], v_new[hd], TN_DIMS, preferred_element_type=F32)
            return [through_state, update]

        def gated_norm():
            for hd in range(nh):
                hs = slice(hd * DN_HEAD_DIM, (hd + 1) * DN_HEAD_DIM)
                o = oraw_ref[:, hs]
                ms = jnp.mean(o * o, axis=-1, keepdims=True)
                o = o * lax.rsqrt(ms + EPS) * nw_ref[...]
                o_ref[:, hs] = (o * z_ref[:, hs].astype(F32)).astype(BF16)

        return [fn for n in range(nc) for fn in chunk_steps(n)] + [gated_norm]

    @pl.when(t == 0)
    def _():
        state_ref[...] = jnp.zeros(state_ref.shape, F32)
        stages, commit = phase1_stages()
        for fn in stages:
            fn()
        commit()

    @pl.when((t > 0) & (t < nt))
    def _():
        stages, commit = phase1_stages()
        others = phase2_stages()
        done = 0
        for n, fn in enumerate(stages):
            fn()
            upto = -(-(n + 1) * len(others) // len(stages))
            for other in others[done:upto]:
                other()
            done = upto
        commit()

    @pl.when(t == nt)
    def _():
        for fn in phase2_stages():
            fn()


def _deltanet(qa, ka, va, za, gcol, grow, dn_norm_w, batch, seq_len):
    tb = TB_DN
    assert seq_len % tb == 0 and tb % CHUNK == 0
    nt = seq_len // tb
    nc = tb // CHUNK
    nb = DN_HEADS * nc
    grow3 = grow.reshape(AB_ROWS, -1, CHUNK).transpose(1, 0, 2)
    cur = lambda w: pl.BlockSpec((tb, w), lambda b, t: (b * nt + jnp.minimum(t, nt - 1), 0))
    prev = lambda w: pl.BlockSpec((tb, w), lambda b, t: (b * nt + jnp.maximum(t - 1, 0), 0))
    return pl.pallas_call(
        functools.partial(_deltanet_kernel, nt=nt),
        grid=(batch, nt + 1),
        in_specs=[cur(DN_WIDTH)] * 3 + [
            prev(DN_WIDTH), cur(LANES),
            pl.BlockSpec((nc, AB_ROWS, CHUNK),
                         lambda b, t: (b * nt + jnp.minimum(t, nt - 1), 0, 0)),
            _resident((1, DN_HEAD_DIM))],
        out_specs=prev(DN_WIDTH),
        out_shape=jax.ShapeDtypeStruct((batch * seq_len, DN_WIDTH), BF16),
        scratch_shapes=[pltpu.VMEM((DN_HEADS, DN_HEAD_DIM, DN_HEAD_DIM), F32),
                        pltpu.VMEM((tb, DN_WIDTH), F32),
                        pltpu.VMEM((nb, 2 * CHUNK, DN_HEAD_DIM), BF16),
                        pltpu.VMEM((nb, CHUNK, CHUNK), BF16),
                        pltpu.VMEM((nb, CHUNK, DN_HEAD_DIM), F32),
                        pltpu.VMEM((nb, CHUNK, DN_HEAD_DIM), BF16),
                        pltpu.VMEM((nb, 1, DN_HEAD_DIM), F32)],
        compiler_params=pltpu.CompilerParams(
            dimension_semantics=("arbitrary", "arbitrary"), vmem_limit_bytes=VMEM_LIMIT),
        name="gated_deltanet",
    )(qa, ka, va, za, gcol, grow3, dn_norm_w.astype(F32).reshape(1, DN_HEAD_DIM))


def _diffattn_kernel(lq1_ref, lk1_ref, lq2_ref, lk2_ref, q_ref, k_ref, v_ref, z_ref, nw_ref,
                     o_ref, m_ref, l_ref, acc_ref):
    tq = q_ref.shape[0]
    tk = TK_DA
    qi = pl.program_id(2)

    lane = lax.broadcasted_iota(jnp.int32, (1, LANES), 1)
    q = q_ref[...]
    zero = jnp.zeros_like(q)
    q_comp = (jnp.where(lane < DA_QK_DIM, q, zero), jnp.where(lane < DA_QK_DIM, zero, q))

    m_ref[...] = jnp.full(m_ref.shape, NEG_INF, F32)
    l_ref[...] = jnp.zeros(l_ref.shape, F32)
    acc_ref[...] = jnp.zeros(acc_ref.shape, F32)

    def attend(j, masked):
        start = pl.multiple_of(j * tk, tk)
        kblk = k_ref[pl.ds(start, tk), :]
        vblk = v_ref[pl.ds(start, tk), :]
        scores = [lax.dot_general(q_comp[comp], kblk, NT_DIMS, preferred_element_type=F32)
                  for comp in range(2)]
        if masked:
            rq = lax.broadcasted_iota(jnp.int32, (tq, tk), 0)
            ck = lax.broadcasted_iota(jnp.int32, (tq, tk), 1)
            scores = [jnp.where(ck <= rq, s, NEG_INF) for s in scores]
        probs = []
        for comp in range(2):
            s = scores[comp]
            m_prev = m_ref[comp]
            m_new = jnp.maximum(m_prev, jnp.max(s, axis=-1, keepdims=True))
            alpha = jnp.exp2(m_prev - m_new)
            p = jnp.exp2((s - jnp.concatenate([m_new] * (tk // LANES), axis=1)).astype(BF16))
            l_ref[comp] = alpha * l_ref[comp] + jnp.sum(p.astype(F32), axis=-1, keepdims=True)
            acc_ref[comp] = alpha * acc_ref[comp]
            m_ref[comp] = m_new
            probs.append(p)
        for comp in range(2):
            acc_ref[comp] += jnp.dot(probs[comp], vblk, preferred_element_type=F32)

    def body(j, carry):
        attend(j, False)
        return carry

    lax.fori_loop(0, qi, body, 0)
    attend(qi, True)


    lam = (jnp.exp(jnp.sum(lq1_ref[...] * lk1_ref[...], axis=-1, keepdims=True))
           - jnp.exp(jnp.sum(lq2_ref[...] * lk2_ref[...], axis=-1, keepdims=True))
           + LAMBDA_INIT)
    o = acc_ref[0] / l_ref[0] - lam * (acc_ref[1] / l_ref[1])
    ms = jnp.mean(o * o, axis=-1, keepdims=True)
    o = o * lax.rsqrt(ms + EPS) * nw_ref[...] * (1.0 - LAMBDA_INIT)
    o_ref[...] = (o * z_ref[...].astype(F32)).astype(BF16)


def _diff_attention(qb, kb, vb, zb, lambda_q1, lambda_k1, lambda_q2, lambda_k2, da_norm_w,
                    batch, seq_len):
    tq = TQ_DA
    assert tq == TK_DA and seq_len % tq == 0
    nq = seq_len // tq
    lam_vec = lambda v: v.astype(F32).reshape(1, DA_QK_DIM)
    q_blk = pl.BlockSpec((tq, LANES), lambda b, h, i: (b * nq + i, h))
    kv_blk = pl.BlockSpec((seq_len, LANES), lambda b, h, i: (b, h))
    return pl.pallas_call(
        _diffattn_kernel,
        grid=(batch, DA_HEADS, nq),
        in_specs=[_resident((1, DA_QK_DIM))] * 4 + [q_blk, kv_blk, kv_blk, q_blk,
                                                    _resident((1, DA_V_DIM))],
        out_specs=q_blk,
        out_shape=jax.ShapeDtypeStruct((batch * seq_len, DA_WIDTH), BF16),
        scratch_shapes=[pltpu.VMEM((2, tq, LANES), F32), pltpu.VMEM((2, tq, LANES), F32),
                        pltpu.VMEM((2, tq, DA_V_DIM), F32)],
        compiler_params=pltpu.CompilerParams(
            dimension_semantics=("arbitrary", "arbitrary", "arbitrary"),
            vmem_limit_bytes=VMEM_LIMIT),
        name="diff_attention",
    )(lam_vec(lambda_q1), lam_vec(lambda_k1), lam_vec(lambda_q2), lam_vec(lambda_k2),
      qb, kb, vb, zb, da_norm_w.astype(F32).reshape(1, DA_V_DIM))


def _output_kernel(oa_ref, ob_ref, ga_ref, gb_ref, x_ref, woa_ref, wob_ref, wo_ref, out_ref):
    y_a = jnp.dot(oa_ref[...], woa_ref[...], preferred_element_type=F32)
    y_b = jnp.dot(ob_ref[...], wob_ref[...], preferred_element_type=F32)
    y = ga_ref[...].astype(F32) * y_a + gb_ref[...].astype(F32) * y_b
    out_ref[...] = x_ref[...] + jnp.dot(y.astype(BF16), wo_ref[...], preferred_element_type=F32)


def _output_projection(oa, ob, ga, gb, x2, w_out_a, w_out_b, w_out):
    m = x2.shape[0]
    tm = TM_OUT
    assert m % tm == 0
    row_blk = lambda w: pl.BlockSpec((tm, w), lambda i: (i, 0))
    return pl.pallas_call(
        _output_kernel,
        grid=(m // tm,),
        in_specs=[row_blk(DN_WIDTH), row_blk(DA_WIDTH), row_blk(D_MODEL), row_blk(D_MODEL),
                  row_blk(D_MODEL), _resident((DN_WIDTH, D_MODEL)), _resident((DA_WIDTH, D_MODEL)),
                  _resident((D_MODEL, D_MODEL))],
        out_specs=row_blk(D_MODEL),
        out_shape=jax.ShapeDtypeStruct((m, D_MODEL), F32),
        compiler_params=pltpu.CompilerParams(
            dimension_semantics=("arbitrary",), vmem_limit_bytes=VMEM_LIMIT),
        name="output_projection",
    )(oa, ob, ga, gb, x2, w_out_a.astype(BF16), w_out_b.astype(BF16), w_out.astype(BF16))


def kernel(x, norm_w, w_in, conv_w, a_log, dt_bias, dn_norm_w, q_norm_w, k_norm_w,
           lambda_q1, lambda_k1, lambda_q2, lambda_k2, da_norm_w, w_out_a, w_out_b, w_out):
    batch, seq_len, d_model = x.shape
    assert d_model == D_MODEL
    x2 = x.reshape(batch * seq_len, d_model)
    (qa, ka, va, za, gcol, grow, qb, kb, vb, zb, ga, gb) = _in_projection(
        x2, norm_w, w_in, conv_w, a_log, dt_bias, q_norm_w, k_norm_w, seq_len)
    oa = _deltanet(qa, ka, va, za, gcol, grow, dn_norm_w, batch, seq_len)
    ob = _diff_attention(qb, kb, vb, zb, lambda_q1, lambda_k1, lambda_q2, lambda_k2, da_norm_w,
                         batch, seq_len)
    out = _output_projection(oa, ob, ga, gb, x2, w_out_a, w_out_b, w_out)
    return out.reshape(batch, seq_len, d_model)
```

```python
import functools
import math

import jax
import jax.numpy as jnp
from jax import lax
from jax.experimental import pallas as pl
from jax.experimental.pallas import tpu as pltpu

F32 = jnp.float32
BF16 = jnp.bfloat16

D_MODEL = 1024
DN_HEADS = 4
DN_HEAD_DIM = 128
DN_WIDTH = DN_HEADS * DN_HEAD_DIM
CONV_WIDTH = 4
CHUNK = 64
DA_HEADS = 4
DA_QK_DIM = 64
DA_V_DIM = 2 * DA_QK_DIM
DA_WIDTH = DA_HEADS * DA_V_DIM
DA_QK_WIDTH = DA_HEADS * 2 * DA_QK_DIM
LAMBDA_INIT = 0.8 - 0.6 * math.exp(-0.3 * 0)
EPS = 1e-6
NEG_INF = -1e30
LOG2_E = math.log2(math.e)

LANES = 128
MXU_COLS = 256
SUBLANES = 8
BF16_ROWS = 16
AB_ROWS = BF16_ROWS
VMEM_LIMIT = 48 * 1024 * 1024

TM_IN = 512
TB_DN = 512
TQ_DA = 512
TK_DA = 512
TM_OUT = 1024

W_MAIN_COLS = {}
_c = 0
for _name, _width in (("qkv", 3 * DN_WIDTH), ("za", DN_WIDTH), ("qb", DA_QK_WIDTH),
                      ("kb", DA_QK_WIDTH), ("vb", DA_WIDTH), ("zb", DA_WIDTH),
                      ("ga", D_MODEL), ("gb", D_MODEL)):
    W_MAIN_COLS[_name] = _c
    _c += _width
W_MAIN_WIDTH = _c

NT_DIMS = (((1,), (1,)), ((), ()))
TN_DIMS = (((0,), (0,)), ((), ()))


def _sigmoid(v):
    return 0.5 + 0.5 * jnp.tanh(0.5 * v)


def _silu(v):
    h = 0.5 * v
    return h + h * jnp.tanh(h)


def _softplus(v):
    return jnp.maximum(v, 0.0) + jnp.log1p(jnp.exp(-jnp.abs(v)))


def _resident(shape):
    zeros = (0,) * len(shape)
    return pl.BlockSpec(shape, lambda *_: zeros, pipeline_mode=pl.Buffered(1))


def _inproj_kernel(x_ref, nw_ref, w_ref, convw_ref, wab_ref, wabt_ref,
                   alog_l_ref, dtb_l_ref, alog_s_ref, dtb_s_ref, qnw_ref, knw_ref,
                   qa_ref, ka_ref, va_ref, za_ref, gcol_ref, grow_ref,
                   qb_ref, kb_ref, vb_ref, zb_ref, ga_ref, gb_ref,
                   conv_buf, *, tiles_per_seq):
    tm = x_ref.shape[0]
    half = tm // 2
    i = pl.program_id(0)
    head0 = SUBLANES - (CONV_WIDTH - 1)
    slab = 512

    @pl.when(i % tiles_per_seq == 0)
    def _():
        conv_buf[0:SUBLANES, :] = jnp.zeros((SUBLANES, 3 * DN_WIDTH), F32)

    def normed(r0):
        x = x_ref[r0:r0 + half, :]
        ms = jnp.mean(x * x, axis=-1, keepdims=True)
        return (x * lax.rsqrt(ms + EPS) * nw_ref[...]).astype(BF16)

    def tasks(h, r0):
        rows = slice(r0, r0 + half)

        def conv_qkv(s, out_ref):
            cs = slice(s * DN_WIDTH, (s + 1) * DN_WIDTH)

            res = []

            def matmul():
                res.append(jnp.dot(h, w_ref[:, cs], preferred_element_type=F32))
                conv_buf[SUBLANES + r0:SUBLANES + r0 + half, cs] = res[0]

            def epilogue():
                acc = convw_ref[CONV_WIDTH - 1:CONV_WIDTH, cs] * res[0]
                for j in range(CONV_WIDTH - 1):
                    acc = acc + (convw_ref[j:j + 1, cs]
                                 * conv_buf[head0 + j + r0:head0 + j + r0 + half, cs])
                y = _silu(acc)
                if s == 2:
                    out_ref[rows, :] = y.astype(BF16)
                    return
                scale = DN_HEAD_DIM ** -0.5 if s == 0 else 1.0
                for hd in range(DN_HEADS):
                    hs = slice(hd * DN_HEAD_DIM, (hd + 1) * DN_HEAD_DIM)
                    blk = y[:, hs]
                    ss = jnp.sum(blk * blk, axis=-1, keepdims=True)
                    out_ref[rows, hs] = (blk * (lax.rsqrt(ss + EPS) * scale)).astype(BF16)
            return matmul, [epilogue]

        def plain(w0, c0, out_ref, act):
            res = []

            def matmul():
                res.append(jnp.dot(h, w_ref[:, w0 + c0:w0 + c0 + slab],
                                   preferred_element_type=F32))

            def chunk(n):
                def run():
                    cs = slice(n * LANES, (n + 1) * LANES)
                    out_ref[rows, c0 + n * LANES:c0 + (n + 1) * LANES] = (
                        act(res[0][:, cs]).astype(BF16))
                return run
            return matmul, [chunk(n) for n in range(slab // LANES)]

        def qk_norm(w0, nrm_ref, out_ref, scale):
            res = []

            def matmul():
                res.append(jnp.dot(h, w_ref[:, w0:w0 + DA_QK_WIDTH], preferred_element_type=F32))

            def chunk(hd):
                def run():
                    lo = lax.broadcasted_iota(jnp.int32, (half, LANES), 1) < DA_QK_DIM
                    blk = res[0][:, hd * LANES:(hd + 1) * LANES]
                    sq = blk * blk
                    s_lo = jnp.sum(jnp.where(lo, sq, 0.0), axis=-1, keepdims=True)
                    s_hi = jnp.sum(jnp.where(lo, 0.0, sq), axis=-1, keepdims=True)
                    msq = jnp.where(lo, s_lo, s_hi) * (1.0 / DA_QK_DIM)
                    out_ref[rows, hd * LANES:(hd + 1) * LANES] = (
                        blk * lax.rsqrt(msq + EPS) * nrm_ref[...] * scale).astype(BF16)
                return run
            return matmul, [chunk(hd) for hd in range(DA_HEADS)]

        def decay_beta():
            def matmul():
                ab = jnp.dot(h, wab_ref[...], preferred_element_type=F32)
                lane = lax.broadcasted_iota(jnp.int32, ab.shape, 1)
                g_l = -jnp.exp(alog_l_ref[...]) * _softplus(ab + dtb_l_ref[...])
                gcol_ref[rows, :] = jnp.where(lane < DN_HEADS, g_l, _sigmoid(ab))
                abt = lax.dot_general(wabt_ref[...], h, NT_DIMS,
                                      preferred_element_type=F32)
                row = lax.broadcasted_iota(jnp.int32, abt.shape, 0)
                g_s = -jnp.exp(alog_s_ref[...]) * _softplus(abt + dtb_s_ref[...])
                grow_ref[:, rows] = jnp.where(row < DN_HEADS, g_s, _sigmoid(abt))
            return matmul, []

        ident = lambda v: v
        col = W_MAIN_COLS
        return [conv_qkv(0, qa_ref), plain(col["ga"], 0, ga_ref, _sigmoid),
                conv_qkv(1, ka_ref), plain(col["ga"], slab, ga_ref, _sigmoid),
                conv_qkv(2, va_ref), plain(col["gb"], 0, gb_ref, _sigmoid),
                qk_norm(col["qb"], qnw_ref, qb_ref, DA_QK_DIM ** -0.5 * LOG2_E),
                plain(col["gb"], slab, gb_ref, _sigmoid),
                qk_norm(col["kb"], knw_ref, kb_ref, 1.0), plain(col["vb"], 0, vb_ref, ident),
                plain(col["za"], 0, za_ref, _silu), plain(col["zb"], 0, zb_ref, _silu),
                decay_beta()]

    def run_tasks(task_list, extra=None):
        for n, (matmul, chunks) in enumerate(task_list):
            matmul()
            for fn in chunks:
                fn()
            if extra is not None and n == 8:
                extra()

    h_second = []
    run_tasks(tasks(normed(0), 0), extra=lambda: h_second.append(normed(half)))
    run_tasks(tasks(h_second[0], half))
    conv_buf[0:SUBLANES, :] = conv_buf[tm:tm + SUBLANES, :]


def _in_projection(x2, norm_w, w_in, conv_w, a_log, dt_bias, q_norm_w, k_norm_w, seq_len):
    m = x2.shape[0]
    tm = TM_IN
    assert m % tm == 0 and seq_len % tm == 0
    n_ab = 2 * DN_HEADS
    assert w_in.shape[1] == W_MAIN_WIDTH + n_ab
    ab0 = W_MAIN_COLS["qb"]
    w_main = jnp.concatenate([w_in[:, :ab0], w_in[:, ab0 + n_ab:]], axis=1).astype(BF16)
    w_ab = w_in[:, ab0:ab0 + n_ab]
    w_ab_l = jnp.pad(w_ab, ((0, 0), (0, LANES - n_ab))).astype(BF16)
    w_ab_s = jnp.pad(w_ab.T, ((0, AB_ROWS - n_ab), (0, 0))).astype(BF16)
    alog_l = jnp.pad(a_log.astype(F32), (0, LANES - DN_HEADS)).reshape(1, LANES)
    dtb_l = jnp.pad(dt_bias.astype(F32), (0, LANES - DN_HEADS)).reshape(1, LANES)
    alog_s = jnp.pad(a_log.astype(F32), (0, AB_ROWS - DN_HEADS)).reshape(AB_ROWS, 1)
    dtb_s = jnp.pad(dt_bias.astype(F32), (0, AB_ROWS - DN_HEADS)).reshape(AB_ROWS, 1)
    qnw = jnp.tile(q_norm_w.astype(F32), 2).reshape(1, LANES)
    knw = jnp.tile(k_norm_w.astype(F32), 2).reshape(1, LANES)

    row_blk = lambda w: pl.BlockSpec((tm, w), lambda i: (i, 0))
    in_specs = [
        row_blk(D_MODEL), _resident((1, D_MODEL)),
        _resident((D_MODEL, W_MAIN_WIDTH)), _resident((CONV_WIDTH, 3 * DN_WIDTH)),
        _resident((D_MODEL, LANES)), _resident((AB_ROWS, D_MODEL)),
        _resident((1, LANES)), _resident((1, LANES)), _resident((AB_ROWS, 1)), _resident((AB_ROWS, 1)),
        _resident((1, LANES)), _resident((1, LANES)),
    ]
    out_shapes = [jax.ShapeDtypeStruct((m, DN_WIDTH), BF16)] * 4 + [
        jax.ShapeDtypeStruct((m, LANES), F32), jax.ShapeDtypeStruct((AB_ROWS, m), F32)] + [
        jax.ShapeDtypeStruct((m, DA_WIDTH), BF16)] * 4 + [
        jax.ShapeDtypeStruct((m, D_MODEL), BF16)] * 2
    out_specs = [row_blk(DN_WIDTH)] * 4 + [
        row_blk(LANES), pl.BlockSpec((AB_ROWS, tm), lambda i: (0, i))] + [
        row_blk(DA_WIDTH)] * 4 + [row_blk(D_MODEL)] * 2
    return pl.pallas_call(
        functools.partial(_inproj_kernel, tiles_per_seq=seq_len // tm),
        grid=(m // tm,),
        in_specs=in_specs,
        out_specs=out_specs,
        out_shape=out_shapes,
        scratch_shapes=[pltpu.VMEM((tm + SUBLANES, 3 * DN_WIDTH), F32)],
        compiler_params=pltpu.CompilerParams(
            dimension_semantics=("arbitrary",), vmem_limit_bytes=VMEM_LIMIT),
        name="in_projection",
    )(x2, norm_w.astype(F32).reshape(1, D_MODEL), w_main, conv_w.astype(F32),
      w_ab_l, w_ab_s, alog_l, dtb_l, alog_s, dtb_s, qnw, knw)


def _bmm(a, b):
    return jnp.einsum('nij,njk->nik', a, b, preferred_element_type=F32)


def _bmm_nt(a, b):
    return jnp.einsum('nid,njd->nij', a, b, preferred_element_type=F32)


def _split_bf16(x, parts):
    out = []
    for _ in range(parts - 1):
        hi = x.astype(BF16)
        out.append(hi)
        x = x - hi.astype(F32)
    out.append(x.astype(BF16))
    return out


def _deltanet_kernel(q_ref, k_ref, v_ref, z_ref, gcol_ref, grow_ref, nw_ref, o_ref,
                     state_ref, oraw_ref, lhs_ref, attn_ref, u_ref, kdec_ref, sdec_ref, *, nt):
    tb = q_ref.shape[0]
    c = CHUNK
    nc = tb // c
    nh = DN_HEADS
    t = pl.program_id(1)

    ri = lax.broadcasted_iota(jnp.int32, (c, c), 0)
    ci = lax.broadcasted_iota(jnp.int32, (c, c), 1)
    lower = ri >= ci
    strict = ri > ci
    eye = (ri == ci).astype(F32)
    tri = lower.astype(BF16)
    tri_t = (ri <= ci).astype(BF16)

    def heads_major(ref):
        return jnp.concatenate(
            [ref[:, hd * DN_HEAD_DIM:(hd + 1) * DN_HEAD_DIM].reshape(nc, c, DN_HEAD_DIM)
             for hd in range(nh)], axis=0)

    def phase1_stages():
        e = {}

        def decays():
            gb_col = gcol_ref[...].reshape(nc, c, LANES)
            tri_b = jnp.broadcast_to(tri, (nc, c, c))
            gc_col = sum(_bmm(tri_b, part) for part in _split_bf16(gb_col, 3))
            gb_row = grow_ref[...]
            gc_row = sum(jnp.dot(part, tri_t, preferred_element_type=F32)
                         for part in _split_bf16(gb_row.reshape(nc * AB_ROWS, c), 3)
                         ).reshape(nc, AB_ROWS, c)

            def col_form(src, lane):
                return jnp.concatenate(
                    [jnp.broadcast_to(src[:, :, lane + hd:lane + hd + 1], (nc, c, LANES))
                     for hd in range(nh)], axis=0)

            def row_form(src, row):
                return jnp.concatenate(
                    [src[:, row + hd:row + hd + 1, :] for hd in range(nh)], axis=0)

            e["g_c"] = col_form(gc_col, 0)
            e["beta_c"] = col_form(gb_col, nh)
            e["g_r"] = row_form(gc_row, 0)
            e["beta_r"] = row_form(gb_row, nh)
            e["g_last"] = e["g_c"][:, c - 1:c, :]
            diff = e["g_c"][:, :, :c] - e["g_r"]
            e["decay"] = jnp.where(lower, jnp.exp(jnp.where(lower, diff, 0.0)), 0.0)

        def gram_kk():
            e["q"], e["k"], e["v"] = heads_major(q_ref), heads_major(k_ref), heads_major(v_ref)
            l_strict = jnp.where(
                strict, _bmm_nt(e["k"], e["k"]) * e["decay"] * e["beta_c"][:, :, :c], 0.0)
            e["inv"] = eye - l_strict
            e["power"] = l_strict.astype(BF16)

        def gram_qk():
            e["attn"] = jnp.where(lower, _bmm_nt(e["q"], e["k"]) * e["decay"], 0.0).astype(BF16)

        def next_power():
            e["power"] = _bmm(e["power"], e["power"]).astype(BF16)

        def apply_power():
            e["inv"] = e["inv"] + _bmm(e["inv"].astype(BF16), e["power"])

        def solve_u():
            e["t_beta"] = e["inv"] * e["beta_r"]
            e["u"] = _bmm(e["t_beta"].astype(BF16), e["v"])

        def solve_w():
            e["w"] = _bmm((e["t_beta"] * jnp.exp(e["g_r"])).astype(BF16), e["k"]).astype(BF16)

        def commit():
            q_decay = (e["q"].astype(F32) * jnp.exp(e["g_c"])).astype(BF16)
            lhs_ref[...] = jnp.concatenate([e["w"], q_decay], axis=1)
            attn_ref[...] = e["attn"]
            u_ref[...] = e["u"]
            kdec_ref[...] = (e["k"].astype(F32) * jnp.exp(e["g_last"] - e["g_c"])).astype(BF16)
            sdec_ref[...] = jnp.exp(e["g_last"])

        levels = [next_power, apply_power] * (c.bit_length() - 2)
        return [decays, gram_kk, gram_qk] + levels + [solve_u, solve_w], commit

    def phase2_stages():
        def chunk_steps(n):
            idx = [hd * nc + n for hd in range(nh)]
            e = {}

            def through_state():
                e["s_old"] = [state_ref[hd] for hd in range(nh)]
                e["ws_qs"] = [jnp.dot(lhs_ref[b], e["s_old"][hd].astype(BF16),
                                      preferred_element_type=F32) for hd, b in enumerate(idx)]

            def update():
                v_new = [(u_ref[b] - e["ws_qs"][hd][:c]).astype(BF16) for hd, b in enumerate(idx)]
                for hd, b in enumerate(idx):
                    oraw_ref[n * c:(n + 1) * c, hd * DN_HEAD_DIM:(hd + 1) * DN_HEAD_DIM] = (
                        e["ws_qs"][hd][c:]
                        + jnp.dot(attn_ref[b], v_new[hd], preferred_element_type=F32))
                for hd, b in enumerate(idx):
                    state_ref[hd] = e["s_old"][hd] * sdec_ref[b] + lax.dot_general(
                        kdec_ref[b], v_new[hd], TN_DIMS, preferred_element_type=F32)
            return [through_state, update]

        def gated_norm():
            for hd in range(nh):
                hs = slice(hd * DN_HEAD_DIM, (hd + 1) * DN_HEAD_DIM)
                o = oraw_ref[:, hs]
                ms = jnp.mean(o * o, axis=-1, keepdims=True)
                o = o * lax.rsqrt(ms + EPS) * nw_ref[...]
                o_ref[:, hs] = (o * z_ref[:, hs].astype(F32)).astype(BF16)

        return [fn for n in range(nc) for fn in chunk_steps(n)] + [gated_norm]

    @pl.when(t == 0)
    def _():
        state_ref[...] = jnp.zeros(state_ref.shape, F32)
        stages, commit = phase1_stages()
        for fn in stages:
            fn()
        commit()

    @pl.when((t > 0) & (t < nt))
    def _():
        stages, commit = phase1_stages()
        others = phase2_stages()
        done = 0
        for n, fn in enumerate(stages):
            fn()
            upto = -(-(n + 1) * len(others) // len(stages))
            for other in others[done:upto]:
                other()
            done = upto
        commit()

    @pl.when(t == nt)
    def _():
        for fn in phase2_stages():
            fn()


def _deltanet(qa, ka, va, za, gcol, grow, dn_norm_w, batch, seq_len):
    tb = TB_DN
    assert seq_len % tb == 0 and tb % CHUNK == 0
    nt = seq_len // tb
    nc = tb // CHUNK
    nb = DN_HEADS * nc
    grow3 = grow.reshape(AB_ROWS, -1, CHUNK).transpose(1, 0, 2)
    cur = lambda w: pl.BlockSpec((tb, w), lambda b, t: (b * nt + jnp.minimum(t, nt - 1), 0))
    prev = lambda w: pl.BlockSpec((tb, w), lambda b, t: (b * nt + jnp.maximum(t - 1, 0), 0))
    return pl.pallas_call(
        functools.partial(_deltanet_kernel, nt=nt),
        grid=(batch, nt + 1),
        in_specs=[cur(DN_WIDTH)] * 3 + [
            prev(DN_WIDTH), cur(LANES),
            pl.BlockSpec((nc, AB_ROWS, CHUNK),
                         lambda b, t: (b * nt + jnp.minimum(t, nt - 1), 0, 0)),
            _resident((1, DN_HEAD_DIM))],
        out_specs=prev(DN_WIDTH),
        out_shape=jax.ShapeDtypeStruct((batch * seq_len, DN_WIDTH), BF16),
        scratch_shapes=[pltpu.VMEM((DN_HEADS, DN_HEAD_DIM, DN_HEAD_DIM), F32),
                        pltpu.VMEM((tb, DN_WIDTH), F32),
                        pltpu.VMEM((nb, 2 * CHUNK, DN_HEAD_DIM), BF16),
                        pltpu.VMEM((nb, CHUNK, CHUNK), BF16),
                        pltpu.VMEM((nb, CHUNK, DN_HEAD_DIM), F32),
                        pltpu.VMEM((nb, CHUNK, DN_HEAD_DIM), BF16),
                        pltpu.VMEM((nb, 1, DN_HEAD_DIM), F32)],
        compiler_params=pltpu.CompilerParams(
            dimension_semantics=("arbitrary", "arbitrary"), vmem_limit_bytes=VMEM_LIMIT),
        name="gated_deltanet",
    )(qa, ka, va, za, gcol, grow3, dn_norm_w.astype(F32).reshape(1, DN_HEAD_DIM))


def _diffattn_kernel(lq1_ref, lk1_ref, lq2_ref, lk2_ref, q_ref, k_ref, v_ref, z_ref, nw_ref,
                     o_ref, m_ref, l_ref, acc_ref):
    tq = q_ref.shape[0]
    tk = TK_DA
    qi = pl.program_id(2)

    lane = lax.broadcasted_iota(jnp.int32, (1, LANES), 1)
    q = q_ref[...]
    zero = jnp.zeros_like(q)
    q_comp = (jnp.where(lane < DA_QK_DIM, q, zero), jnp.where(lane < DA_QK_DIM, zero, q))

    m_ref[...] = jnp.full(m_ref.shape, NEG_INF, F32)
    l_ref[...] = jnp.zeros(l_ref.shape, F32)
    acc_ref[...] = jnp.zeros(acc_ref.shape, F32)

    def attend(j, masked):
        start = pl.multiple_of(j * tk, tk)
        kblk = k_ref[pl.ds(start, tk), :]
        vblk = v_ref[pl.ds(start, tk), :]
        scores = [lax.dot_general(q_comp[comp], kblk, NT_DIMS, preferred_element_type=F32)
                  for comp in range(2)]
        if masked:
            rq = lax.broadcasted_iota(jnp.int32, (tq, tk), 0)
            ck = lax.broadcasted_iota(jnp.int32, (tq, tk), 1)
            scores = [jnp.where(ck <= rq, s, NEG_INF) for s in scores]
        probs = []
        for comp in range(2):
            s = scores[comp]
            m_prev = m_ref[comp]
            m_new = jnp.maximum(m_prev, jnp.max(s, axis=-1, keepdims=True))
            alpha = jnp.exp2(m_prev - m_new)
            p = jnp.exp2((s - jnp.concatenate([m_new] * (tk // LANES), axis=1)).astype(BF16))
            l_ref[comp] = alpha * l_ref[comp] + jnp.sum(p.astype(F32), axis=-1, keepdims=True)
            acc_ref[comp] = alpha * acc_ref[comp]
            m_ref[comp] = m_new
            probs.append(p)
        for comp in range(2):
            acc_ref[comp] += jnp.dot(probs[comp], vblk, preferred_element_type=F32)

    def body(j, carry):
        attend(j, False)
        return carry

    lax.fori_loop(0, qi, body, 0)
    attend(qi, True)


    lam = (jnp.exp(jnp.sum(lq1_ref[...] * lk1_ref[...], axis=-1, keepdims=True))
           - jnp.exp(jnp.sum(lq2_ref[...] * lk2_ref[...], axis=-1, keepdims=True))
           + LAMBDA_INIT)
    o = acc_ref[0] / l_ref[0] - lam * (acc_ref[1] / l_ref[1])
    ms = jnp.mean(o * o, axis=-1, keepdims=True)
    o = o * lax.rsqrt(ms + EPS) * nw_ref[...] * (1.0 - LAMBDA_INIT)
    o_ref[...] = (o * z_ref[...].astype(F32)).astype(BF16)


def _diff_attention(qb, kb, vb, zb, lambda_q1, lambda_k1, lambda_q2, lambda_k2, da_norm_w,
                    batch, seq_len):
    tq = TQ_DA
    assert tq == TK_DA and seq_len % tq == 0
    nq = seq_len // tq
    lam_vec = lambda v: v.astype(F32).reshape(1, DA_QK_DIM)
    q_blk = pl.BlockSpec((tq, LANES), lambda b, h, i: (b * nq + i, h))
    kv_blk = pl.BlockSpec((seq_len, LANES), lambda b, h, i: (b, h))
    return pl.pallas_call(
        _diffattn_kernel,
        grid=(batch, DA_HEADS, nq),
        in_specs=[_resident((1, DA_QK_DIM))] * 4 + [q_blk, kv_blk, kv_blk, q_blk,
                                                    _resident((1, DA_V_DIM))],
        out_specs=q_blk,
        out_shape=jax.ShapeDtypeStruct((batch * seq_len, DA_WIDTH), BF16),
        scratch_shapes=[pltpu.VMEM((2, tq, LANES), F32), pltpu.VMEM((2, tq, LANES), F32),
                        pltpu.VMEM((2, tq, DA_V_DIM), F32)],
        compiler_params=pltpu.CompilerParams(
            dimension_semantics=("arbitrary", "arbitrary", "arbitrary"),
            vmem_limit_bytes=VMEM_LIMIT),
        name="diff_attention",
    )(lam_vec(lambda_q1), lam_vec(lambda_k1), lam_vec(lambda_q2), lam_vec(lambda_k2),
      qb, kb, vb, zb, da_norm_w.astype(F32).reshape(1, DA_V_DIM))


def _output_kernel(oa_ref, ob_ref, ga_ref, gb_ref, x_ref, woa_ref, wob_ref, wo_ref, out_ref):
    y_a = jnp.dot(oa_ref[...], woa_ref[...], preferred_element_type=F32)
    y_b = jnp.dot(ob_ref[...], wob_ref[...], preferred_element_type=F32)
    y = ga_ref[...].astype(F32) * y_a + gb_ref[...].astype(F32) * y_b
    out_ref[...] = x_ref[...] + jnp.dot(y.astype(BF16), wo_ref[...], preferred_element_type=F32)


def _output_projection(oa, ob, ga, gb, x2, w_out_a, w_out_b, w_out):
    m = x2.shape[0]
    tm = TM_OUT
    assert m % tm == 0
    row_blk = lambda w: pl.BlockSpec((tm, w), lambda i: (i, 0))
    return pl.pallas_call(
        _output_kernel,
        grid=(m // tm,),
        in_specs=[row_blk(DN_WIDTH), row_blk(DA_WIDTH), row_blk(D_MODEL), row_blk(D_MODEL),
                  row_blk(D_MODEL), _resident((DN_WIDTH, D_MODEL)), _resident((DA_WIDTH, D_MODEL)),
                  _resident((D_MODEL, D_MODEL))],
        out_specs=row_blk(D_MODEL),
        out_shape=jax.ShapeDtypeStruct((m, D_MODEL), F32),
        compiler_params=pltpu.CompilerParams(
            dimension_semantics=("arbitrary",), vmem_limit_bytes=VMEM_LIMIT),
        name="output_projection",
    )(oa, ob, ga, gb, x2, w_out_a.astype(BF16), w_out_b.astype(BF16), w_out.astype(BF16))


def kernel(x, norm_w, w_in, conv_w, a_log, dt_bias, dn_norm_w, q_norm_w, k_norm_w,
           lambda_q1, lambda_k1, lambda_q2, lambda_k2, da_norm_w, w_out_a, w_out_b, w_out):
    batch, seq_len, d_model = x.shape
    assert d_model == D_MODEL
    x2 = x.reshape(batch * seq_len, d_model)
    (qa, ka, va, za, gcol, grow, qb, kb, vb, zb, ga, gb) = _in_projection(
        x2, norm_w, w_in, conv_w, a_log, dt_bias, q_norm_w, k_norm_w, seq_len)
    oa = _deltanet(qa, ka, va, za, gcol, grow, dn_norm_w, batch, seq_len)
    ob = _diff_attention(qb, kb, vb, zb, lambda_q1, lambda_k1, lambda_q2, lambda_k2, da_norm_w,
                         batch, seq_len)
    out = _output_projection(oa, ob, ga, gb, x2, w_out_a, w_out_b, w_out)
    return out.reshape(batch, seq_len, d_model)
```

```python
import functools
import math

import jax
import jax.numpy as jnp
from jax import lax
from jax.experimental import pallas as pl
from jax.experimental.pallas import tpu as pltpu

F32 = jnp.float32
BF16 = jnp.bfloat16

D_MODEL = 1024
DN_HEADS = 4
DN_HEAD_DIM = 128
DN_WIDTH = DN_HEADS * DN_HEAD_DIM
CONV_WIDTH = 4
CHUNK = 64
DA_HEADS = 4
DA_QK_DIM = 64
DA_V_DIM = 2 * DA_QK_DIM
DA_WIDTH = DA_HEADS * DA_V_DIM
DA_QK_WIDTH = DA_HEADS * 2 * DA_QK_DIM
LAMBDA_INIT = 0.8 - 0.6 * math.exp(-0.3 * 0)
EPS = 1e-6
NEG_INF = -1e30
LOG2_E = math.log2(math.e)

LANES = 128
MXU_COLS = 256
SUBLANES = 8
BF16_ROWS = 16
AB_ROWS = BF16_ROWS
VMEM_LIMIT = 48 * 1024 * 1024

TM_IN = 512
TB_DN = 512
TQ_DA = 512
TK_DA = 512
HEADS_DA = 2
TM_OUT = 1024

NT_DIMS = (((1,), (1,)), ((), ()))
TN_DIMS = (((0,), (0,)), ((), ()))


def _sigmoid(v):
    return 0.5 + 0.5 * jnp.tanh(0.5 * v)


def _silu(v):
    h = 0.5 * v
    return h + h * jnp.tanh(h)


def _softplus(v):
    return jnp.maximum(v, 0.0) + jnp.log1p(jnp.exp(-jnp.abs(v)))


def _resident(shape):
    zeros = (0,) * len(shape)
    return pl.BlockSpec(shape, lambda *_: zeros, pipeline_mode=pl.Buffered(1))


def _inproj_kernel(x_ref, nw_ref, wqkv_ref, convw_ref, wza_ref, wab_ref, wabt_ref,
                   alog_l_ref, dtb_l_ref, alog_s_ref, dtb_s_ref, qnw_ref, knw_ref,
                   wqb_ref, wkb_ref, wvb_ref, wzb_ref, wga_ref, wgb_ref,
                   qa_ref, ka_ref, va_ref, za_ref, gcol_ref, grow_ref,
                   qb_ref, kb_ref, vb_ref, zb_ref, ga_ref, gb_ref,
                   conv_buf, *, tiles_per_seq):
    tm = x_ref.shape[0]
    half = tm // 2
    i = pl.program_id(0)
    head0 = SUBLANES - (CONV_WIDTH - 1)
    slab = 512

    @pl.when(i % tiles_per_seq == 0)
    def _():
        conv_buf[0:SUBLANES, :] = jnp.zeros((SUBLANES, 3 * DN_WIDTH), F32)

    def normed(r0):
        x = x_ref[r0:r0 + half, :]
        ms = jnp.mean(x * x, axis=-1, keepdims=True)
        return (x * lax.rsqrt(ms + EPS) * nw_ref[...]).astype(BF16)

    def tasks(h, r0):
        rows = slice(r0, r0 + half)

        def conv_qkv(s, out_ref):
            cs = slice(s * DN_WIDTH, (s + 1) * DN_WIDTH)

            res = []

            def matmul():
                res.append(jnp.dot(h, wqkv_ref[:, cs], preferred_element_type=F32))
                conv_buf[SUBLANES + r0:SUBLANES + r0 + half, cs] = res[0]

            def epilogue():
                acc = convw_ref[CONV_WIDTH - 1:CONV_WIDTH, cs] * res[0]
                for j in range(CONV_WIDTH - 1):
                    acc = acc + (convw_ref[j:j + 1, cs]
                                 * conv_buf[head0 + j + r0:head0 + j + r0 + half, cs])
                y = _silu(acc)
                if s == 2:
                    out_ref[rows, :] = y.astype(BF16)
                    return
                scale = DN_HEAD_DIM ** -0.5 if s == 0 else 1.0
                for hd in range(DN_HEADS):
                    hs = slice(hd * DN_HEAD_DIM, (hd + 1) * DN_HEAD_DIM)
                    blk = y[:, hs]
                    ss = jnp.sum(blk * blk, axis=-1, keepdims=True)
                    out_ref[rows, hs] = (blk * (lax.rsqrt(ss + EPS) * scale)).astype(BF16)
            return matmul, [epilogue]

        def plain(w_ref, c0, out_ref, act):
            res = []

            def matmul():
                res.append(jnp.dot(h, w_ref[:, c0:c0 + slab], preferred_element_type=F32))

            def chunk(n):
                def run():
                    cs = slice(n * LANES, (n + 1) * LANES)
                    out_ref[rows, c0 + n * LANES:c0 + (n + 1) * LANES] = (
                        act(res[0][:, cs]).astype(BF16))
                return run
            return matmul, [chunk(n) for n in range(slab // LANES)]

        def qk_norm(w_ref, nrm_ref, out_ref, scale):
            res = []

            def matmul():
                res.append(jnp.dot(h, w_ref[...], preferred_element_type=F32))

            def chunk(hd):
                def run():
                    lo = lax.broadcasted_iota(jnp.int32, (half, LANES), 1) < DA_QK_DIM
                    blk = res[0][:, hd * LANES:(hd + 1) * LANES]
                    sq = blk * blk
                    s_lo = jnp.sum(jnp.where(lo, sq, 0.0), axis=-1, keepdims=True)
                    s_hi = jnp.sum(jnp.where(lo, 0.0, sq), axis=-1, keepdims=True)
                    msq = jnp.where(lo, s_lo, s_hi) * (1.0 / DA_QK_DIM)
                    out_ref[rows, hd * LANES:(hd + 1) * LANES] = (
                        blk * lax.rsqrt(msq + EPS) * nrm_ref[...] * scale).astype(BF16)
                return run
            return matmul, [chunk(hd) for hd in range(DA_HEADS)]

        def decay_beta():
            def matmul():
                ab = jnp.dot(h, wab_ref[...], preferred_element_type=F32)
                lane = lax.broadcasted_iota(jnp.int32, ab.shape, 1)
                g_l = -jnp.exp(alog_l_ref[...]) * _softplus(ab + dtb_l_ref[...])
                gcol_ref[rows, :] = jnp.where(lane < DN_HEADS, g_l, _sigmoid(ab))
                abt = lax.dot_general(wabt_ref[...], h, NT_DIMS,
                                      preferred_element_type=F32)
                row = lax.broadcasted_iota(jnp.int32, abt.shape, 0)
                g_s = -jnp.exp(alog_s_ref[...]) * _softplus(abt + dtb_s_ref[...])
                grow_ref[:, rows] = jnp.where(row < DN_HEADS, g_s, _sigmoid(abt))
            return matmul, []

        ident = lambda v: v
        return [conv_qkv(0, qa_ref), plain(wga_ref, 0, ga_ref, _sigmoid),
                conv_qkv(1, ka_ref), plain(wga_ref, slab, ga_ref, _sigmoid),
                conv_qkv(2, va_ref), plain(wgb_ref, 0, gb_ref, _sigmoid),
                qk_norm(wqb_ref, qnw_ref, qb_ref, DA_QK_DIM ** -0.5 * LOG2_E),
                plain(wgb_ref, slab, gb_ref, _sigmoid),
                qk_norm(wkb_ref, knw_ref, kb_ref, 1.0), plain(wvb_ref, 0, vb_ref, ident),
                plain(wza_ref, 0, za_ref, _silu), plain(wzb_ref, 0, zb_ref, _silu),
                decay_beta()]

    def run_tasks(task_list, extra=None):
        for n, (matmul, chunks) in enumerate(task_list):
            matmul()
            for fn in chunks:
                fn()
            if extra is not None and n == 8:
                extra()

    h_second = []
    run_tasks(tasks(normed(0), 0), extra=lambda: h_second.append(normed(half)))
    run_tasks(tasks(h_second[0], half))
    conv_buf[0:SUBLANES, :] = conv_buf[tm:tm + SUBLANES, :]


def _in_projection(x2, norm_w, w_in, conv_w, a_log, dt_bias, q_norm_w, k_norm_w, seq_len):
    m = x2.shape[0]
    tm = TM_IN
    assert m % tm == 0 and seq_len % tm == 0
    o = 0
    cols = {}
    for name, width in (("qkv", 3 * DN_WIDTH), ("za", DN_WIDTH), ("a", DN_HEADS), ("b", DN_HEADS),
                        ("qb", DA_QK_WIDTH), ("kb", DA_QK_WIDTH), ("vb", DA_WIDTH), ("zb", DA_WIDTH),
                        ("ga", D_MODEL), ("gb", D_MODEL)):
        cols[name] = w_in[:, o:o + width]
        o += width
    assert o == w_in.shape[1]
    wb = {k: v.astype(BF16) for k, v in cols.items()}
    w_ab = jnp.concatenate([cols["a"], cols["b"]], axis=1)
    w_ab_l = jnp.pad(w_ab, ((0, 0), (0, LANES - 2 * DN_HEADS))).astype(BF16)
    w_ab_s = jnp.pad(w_ab.T, ((0, AB_ROWS - 2 * DN_HEADS), (0, 0))).astype(BF16)
    alog_l = jnp.pad(a_log.astype(F32), (0, LANES - DN_HEADS)).reshape(1, LANES)
    dtb_l = jnp.pad(dt_bias.astype(F32), (0, LANES - DN_HEADS)).reshape(1, LANES)
    alog_s = jnp.pad(a_log.astype(F32), (0, AB_ROWS - DN_HEADS)).reshape(AB_ROWS, 1)
    dtb_s = jnp.pad(dt_bias.astype(F32), (0, AB_ROWS - DN_HEADS)).reshape(AB_ROWS, 1)
    qnw = jnp.tile(q_norm_w.astype(F32), 2).reshape(1, LANES)
    knw = jnp.tile(k_norm_w.astype(F32), 2).reshape(1, LANES)

    row_blk = lambda w: pl.BlockSpec((tm, w), lambda i: (i, 0))
    in_specs = [
        row_blk(D_MODEL), _resident((1, D_MODEL)),
        _resident((D_MODEL, 3 * DN_WIDTH)), _resident((CONV_WIDTH, 3 * DN_WIDTH)),
        _resident((D_MODEL, DN_WIDTH)), _resident((D_MODEL, LANES)), _resident((AB_ROWS, D_MODEL)),
        _resident((1, LANES)), _resident((1, LANES)), _resident((AB_ROWS, 1)), _resident((AB_ROWS, 1)),
        _resident((1, LANES)), _resident((1, LANES)),
        _resident((D_MODEL, DA_QK_WIDTH)), _resident((D_MODEL, DA_QK_WIDTH)),
        _resident((D_MODEL, DA_WIDTH)), _resident((D_MODEL, DA_WIDTH)),
        _resident((D_MODEL, D_MODEL)), _resident((D_MODEL, D_MODEL)),
    ]
    out_shapes = [jax.ShapeDtypeStruct((m, DN_WIDTH), BF16)] * 4 + [
        jax.ShapeDtypeStruct((m, LANES), F32), jax.ShapeDtypeStruct((AB_ROWS, m), F32)] + [
        jax.ShapeDtypeStruct((m, DA_WIDTH), BF16)] * 4 + [
        jax.ShapeDtypeStruct((m, D_MODEL), BF16)] * 2
    out_specs = [row_blk(DN_WIDTH)] * 4 + [
        row_blk(LANES), pl.BlockSpec((AB_ROWS, tm), lambda i: (0, i))] + [
        row_blk(DA_WIDTH)] * 4 + [row_blk(D_MODEL)] * 2
    return pl.pallas_call(
        functools.partial(_inproj_kernel, tiles_per_seq=seq_len // tm),
        grid=(m // tm,),
        in_specs=in_specs,
        out_specs=out_specs,
        out_shape=out_shapes,
        scratch_shapes=[pltpu.VMEM((tm + SUBLANES, 3 * DN_WIDTH), F32)],
        compiler_params=pltpu.CompilerParams(
            dimension_semantics=("arbitrary",), vmem_limit_bytes=VMEM_LIMIT),
        name="in_projection",
    )(x2, norm_w.astype(F32).reshape(1, D_MODEL), wb["qkv"], conv_w.astype(F32), wb["za"],
      w_ab_l, w_ab_s, alog_l, dtb_l, alog_s, dtb_s, qnw, knw,
      wb["qb"], wb["kb"], wb["vb"], wb["zb"], wb["ga"], wb["gb"])


def _bmm(a, b):
    return jnp.einsum('nij,njk->nik', a, b, preferred_element_type=F32)


def _bmm_nt(a, b):
    return jnp.einsum('nid,njd->nij', a, b, preferred_element_type=F32)


def _split_bf16(x, parts):
    out = []
    for _ in range(parts - 1):
        hi = x.astype(BF16)
        out.append(hi)
        x = x - hi.astype(F32)
    out.append(x.astype(BF16))
    return out


def _deltanet_kernel(q_ref, k_ref, v_ref, z_ref, gcol_ref, grow_ref, nw_ref, o_ref,
                     state_ref, oraw_ref, lhs_ref, attn_ref, u_ref, kdec_ref, sdec_ref, *, nt):
    tb = q_ref.shape[0]
    c = CHUNK
    nc = tb // c
    nh = DN_HEADS
    t = pl.program_id(1)

    ri = lax.broadcasted_iota(jnp.int32, (c, c), 0)
    ci = lax.broadcasted_iota(jnp.int32, (c, c), 1)
    lower = ri >= ci
    strict = ri > ci
    eye = (ri == ci).astype(F32)
    tri = lower.astype(BF16)
    tri_t = (ri <= ci).astype(BF16)

    def heads_major(ref):
        return jnp.concatenate(
            [ref[:, hd * DN_HEAD_DIM:(hd + 1) * DN_HEAD_DIM].reshape(nc, c, DN_HEAD_DIM)
             for hd in range(nh)], axis=0)

    def phase1_stages():
        e = {}

        def decays():
            gb_col = gcol_ref[...].reshape(nc, c, LANES)
            tri_b = jnp.broadcast_to(tri, (nc, c, c))
            gc_col = sum(_bmm(tri_b, part) for part in _split_bf16(gb_col, 3))
            gb_row = grow_ref[...]
            gc_row = sum(jnp.dot(part, tri_t, preferred_element_type=F32)
                         for part in _split_bf16(gb_row.reshape(nc * AB_ROWS, c), 3)
                         ).reshape(nc, AB_ROWS, c)

            def col_form(src, lane):
                return jnp.concatenate(
                    [jnp.broadcast_to(src[:, :, lane + hd:lane + hd + 1], (nc, c, LANES))
                     for hd in range(nh)], axis=0)

            def row_form(src, row):
                return jnp.concatenate(
                    [src[:, row + hd:row + hd + 1, :] for hd in range(nh)], axis=0)

            e["g_c"] = col_form(gc_col, 0)
            e["beta_c"] = col_form(gb_col, nh)
            e["g_r"] = row_form(gc_row, 0)
            e["beta_r"] = row_form(gb_row, nh)
            e["g_last"] = e["g_c"][:, c - 1:c, :]
            diff = e["g_c"][:, :, :c] - e["g_r"]
            e["decay"] = jnp.where(lower, jnp.exp(jnp.where(lower, diff, 0.0)), 0.0)

        def gram_kk():
            e["q"], e["k"], e["v"] = heads_major(q_ref), heads_major(k_ref), heads_major(v_ref)
            l_strict = jnp.where(
                strict, _bmm_nt(e["k"], e["k"]) * e["decay"] * e["beta_c"][:, :, :c], 0.0)
            e["inv"] = eye - l_strict
            e["power"] = l_strict.astype(BF16)

        def gram_qk():
            e["attn"] = jnp.where(lower, _bmm_nt(e["q"], e["k"]) * e["decay"], 0.0).astype(BF16)

        def next_power():
            e["power"] = _bmm(e["power"], e["power"]).astype(BF16)

        def apply_power():
            e["inv"] = e["inv"] + _bmm(e["inv"].astype(BF16), e["power"])

        def solve_u():
            e["t_beta"] = e["inv"] * e["beta_r"]
            e["u"] = _bmm(e["t_beta"].astype(BF16), e["v"])

        def solve_w():
            e["w"] = _bmm((e["t_beta"] * jnp.exp(e["g_r"])).astype(BF16), e["k"]).astype(BF16)

        def commit():
            q_decay = (e["q"].astype(F32) * jnp.exp(e["g_c"])).astype(BF16)
            lhs_ref[...] = jnp.concatenate([e["w"], q_decay], axis=1)
            attn_ref[...] = e["attn"]
            u_ref[...] = e["u"]
            kdec_ref[...] = (e["k"].astype(F32) * jnp.exp(e["g_last"] - e["g_c"])).astype(BF16)
            sdec_ref[...] = jnp.exp(e["g_last"])

        levels = [next_power, apply_power] * (c.bit_length() - 2)
        return [decays, gram_kk, gram_qk] + levels + [solve_u, solve_w], commit

    def phase2_stages():
        def chunk_steps(n):
            idx = [hd * nc + n for hd in range(nh)]
            e = {}

            def through_state():
                e["s_old"] = [state_ref[hd] for hd in range(nh)]
                e["ws_qs"] = [jnp.dot(lhs_ref[b], e["s_old"][hd].astype(BF16),
                                      preferred_element_type=F32) for hd, b in enumerate(idx)]

            def update():
                v_new = [(u_ref[b] - e["ws_qs"][hd][:c]).astype(BF16) for hd, b in enumerate(idx)]
                for hd, b in enumerate(idx):
                    oraw_ref[n * c:(n + 1) * c, hd * DN_HEAD_DIM:(hd + 1) * DN_HEAD_DIM] = (
                        e["ws_qs"][hd][c:]
                        + jnp.dot(attn_ref[b], v_new[hd], preferred_element_type=F32))
                for hd, b in enumerate(idx):
                    state_ref[hd] = e["s_old"][hd] * sdec_ref[b] + lax.dot_general(
                        kdec_ref[b], v_new[hd], TN_DIMS, preferred_element_type=F32)
            return [through_state, update]

        def gated_norm():
            for hd in range(nh):
                hs = slice(hd * DN_HEAD_DIM, (hd + 1) * DN_HEAD_DIM)
                o = oraw_ref[:, hs]
                ms = jnp.mean(o * o, axis=-1, keepdims=True)
                o = o * lax.rsqrt(ms + EPS) * nw_ref[...]
                o_ref[:, hs] = (o * z_ref[:, hs].astype(F32)).astype(BF16)

        return [fn for n in range(nc) for fn in chunk_steps(n)] + [gated_norm]

    @pl.when(t == 0)
    def _():
        state_ref[...] = jnp.zeros(state_ref.shape, F32)
        stages, commit = phase1_stages()
        for fn in stages:
            fn()
        commit()

    @pl.when((t > 0) & (t < nt))
    def _():
        stages, commit = phase1_stages()
        others = phase2_stages()
        done = 0
        for n, fn in enumerate(stages):
            fn()
            upto = -(-(n + 1) * len(others) // len(stages))
            for other in others[done:upto]:
                other()
            done = upto
        commit()

    @pl.when(t == nt)
    def _():
        for fn in phase2_stages():
            fn()


def _deltanet(qa, ka, va, za, gcol, grow, dn_norm_w, batch, seq_len):
    tb = TB_DN
    assert seq_len % tb == 0 and tb % CHUNK == 0
    nt = seq_len // tb
    nc = tb // CHUNK
    nb = DN_HEADS * nc
    grow3 = grow.reshape(AB_ROWS, -1, CHUNK).transpose(1, 0, 2)
    cur = lambda w: pl.BlockSpec((tb, w), lambda b, t: (b * nt + jnp.minimum(t, nt - 1), 0))
    prev = lambda w: pl.BlockSpec((tb, w), lambda b, t: (b * nt + jnp.maximum(t - 1, 0), 0))
    return pl.pallas_call(
        functools.partial(_deltanet_kernel, nt=nt),
        grid=(batch, nt + 1),
        in_specs=[cur(DN_WIDTH)] * 3 + [
            prev(DN_WIDTH), cur(LANES),
            pl.BlockSpec((nc, AB_ROWS, CHUNK),
                         lambda b, t: (b * nt + jnp.minimum(t, nt - 1), 0, 0)),
            _resident((1, DN_HEAD_DIM))],
        out_specs=prev(DN_WIDTH),
        out_shape=jax.ShapeDtypeStruct((batch * seq_len, DN_WIDTH), BF16),
        scratch_shapes=[pltpu.VMEM((DN_HEADS, DN_HEAD_DIM, DN_HEAD_DIM), F32),
                        pltpu.VMEM((tb, DN_WIDTH), F32),
                        pltpu.VMEM((nb, 2 * CHUNK, DN_HEAD_DIM), BF16),
                        pltpu.VMEM((nb, CHUNK, CHUNK), BF16),
                        pltpu.VMEM((nb, CHUNK, DN_HEAD_DIM), F32),
                        pltpu.VMEM((nb, CHUNK, DN_HEAD_DIM), BF16),
                        pltpu.VMEM((nb, 1, DN_HEAD_DIM), F32)],
        compiler_params=pltpu.CompilerParams(
            dimension_semantics=("arbitrary", "arbitrary"), vmem_limit_bytes=VMEM_LIMIT),
        name="gated_deltanet",
    )(qa, ka, va, za, gcol, grow3, dn_norm_w.astype(F32).reshape(1, DN_HEAD_DIM))


def _diffattn_kernel(lq1_ref, lk1_ref, lq2_ref, lk2_ref, q_ref, k_ref, v_ref, z_ref, nw_ref,
                     o_ref, m_ref, l_ref, acc_ref):
    tq = q_ref.shape[0]
    tk = TK_DA
    qi = pl.program_id(2)
    chains = [(hd, comp) for hd in range(HEADS_DA) for comp in range(2)]

    lane = lax.broadcasted_iota(jnp.int32, (1, LANES), 1)
    first_half = lane < DA_QK_DIM
    q_chain = []
    for hd in range(HEADS_DA):
        q = q_ref[:, hd * LANES:(hd + 1) * LANES]
        zero = jnp.zeros_like(q)
        q_chain += [jnp.where(first_half, q, zero), jnp.where(first_half, zero, q)]

    m_ref[...] = jnp.full(m_ref.shape, NEG_INF, F32)
    l_ref[...] = jnp.zeros(l_ref.shape, F32)
    acc_ref[...] = jnp.zeros(acc_ref.shape, F32)

    def attend(j, masked):
        start = pl.multiple_of(j * tk, tk)
        kblk = [k_ref[pl.ds(start, tk), hd * LANES:(hd + 1) * LANES] for hd in range(HEADS_DA)]
        ones = jnp.ones((tk, LANES), BF16)
        vblk = [jnp.concatenate([v_ref[pl.ds(start, tk), hd * LANES:(hd + 1) * LANES], ones], axis=1)
                for hd in range(HEADS_DA)]
        scores = [lax.dot_general(q_chain[c], kblk[hd], NT_DIMS, preferred_element_type=F32)
                  for c, (hd, _) in enumerate(chains)]
        if masked:
            rq = lax.broadcasted_iota(jnp.int32, (tq, tk), 0)
            ck = lax.broadcasted_iota(jnp.int32, (tq, tk), 1)
            scores = [jnp.where(ck <= rq, s, NEG_INF) for s in scores]
        probs = []
        for c, s in enumerate(scores):
            m_prev = m_ref[c]
            m_new = jnp.maximum(m_prev, jnp.max(s, axis=-1, keepdims=True))
            alpha = jnp.exp2(m_prev - m_new)
            probs.append(jnp.exp2(
                (s - jnp.concatenate([m_new] * (tk // LANES), axis=1)).astype(BF16)))
            l_ref[c] = alpha * l_ref[c]
            acc_ref[c] = alpha * acc_ref[c]
            m_ref[c] = m_new
        for c, (hd, _) in enumerate(chains):
            pv = jnp.dot(probs[c], vblk[hd], preferred_element_type=F32)
            acc_ref[c] += pv[:, :DA_V_DIM]
            l_ref[c] += pv[:, DA_V_DIM:]

    def body(j, carry):
        attend(j, False)
        return carry

    lax.fori_loop(0, qi, body, 0)
    attend(qi, True)

    lam = (jnp.exp(jnp.sum(lq1_ref[...] * lk1_ref[...], axis=-1, keepdims=True))
           - jnp.exp(jnp.sum(lq2_ref[...] * lk2_ref[...], axis=-1, keepdims=True))
           + LAMBDA_INIT)
    for hd in range(HEADS_DA):
        hs = slice(hd * LANES, (hd + 1) * LANES)
        o = acc_ref[2 * hd] / l_ref[2 * hd] - lam * (acc_ref[2 * hd + 1] / l_ref[2 * hd + 1])
        ms = jnp.mean(o * o, axis=-1, keepdims=True)
        o = o * lax.rsqrt(ms + EPS) * nw_ref[...] * (1.0 - LAMBDA_INIT)
        o_ref[:, hs] = (o * z_ref[:, hs].astype(F32)).astype(BF16)


def _diff_attention(qb, kb, vb, zb, lambda_q1, lambda_k1, lambda_q2, lambda_k2, da_norm_w,
                    batch, seq_len):
    tq = TQ_DA
    assert tq == TK_DA and seq_len % tq == 0 and DA_HEADS % HEADS_DA == 0
    nq = seq_len // tq
    width = HEADS_DA * LANES
    lam_vec = lambda v: v.astype(F32).reshape(1, DA_QK_DIM)
    q_blk = pl.BlockSpec((tq, width), lambda b, h, i: (b * nq + i, h))
    kv_blk = pl.BlockSpec((seq_len, width), lambda b, h, i: (b, h))
    stat = pltpu.VMEM((2 * HEADS_DA, tq, LANES), F32)
    return pl.pallas_call(
        _diffattn_kernel,
        grid=(batch, DA_HEADS // HEADS_DA, nq),
        in_specs=[_resident((1, DA_QK_DIM))] * 4 + [q_blk, kv_blk, kv_blk, q_blk,
                                                    _resident((1, DA_V_DIM))],
        out_specs=q_blk,
        out_shape=jax.ShapeDtypeStruct((batch * seq_len, DA_WIDTH), BF16),
        scratch_shapes=[stat, stat, pltpu.VMEM((2 * HEADS_DA, tq, DA_V_DIM), F32)],
        compiler_params=pltpu.CompilerParams(
            dimension_semantics=("arbitrary", "arbitrary", "arbitrary"),
            vmem_limit_bytes=VMEM_LIMIT),
        name="diff_attention",
    )(lam_vec(lambda_q1), lam_vec(lambda_k1), lam_vec(lambda_q2), lam_vec(lambda_k2),
      qb, kb, vb, zb, da_norm_w.astype(F32).reshape(1, DA_V_DIM))


def _output_kernel(oa_ref, ob_ref, ga_ref, gb_ref, x_ref, woa_ref, wob_ref, wo_ref, out_ref):
    y_a = jnp.dot(oa_ref[...], woa_ref[...], preferred_element_type=F32)
    y_b = jnp.dot(ob_ref[...], wob_ref[...], preferred_element_type=F32)
    y = ga_ref[...].astype(F32) * y_a + gb_ref[...].astype(F32) * y_b
    out_ref[...] = x_ref[...] + jnp.dot(y.astype(BF16), wo_ref[...], preferred_element_type=F32)


def _output_projection(oa, ob, ga, gb, x2, w_out_a, w_out_b, w_out):
    m = x2.shape[0]
    tm = TM_OUT
    assert m % tm == 0
    row_blk = lambda w: pl.BlockSpec((tm, w), lambda i: (i, 0))
    return pl.pallas_call(
        _output_kernel,
        grid=(m // tm,),
        in_specs=[row_blk(DN_WIDTH), row_blk(DA_WIDTH), row_blk(D_MODEL), row_blk(D_MODEL),
                  row_blk(D_MODEL), _resident((DN_WIDTH, D_MODEL)), _resident((DA_WIDTH, D_MODEL)),
                  _resident((D_MODEL, D_MODEL))],
        out_specs=row_blk(D_MODEL),
        out_shape=jax.ShapeDtypeStruct((m, D_MODEL), F32),
        compiler_params=pltpu.CompilerParams(
            dimension_semantics=("arbitrary",), vmem_limit_bytes=VMEM_LIMIT),
        name="output_projection",
    )(oa, ob, ga, gb, x2, w_out_a.astype(BF16), w_out_b.astype(BF16), w_out.astype(BF16))


def kernel(x, norm_w, w_in, conv_w, a_log, dt_bias, dn_norm_w, q_norm_w, k_norm_w,
           lambda_q1, lambda_k1, lambda_q2, lambda_k2, da_norm_w, w_out_a, w_out_b, w_out):
    batch, seq_len, d_model = x.shape
    assert d_model == D_MODEL
    x2 = x.reshape(batch * seq_len, d_model)
    (qa, ka, va, za, gcol, grow, qb, kb, vb, zb, ga, gb) = _in_projection(
        x2, norm_w, w_in, conv_w, a_log, dt_bias, q_norm_w, k_norm_w, seq_len)
    oa = _deltanet(qa, ka, va, za, gcol, grow, dn_norm_w, batch, seq_len)
    ob = _diff_attention(qb, kb, vb, zb, lambda_q1, lambda_k1, lambda_q2, lambda_k2, da_norm_w,
                         batch, seq_len)
    out = _output_projection(oa, ob, ga, gb, x2, w_out_a, w_out_b, w_out)
    return out.reshape(batch, seq_len, d_model)
```

```python
import functools
import math

import jax
import jax.numpy as jnp
from jax import lax
from jax.experimental import pallas as pl
from jax.experimental.pallas import tpu as pltpu

F32 = jnp.float32
BF16 = jnp.bfloat16

D_MODEL = 1024
DN_HEADS = 4
DN_HEAD_DIM = 128
DN_WIDTH = DN_HEADS * DN_HEAD_DIM
CONV_WIDTH = 4
CHUNK = 64
DA_HEADS = 4
DA_QK_DIM = 64
DA_V_DIM = 2 * DA_QK_DIM
DA_WIDTH = DA_HEADS * DA_V_DIM
DA_QK_WIDTH = DA_HEADS * 2 * DA_QK_DIM
LAMBDA_INIT = 0.8 - 0.6 * math.exp(-0.3 * 0)
EPS = 1e-6
NEG_INF = -1e30
LOG2_E = math.log2(math.e)

LANES = 128
MXU_COLS = 256
SUBLANES = 8
BF16_ROWS = 16
AB_ROWS = BF16_ROWS
VMEM_LIMIT = 48 * 1024 * 1024

TM_IN = 512
TB_DN = 512
TQ_DA = 512
TK_DA = 512
HEADS_DA = 4
TM_OUT = 1024
TR_SPLIT = 128

NT_DIMS = (((1,), (1,)), ((), ()))
TN_DIMS = (((0,), (0,)), ((), ()))


def _sigmoid(v):
    return 0.5 + 0.5 * jnp.tanh(0.5 * v)


def _silu(v):
    h = 0.5 * v
    return h + h * jnp.tanh(h)


def _softplus(v):
    return jnp.maximum(v, 0.0) + jnp.log1p(jnp.exp(-jnp.abs(v)))


def _resident(shape):
    zeros = (0,) * len(shape)
    return pl.BlockSpec(shape, lambda *_: zeros, pipeline_mode=pl.Buffered(1))


def _inproj_kernel(x_ref, nw_ref, wqkv_ref, convw_ref, wza_ref, wab_ref, wabt_ref,
                   alog_l_ref, dtb_l_ref, alog_s_ref, dtb_s_ref, qnw_ref, knw_ref,
                   wqb_ref, wkb_ref, wvb_ref, wzb_ref, wga_ref, wgb_ref,
                   qa_ref, ka_ref, va_ref, za_ref, gcol_ref, grow_ref,
                   qb_ref, kb_ref, vb_ref, zb_ref, ga_ref, gb_ref,
                   conv_buf, *, tiles_per_seq):
    tm = x_ref.shape[0]
    half = tm // 2
    i = pl.program_id(0)
    head0 = SUBLANES - (CONV_WIDTH - 1)
    slab = 512

    @pl.when(i % tiles_per_seq == 0)
    def _():
        conv_buf[0:SUBLANES, :] = jnp.zeros((SUBLANES, 3 * DN_WIDTH), F32)

    def normed(r0):
        x = x_ref[r0:r0 + half, :]
        ms = jnp.mean(x * x, axis=-1, keepdims=True)
        return (x * lax.rsqrt(ms + EPS) * nw_ref[...]).astype(BF16)

    def tasks(h, r0):
        rows = slice(r0, r0 + half)

        def conv_qkv(s, out_ref):
            cs = slice(s * DN_WIDTH, (s + 1) * DN_WIDTH)

            res = []

            def matmul():
                res.append(jnp.dot(h, wqkv_ref[:, cs], preferred_element_type=F32))
                conv_buf[SUBLANES + r0:SUBLANES + r0 + half, cs] = res[0]

            def epilogue():
                acc = convw_ref[CONV_WIDTH - 1:CONV_WIDTH, cs] * res[0]
                for j in range(CONV_WIDTH - 1):
                    acc = acc + (convw_ref[j:j + 1, cs]
                                 * conv_buf[head0 + j + r0:head0 + j + r0 + half, cs])
                y = _silu(acc)
                if s == 2:
                    out_ref[rows, :] = y.astype(BF16)
                    return
                scale = DN_HEAD_DIM ** -0.5 if s == 0 else 1.0
                for hd in range(DN_HEADS):
                    hs = slice(hd * DN_HEAD_DIM, (hd + 1) * DN_HEAD_DIM)
                    blk = y[:, hs]
                    ss = jnp.sum(blk * blk, axis=-1, keepdims=True)
                    out_ref[rows, hs] = (blk * (lax.rsqrt(ss + EPS) * scale)).astype(BF16)
            return matmul, [epilogue]

        def plain(w_ref, c0, out_ref, act):
            res = []

            def matmul():
                res.append(jnp.dot(h, w_ref[:, c0:c0 + slab], preferred_element_type=F32))

            def chunk(n):
                def run():
                    cs = slice(n * LANES, (n + 1) * LANES)
                    out_ref[rows, c0 + n * LANES:c0 + (n + 1) * LANES] = (
                        act(res[0][:, cs]).astype(BF16))
                return run
            return matmul, [chunk(n) for n in range(slab // LANES)]

        def qk_norm(w_ref, nrm_ref, out_ref, scale):
            res = []

            def matmul():
                res.append(jnp.dot(h, w_ref[...], preferred_element_type=F32))

            def chunk(hd):
                def run():
                    lo = lax.broadcasted_iota(jnp.int32, (half, LANES), 1) < DA_QK_DIM
                    blk = res[0][:, hd * LANES:(hd + 1) * LANES]
                    sq = blk * blk
                    s_lo = jnp.sum(jnp.where(lo, sq, 0.0), axis=-1, keepdims=True)
                    s_hi = jnp.sum(jnp.where(lo, 0.0, sq), axis=-1, keepdims=True)
                    msq = jnp.where(lo, s_lo, s_hi) * (1.0 / DA_QK_DIM)
                    out_ref[rows, hd * LANES:(hd + 1) * LANES] = (
                        blk * lax.rsqrt(msq + EPS) * nrm_ref[...] * scale).astype(BF16)
                return run
            return matmul, [chunk(hd) for hd in range(DA_HEADS)]

        def decay_beta():
            def matmul():
                ab = jnp.dot(h, wab_ref[...], preferred_element_type=F32)
                lane = lax.broadcasted_iota(jnp.int32, ab.shape, 1)
                g_l = -jnp.exp(alog_l_ref[...]) * _softplus(ab + dtb_l_ref[...])
                gcol_ref[rows, :] = jnp.where(lane < DN_HEADS, g_l, _sigmoid(ab))
                abt = lax.dot_general(wabt_ref[...], h, NT_DIMS,
                                      preferred_element_type=F32)
                row = lax.broadcasted_iota(jnp.int32, abt.shape, 0)
                g_s = -jnp.exp(alog_s_ref[...]) * _softplus(abt + dtb_s_ref[...])
                grow_ref[:, rows] = jnp.where(row < DN_HEADS, g_s, _sigmoid(abt))
            return matmul, []

        ident = lambda v: v
        return [conv_qkv(0, qa_ref), plain(wga_ref, 0, ga_ref, _sigmoid),
                conv_qkv(1, ka_ref), plain(wga_ref, slab, ga_ref, _sigmoid),
                conv_qkv(2, va_ref), plain(wgb_ref, 0, gb_ref, _sigmoid),
                qk_norm(wqb_ref, qnw_ref, qb_ref, DA_QK_DIM ** -0.5 * LOG2_E),
                plain(wgb_ref, slab, gb_ref, _sigmoid),
                qk_norm(wkb_ref, knw_ref, kb_ref, 1.0), plain(wvb_ref, 0, vb_ref, ident),
                plain(wza_ref, 0, za_ref, _silu), plain(wzb_ref, 0, zb_ref, _silu),
                decay_beta()]

    def run_tasks(task_list, extra=None):
        for n, (matmul, chunks) in enumerate(task_list):
            matmul()
            for fn in chunks:
                fn()
            if extra is not None and n == 8:
                extra()

    h_second = []
    run_tasks(tasks(normed(0), 0), extra=lambda: h_second.append(normed(half)))
    run_tasks(tasks(h_second[0], half))
    conv_buf[0:SUBLANES, :] = conv_buf[tm:tm + SUBLANES, :]


def _split_cast_kernel(w_ref, *out_refs, spans):
    for out_ref, (start, width) in zip(out_refs, spans):
        out_ref[...] = w_ref[:, start:start + width].astype(BF16)


def _split_cast_weights(w, spans):
    rows, cols = w.shape
    tr = TR_SPLIT
    assert rows % tr == 0
    names = list(spans)
    outs = pl.pallas_call(
        functools.partial(_split_cast_kernel, spans=tuple(spans[n] for n in names)),
        grid=(rows // tr,),
        in_specs=[pl.BlockSpec((tr, cols), lambda i: (i, 0))],
        out_specs=[pl.BlockSpec((tr, spans[n][1]), lambda i: (i, 0)) for n in names],
        out_shape=[jax.ShapeDtypeStruct((rows, spans[n][1]), BF16) for n in names],
        compiler_params=pltpu.CompilerParams(
            dimension_semantics=("arbitrary",), vmem_limit_bytes=VMEM_LIMIT),
        name="split_cast_weights",
    )(w)
    return dict(zip(names, outs))


def _in_projection(x2, norm_w, w_in, conv_w, a_log, dt_bias, q_norm_w, k_norm_w, seq_len):
    m = x2.shape[0]
    tm = TM_IN
    assert m % tm == 0 and seq_len % tm == 0
    o = 0
    spans = {}
    for name, width in (("qkv", 3 * DN_WIDTH), ("za", DN_WIDTH), ("ab", 2 * DN_HEADS),
                        ("qb", DA_QK_WIDTH), ("kb", DA_QK_WIDTH), ("vb", DA_WIDTH), ("zb", DA_WIDTH),
                        ("ga", D_MODEL), ("gb", D_MODEL)):
        spans[name] = (o, width)
        o += width
    assert o == w_in.shape[1]
    wb = _split_cast_weights(w_in, {k: v for k, v in spans.items() if k != "ab"})
    w_ab = w_in[:, spans["ab"][0]:spans["ab"][0] + spans["ab"][1]]
    w_ab_l = jnp.pad(w_ab, ((0, 0), (0, LANES - 2 * DN_HEADS))).astype(BF16)
    w_ab_s = jnp.pad(w_ab.T, ((0, AB_ROWS - 2 * DN_HEADS), (0, 0))).astype(BF16)
    alog_l = jnp.pad(a_log.astype(F32), (0, LANES - DN_HEADS)).reshape(1, LANES)
    dtb_l = jnp.pad(dt_bias.astype(F32), (0, LANES - DN_HEADS)).reshape(1, LANES)
    alog_s = jnp.pad(a_log.astype(F32), (0, AB_ROWS - DN_HEADS)).reshape(AB_ROWS, 1)
    dtb_s = jnp.pad(dt_bias.astype(F32), (0, AB_ROWS - DN_HEADS)).reshape(AB_ROWS, 1)
    qnw = jnp.tile(q_norm_w.astype(F32), 2).reshape(1, LANES)
    knw = jnp.tile(k_norm_w.astype(F32), 2).reshape(1, LANES)

    row_blk = lambda w: pl.BlockSpec((tm, w), lambda i: (i, 0))
    in_specs = [
        row_blk(D_MODEL), _resident((1, D_MODEL)),
        _resident((D_MODEL, 3 * DN_WIDTH)), _resident((CONV_WIDTH, 3 * DN_WIDTH)),
        _resident((D_MODEL, DN_WIDTH)), _resident((D_MODEL, LANES)), _resident((AB_ROWS, D_MODEL)),
        _resident((1, LANES)), _resident((1, LANES)), _resident((AB_ROWS, 1)), _resident((AB_ROWS, 1)),
        _resident((1, LANES)), _resident((1, LANES)),
        _resident((D_MODEL, DA_QK_WIDTH)), _resident((D_MODEL, DA_QK_WIDTH)),
        _resident((D_MODEL, DA_WIDTH)), _resident((D_MODEL, DA_WIDTH)),
        _resident((D_MODEL, D_MODEL)), _resident((D_MODEL, D_MODEL)),
    ]
    out_shapes = [jax.ShapeDtypeStruct((m, DN_WIDTH), BF16)] * 4 + [
        jax.ShapeDtypeStruct((m, LANES), F32), jax.ShapeDtypeStruct((AB_ROWS, m), F32)] + [
        jax.ShapeDtypeStruct((m, DA_WIDTH), BF16)] * 4 + [
        jax.ShapeDtypeStruct((m, D_MODEL), BF16)] * 2
    out_specs = [row_blk(DN_WIDTH)] * 4 + [
        row_blk(LANES), pl.BlockSpec((AB_ROWS, tm), lambda i: (0, i))] + [
        row_blk(DA_WIDTH)] * 4 + [row_blk(D_MODEL)] * 2
    return pl.pallas_call(
        functools.partial(_inproj_kernel, tiles_per_seq=seq_len // tm),
        grid=(m // tm,),
        in_specs=in_specs,
        out_specs=out_specs,
        out_shape=out_shapes,
        scratch_shapes=[pltpu.VMEM((tm + SUBLANES, 3 * DN_WIDTH), F32)],
        compiler_params=pltpu.CompilerParams(
            dimension_semantics=("arbitrary",), vmem_limit_bytes=VMEM_LIMIT),
        name="in_projection",
    )(x2, norm_w.astype(F32).reshape(1, D_MODEL), wb["qkv"], conv_w.astype(F32), wb["za"],
      w_ab_l, w_ab_s, alog_l, dtb_l, alog_s, dtb_s, qnw, knw,
      wb["qb"], wb["kb"], wb["vb"], wb["zb"], wb["ga"], wb["gb"])


def _bmm(a, b):
    return jnp.einsum('nij,njk->nik', a, b, preferred_element_type=F32)


def _bmm_nt(a, b):
    return jnp.einsum('nid,njd->nij', a, b, preferred_element_type=F32)


def _split_bf16(x, parts):
    out = []
    for _ in range(parts - 1):
        hi = x.astype(BF16)
        out.append(hi)
        x = x - hi.astype(F32)
    out.append(x.astype(BF16))
    return out


def _deltanet_kernel(q_ref, k_ref, v_ref, z_ref, gcol_ref, grow_ref, nw_ref, o_ref,
                     state_ref, oraw_ref, lhs_ref, attn_ref, u_ref, kdec_ref, sdec_ref, *, nt):
    tb = q_ref.shape[0]
    c = CHUNK
    nc = tb // c
    nh = DN_HEADS
    t = pl.program_id(1)

    ri = lax.broadcasted_iota(jnp.int32, (c, c), 0)
    ci = lax.broadcasted_iota(jnp.int32, (c, c), 1)
    lower = ri >= ci
    strict = ri > ci
    eye = (ri == ci).astype(F32)
    tri = lower.astype(BF16)
    tri_t = (ri <= ci).astype(BF16)

    def heads_major(ref):
        return jnp.concatenate(
            [ref[:, hd * DN_HEAD_DIM:(hd + 1) * DN_HEAD_DIM].reshape(nc, c, DN_HEAD_DIM)
             for hd in range(nh)], axis=0)

    def phase1_stages():
        e = {}

        def decays():
            gb_col = gcol_ref[...].reshape(nc, c, LANES)
            tri_b = jnp.broadcast_to(tri, (nc, c, c))
            gc_col = sum(_bmm(tri_b, part) for part in _split_bf16(gb_col, 3))
            gb_row = grow_ref[...]
            gc_row = sum(jnp.dot(part, tri_t, preferred_element_type=F32)
                         for part in _split_bf16(gb_row.reshape(nc * AB_ROWS, c), 3)
                         ).reshape(nc, AB_ROWS, c)

            def col_form(src, lane):
                return jnp.concatenate(
                    [jnp.broadcast_to(src[:, :, lane + hd:lane + hd + 1], (nc, c, LANES))
                     for hd in range(nh)], axis=0)

            def row_form(src, row):
                return jnp.concatenate(
                    [src[:, row + hd:row + hd + 1, :] for hd in range(nh)], axis=0)

            e["g_c"] = col_form(gc_col, 0)
            e["beta_c"] = col_form(gb_col, nh)
            e["g_r"] = row_form(gc_row, 0)
            e["beta_r"] = row_form(gb_row, nh)
            e["g_last"] = e["g_c"][:, c - 1:c, :]
            diff = e["g_c"][:, :, :c] - e["g_r"]
            e["decay"] = jnp.where(lower, jnp.exp(jnp.where(lower, diff, 0.0)), 0.0)

        def gram_kk():
            e["q"], e["k"], e["v"] = heads_major(q_ref), heads_major(k_ref), heads_major(v_ref)
            l_strict = jnp.where(
                strict, _bmm_nt(e["k"], e["k"]) * e["decay"] * e["beta_c"][:, :, :c], 0.0)
            e["inv"] = eye - l_strict
            e["power"] = l_strict.astype(BF16)

        def gram_qk():
            e["attn"] = jnp.where(lower, _bmm_nt(e["q"], e["k"]) * e["decay"], 0.0).astype(BF16)

        def next_power():
            e["power"] = _bmm(e["power"], e["power"]).astype(BF16)

        def apply_power():
            e["inv"] = e["inv"] + _bmm(e["inv"].astype(BF16), e["power"])

        def solve_u():
            e["t_beta"] = e["inv"] * e["beta_r"]
            e["u"] = _bmm(e["t_beta"].astype(BF16), e["v"])

        def solve_w():
            e["w"] = _bmm((e["t_beta"] * jnp.exp(e["g_r"])).astype(BF16), e["k"]).astype(BF16)

        def commit():
            q_decay = (e["q"].astype(F32) * jnp.exp(e["g_c"])).astype(BF16)
            lhs_ref[...] = jnp.concatenate([e["w"], q_decay], axis=1)
            attn_ref[...] = e["attn"]
            u_ref[...] = e["u"]
            kdec_ref[...] = (e["k"].astype(F32) * jnp.exp(e["g_last"] - e["g_c"])).astype(BF16)
            sdec_ref[...] = jnp.exp(e["g_last"])

        levels = [next_power, apply_power] * (c.bit_length() - 2)
        return [decays, gram_kk, gram_qk] + levels + [solve_u, solve_w], commit

    def phase2_stages():
        def chunk_steps(n):
            idx = [hd * nc + n for hd in range(nh)]
            e = {}

            def through_state():
                e["s_old"] = [state_ref[hd] for hd in range(nh)]
                e["ws_qs"] = [jnp.dot(lhs_ref[b], e["s_old"][hd].astype(BF16),
                                      preferred_element_type=F32) for hd, b in enumerate(idx)]

            def update():
                v_new = [(u_ref[b] - e["ws_qs"][hd][:c]).astype(BF16) for hd, b in enumerate(idx)]
                for hd, b in enumerate(idx):
                    oraw_ref[n * c:(n + 1) * c, hd * DN_HEAD_DIM:(hd + 1) * DN_HEAD_DIM] = (
                        e["ws_qs"][hd][c:]
                        + jnp.dot(attn_ref[b], v_new[hd], preferred_element_type=F32))
                for hd, b in enumerate(idx):
                    state_ref[hd] = e["s_old"][hd] * sdec_ref[b] + lax.dot_general(
                        kdec_ref[b], v_new[hd], TN_DIMS, preferred_element_type=F32)
            return [through_state, update]

        def gated_norm():
            for hd in range(nh):
                hs = slice(hd * DN_HEAD_DIM, (hd + 1) * DN_HEAD_DIM)
                o = oraw_ref[:, hs]
                ms = jnp.mean(o * o, axis=-1, keepdims=True)
                o = o * lax.rsqrt(ms + EPS) * nw_ref[...]
                o_ref[:, hs] = (o * z_ref[:, hs].astype(F32)).astype(BF16)

        return [fn for n in range(nc) for fn in chunk_steps(n)] + [gated_norm]

    @pl.when(t == 0)
    def _():
        state_ref[...] = jnp.zeros(state_ref.shape, F32)
        stages, commit = phase1_stages()
        for fn in stages:
            fn()
        commit()

    @pl.when((t > 0) & (t < nt))
    def _():
        stages, commit = phase1_stages()
        others = phase2_stages()
        done = 0
        for n, fn in enumerate(stages):
            fn()
            upto = -(-(n + 1) * len(others) // len(stages))
            for other in others[done:upto]:
                other()
            done = upto
        commit()

    @pl.when(t == nt)
    def _():
        for fn in phase2_stages():
            fn()


def _deltanet(qa, ka, va, za, gcol, grow, dn_norm_w, batch, seq_len):
    tb = TB_DN
    assert seq_len % tb == 0 and tb % CHUNK == 0
    nt = seq_len // tb
    nc = tb // CHUNK
    nb = DN_HEADS * nc
    grow3 = grow.reshape(AB_ROWS, -1, CHUNK).transpose(1, 0, 2)
    cur = lambda w: pl.BlockSpec((tb, w), lambda b, t: (b * nt + jnp.minimum(t, nt - 1), 0))
    prev = lambda w: pl.BlockSpec((tb, w), lambda b, t: (b * nt + jnp.maximum(t - 1, 0), 0))
    return pl.pallas_call(
        functools.partial(_deltanet_kernel, nt=nt),
        grid=(batch, nt + 1),
        in_specs=[cur(DN_WIDTH)] * 3 + [
            prev(DN_WIDTH), cur(LANES),
            pl.BlockSpec((nc, AB_ROWS, CHUNK),
                         lambda b, t: (b * nt + jnp.minimum(t, nt - 1), 0, 0)),
            _resident((1, DN_HEAD_DIM))],
        out_specs=prev(DN_WIDTH),
        out_shape=jax.ShapeDtypeStruct((batch * seq_len, DN_WIDTH), BF16),
        scratch_shapes=[pltpu.VMEM((DN_HEADS, DN_HEAD_DIM, DN_HEAD_DIM), F32),
                        pltpu.VMEM((tb, DN_WIDTH), F32),
                        pltpu.VMEM((nb, 2 * CHUNK, DN_HEAD_DIM), BF16),
                        pltpu.VMEM((nb, CHUNK, CHUNK), BF16),
                        pltpu.VMEM((nb, CHUNK, DN_HEAD_DIM), F32),
                        pltpu.VMEM((nb, CHUNK, DN_HEAD_DIM), BF16),
                        pltpu.VMEM((nb, 1, DN_HEAD_DIM), F32)],
        compiler_params=pltpu.CompilerParams(
            dimension_semantics=("arbitrary", "arbitrary"), vmem_limit_bytes=VMEM_LIMIT),
        name="gated_deltanet",
    )(qa, ka, va, za, gcol, grow3, dn_norm_w.astype(F32).reshape(1, DN_HEAD_DIM))


def _diffattn_kernel(lq1_ref, lk1_ref, lq2_ref, lk2_ref, q_ref, k_ref, v_ref, z_ref, nw_ref,
                     o_ref, m_ref, l_ref, acc_ref):
    tq = q_ref.shape[0]
    tk = TK_DA
    qi = pl.program_id(2)
    chains = [(hd, comp) for hd in range(HEADS_DA) for comp in range(2)]

    lane = lax.broadcasted_iota(jnp.int32, (1, LANES), 1)
    first_half = lane < DA_QK_DIM
    q_chain = []
    for hd in range(HEADS_DA):
        q = q_ref[:, hd * LANES:(hd + 1) * LANES]
        zero = jnp.zeros_like(q)
        q_chain += [jnp.where(first_half, q, zero), jnp.where(first_half, zero, q)]

    m_ref[...] = jnp.full(m_ref.shape, NEG_INF, F32)
    l_ref[...] = jnp.zeros(l_ref.shape, F32)
    acc_ref[...] = jnp.zeros(acc_ref.shape, F32)

    def attend(j, masked):
        start = pl.multiple_of(j * tk, tk)
        kblk = [k_ref[pl.ds(start, tk), hd * LANES:(hd + 1) * LANES] for hd in range(HEADS_DA)]
        ones = jnp.ones((tk, LANES), BF16)
        vblk = [jnp.concatenate([v_ref[pl.ds(start, tk), hd * LANES:(hd + 1) * LANES], ones], axis=1)
                for hd in range(HEADS_DA)]
        scores = [lax.dot_general(q_chain[c], kblk[hd], NT_DIMS, preferred_element_type=F32)
                  for c, (hd, _) in enumerate(chains)]
        if masked:
            rq = lax.broadcasted_iota(jnp.int32, (tq, tk), 0)
            ck = lax.broadcasted_iota(jnp.int32, (tq, tk), 1)
            scores = [jnp.where(ck <= rq, s, NEG_INF) for s in scores]
        probs = []
        for c, s in enumerate(scores):
            m_prev = m_ref[c]
            m_new = jnp.maximum(m_prev, jnp.max(s, axis=-1, keepdims=True))
            alpha = jnp.exp2(m_prev - m_new)
            probs.append(jnp.exp2(
                (s - jnp.concatenate([m_new] * (tk // LANES), axis=1)).astype(BF16)))
            l_ref[c] = alpha * l_ref[c]
            acc_ref[c] = alpha * acc_ref[c]
            m_ref[c] = m_new
        for c, (hd, _) in enumerate(chains):
            pv = jnp.dot(probs[c], vblk[hd], preferred_element_type=F32)
            acc_ref[c] += pv[:, :DA_V_DIM]
            l_ref[c] += pv[:, DA_V_DIM:]

    def body(j, carry):
        attend(j, False)
        return carry

    lax.fori_loop(0, qi, body, 0)
    attend(qi, True)

    lam = (jnp.exp(jnp.sum(lq1_ref[...] * lk1_ref[...], axis=-1, keepdims=True))
           - jnp.exp(jnp.sum(lq2_ref[...] * lk2_ref[...], axis=-1, keepdims=True))
           + LAMBDA_INIT)
    for hd in range(HEADS_DA):
        hs = slice(hd * LANES, (hd + 1) * LANES)
        o = acc_ref[2 * hd] / l_ref[2 * hd] - lam * (acc_ref[2 * hd + 1] / l_ref[2 * hd + 1])
        ms = jnp.mean(o * o, axis=-1, keepdims=True)
        o = o * lax.rsqrt(ms + EPS) * nw_ref[...] * (1.0 - LAMBDA_INIT)
        o_ref[:, hs] = (o * z_ref[:, hs].astype(F32)).astype(BF16)


def _diff_attention(qb, kb, vb, zb, lambda_q1, lambda_k1, lambda_q2, lambda_k2, da_norm_w,
                    batch, seq_len):
    tq = TQ_DA
    assert tq == TK_DA and seq_len % tq == 0 and DA_HEADS % HEADS_DA == 0
    nq = seq_len // tq
    width = HEADS_DA * LANES
    lam_vec = lambda v: v.astype(F32).reshape(1, DA_QK_DIM)
    q_blk = pl.BlockSpec((tq, width), lambda b, h, i: (b * nq + i, h))
    kv_blk = pl.BlockSpec((seq_len, width), lambda b, h, i: (b, h))
    stat = pltpu.VMEM((2 * HEADS_DA, tq, LANES), F32)
    return pl.pallas_call(
        _diffattn_kernel,
        grid=(batch, DA_HEADS // HEADS_DA, nq),
        in_specs=[_resident((1, DA_QK_DIM))] * 4 + [q_blk, kv_blk, kv_blk, q_blk,
                                                    _resident((1, DA_V_DIM))],
        out_specs=q_blk,
        out_shape=jax.ShapeDtypeStruct((batch * seq_len, DA_WIDTH), BF16),
        scratch_shapes=[stat, stat, pltpu.VMEM((2 * HEADS_DA, tq, DA_V_DIM), F32)],
        compiler_params=pltpu.CompilerParams(
            dimension_semantics=("arbitrary", "arbitrary", "arbitrary"),
            vmem_limit_bytes=VMEM_LIMIT),
        name="diff_attention",
    )(lam_vec(lambda_q1), lam_vec(lambda_k1), lam_vec(lambda_q2), lam_vec(lambda_k2),
      qb, kb, vb, zb, da_norm_w.astype(F32).reshape(1, DA_V_DIM))


def _output_kernel(oa_ref, ob_ref, ga_ref, gb_ref, x_ref, woa_ref, wob_ref, wo_ref, out_ref):
    y_a = jnp.dot(oa_ref[...], woa_ref[...], preferred_element_type=F32)
    y_b = jnp.dot(ob_ref[...], wob_ref[...], preferred_element_type=F32)
    y = ga_ref[...].astype(F32) * y_a + gb_ref[...].astype(F32) * y_b
    out_ref[...] = x_ref[...] + jnp.dot(y.astype(BF16), wo_ref[...], preferred_element_type=F32)


def _output_projection(oa, ob, ga, gb, x2, w_out_a, w_out_b, w_out):
    m = x2.shape[0]
    tm = TM_OUT
    assert m % tm == 0
    row_blk = lambda w: pl.BlockSpec((tm, w), lambda i: (i, 0))
    return pl.pallas_call(
        _output_kernel,
        grid=(m // tm,),
        in_specs=[row_blk(DN_WIDTH), row_blk(DA_WIDTH), row_blk(D_MODEL), row_blk(D_MODEL),
                  row_blk(D_MODEL), _resident((DN_WIDTH, D_MODEL)), _resident((DA_WIDTH, D_MODEL)),
                  _resident((D_MODEL, D_MODEL))],
        out_specs=row_blk(D_MODEL),
        out_shape=jax.ShapeDtypeStruct((m, D_MODEL), F32),
        compiler_params=pltpu.CompilerParams(
            dimension_semantics=("arbitrary",), vmem_limit_bytes=VMEM_LIMIT),
        name="output_projection",
    )(oa, ob, ga, gb, x2, w_out_a.astype(BF16), w_out_b.astype(BF16), w_out.astype(BF16))


def kernel(x, norm_w, w_in, conv_w, a_log, dt_bias, dn_norm_w, q_norm_w, k_norm_w,
           lambda_q1, lambda_k1, lambda_q2, lambda_k2, da_norm_w, w_out_a, w_out_b, w_out):
    batch, seq_len, d_model = x.shape
    assert d_model == D_MODEL
    x2 = x.reshape(batch * seq_len, d_model)
    (qa, ka, va, za, gcol, grow, qb, kb, vb, zb, ga, gb) = _in_projection(
        x2, norm_w, w_in, conv_w, a_log, dt_bias, q_norm_w, k_norm_w, seq_len)
    oa = _deltanet(qa, ka, va, za, gcol, grow, dn_norm_w, batch, seq_len)
    ob = _diff_attention(qb, kb, vb, zb, lambda_q1, lambda_k1, lambda_q2, lambda_k2, da_norm_w,
                         batch, seq_len)
    out = _output_projection(oa, ob, ga, gb, x2, w_out_a, w_out_b, w_out)
    return out.reshape(batch, seq_len, d_model)
```

```python
import functools
import math

import jax
import jax.numpy as jnp
from jax import lax
from jax.experimental import pallas as pl
from jax.experimental.pallas import tpu as pltpu

F32 = jnp.float32
BF16 = jnp.bfloat16

D_MODEL = 1024
DN_HEADS = 4
DN_HEAD_DIM = 128
DN_WIDTH = DN_HEADS * DN_HEAD_DIM
CONV_WIDTH = 4
CHUNK = 64
DA_HEADS = 4
DA_QK_DIM = 64
DA_V_DIM = 2 * DA_QK_DIM
DA_WIDTH = DA_HEADS * DA_V_DIM
DA_QK_WIDTH = DA_HEADS * 2 * DA_QK_DIM
LAMBDA_INIT = 0.8 - 0.6 * math.exp(-0.3 * 0)
EPS = 1e-6
NEG_INF = -1e30
LOG2_E = math.log2(math.e)

LANES = 128
MXU_COLS = 256
SUBLANES = 8
BF16_ROWS = 16
AB_ROWS = BF16_ROWS
VMEM_LIMIT = 48 * 1024 * 1024

TM_IN = 512
TB_DN = 512
TQ_DA = 512
TK_DA = 512
HEADS_DA = 2
TM_OUT = 512

NT_DIMS = (((1,), (1,)), ((), ()))
TN_DIMS = (((0,), (0,)), ((), ()))


def _sigmoid(v):
    return 0.5 + 0.5 * jnp.tanh(0.5 * v)


def _silu(v):
    h = 0.5 * v
    return h + h * jnp.tanh(h)


def _softplus(v):
    return jnp.maximum(v, 0.0) + jnp.log1p(jnp.exp(-jnp.abs(v)))


def _resident(shape):
    zeros = (0,) * len(shape)
    return pl.BlockSpec(shape, lambda *_: zeros, pipeline_mode=pl.Buffered(1))


def _inproj_kernel(x_ref, nw_ref, wqkv_ref, convw_ref, wza_ref, wab_ref, wabt_ref,
                   alog_l_ref, dtb_l_ref, alog_s_ref, dtb_s_ref, qnw_ref, knw_ref,
                   wqb_ref, wkb_ref, wvb_ref, wzb_ref,
                   qa_ref, ka_ref, va_ref, za_ref, gcol_ref, grow_ref,
                   qb_ref, kb_ref, vb_ref, zb_ref,
                   conv_buf, *, tiles_per_seq):
    tm = x_ref.shape[0]
    half = tm // 2
    i = pl.program_id(0)
    head0 = SUBLANES - (CONV_WIDTH - 1)
    slab = 512

    @pl.when(i % tiles_per_seq == 0)
    def _():
        conv_buf[0:SUBLANES, :] = jnp.zeros((SUBLANES, 3 * DN_WIDTH), F32)

    def normed(r0):
        x = x_ref[r0:r0 + half, :]
        ms = jnp.mean(x * x, axis=-1, keepdims=True)
        return (x * lax.rsqrt(ms + EPS) * nw_ref[...]).astype(BF16)

    def tasks(h, r0):
        rows = slice(r0, r0 + half)

        def conv_qkv(s, out_ref):
            cs = slice(s * DN_WIDTH, (s + 1) * DN_WIDTH)

            res = []

            def matmul():
                res.append(jnp.dot(h, wqkv_ref[:, cs], preferred_element_type=F32))
                conv_buf[SUBLANES + r0:SUBLANES + r0 + half, cs] = res[0]

            def epilogue():
                acc = convw_ref[CONV_WIDTH - 1:CONV_WIDTH, cs] * res[0]
                for j in range(CONV_WIDTH - 1):
                    acc = acc + (convw_ref[j:j + 1, cs]
                                 * conv_buf[head0 + j + r0:head0 + j + r0 + half, cs])
                y = _silu(acc)
                if s == 2:
                    out_ref[rows, :] = y.astype(BF16)
                    return
                scale = DN_HEAD_DIM ** -0.5 if s == 0 else 1.0
                for hd in range(DN_HEADS):
                    hs = slice(hd * DN_HEAD_DIM, (hd + 1) * DN_HEAD_DIM)
                    blk = y[:, hs]
                    ss = jnp.sum(blk * blk, axis=-1, keepdims=True)
                    out_ref[rows, hs] = (blk * (lax.rsqrt(ss + EPS) * scale)).astype(BF16)
            return matmul, [epilogue]

        def plain(w_ref, c0, out_ref, act):
            res = []

            def matmul():
                res.append(jnp.dot(h, w_ref[:, c0:c0 + slab], preferred_element_type=F32))

            def chunk(n):
                def run():
                    cs = slice(n * LANES, (n + 1) * LANES)
                    out_ref[rows, c0 + n * LANES:c0 + (n + 1) * LANES] = (
                        act(res[0][:, cs]).astype(BF16))
                return run
            return matmul, [chunk(n) for n in range(slab // LANES)]

        def qk_norm(w_ref, nrm_ref, out_ref, scale):
            res = []

            def matmul():
                res.append(jnp.dot(h, w_ref[...], preferred_element_type=F32))

            def chunk(hd):
                def run():
                    lo = lax.broadcasted_iota(jnp.int32, (half, LANES), 1) < DA_QK_DIM
                    blk = res[0][:, hd * LANES:(hd + 1) * LANES]
                    sq = blk * blk
                    s_lo = jnp.sum(jnp.where(lo, sq, 0.0), axis=-1, keepdims=True)
                    s_hi = jnp.sum(jnp.where(lo, 0.0, sq), axis=-1, keepdims=True)
                    msq = jnp.where(lo, s_lo, s_hi) * (1.0 / DA_QK_DIM)
                    out_ref[rows, hd * LANES:(hd + 1) * LANES] = (
                        blk * lax.rsqrt(msq + EPS) * nrm_ref[...] * scale).astype(BF16)
                return run
            return matmul, [chunk(hd) for hd in range(DA_HEADS)]

        def decay_beta():
            def matmul():
                ab = jnp.dot(h, wab_ref[...], preferred_element_type=F32)
                lane = lax.broadcasted_iota(jnp.int32, ab.shape, 1)
                g_l = -jnp.exp(alog_l_ref[...]) * _softplus(ab + dtb_l_ref[...])
                gcol_ref[rows, :] = jnp.where(lane < DN_HEADS, g_l, _sigmoid(ab))
                abt = lax.dot_general(wabt_ref[...], h, NT_DIMS,
                                      preferred_element_type=F32)
                row = lax.broadcasted_iota(jnp.int32, abt.shape, 0)
                g_s = -jnp.exp(alog_s_ref[...]) * _softplus(abt + dtb_s_ref[...])
                grow_ref[:, rows] = jnp.where(row < DN_HEADS, g_s, _sigmoid(abt))
            return matmul, []

        ident = lambda v: v
        return [conv_qkv(0, qa_ref), plain(wvb_ref, 0, vb_ref, ident),
                conv_qkv(1, ka_ref), plain(wza_ref, 0, za_ref, _silu),
                conv_qkv(2, va_ref), plain(wzb_ref, 0, zb_ref, _silu),
                qk_norm(wqb_ref, qnw_ref, qb_ref, DA_QK_DIM ** -0.5 * LOG2_E),
                qk_norm(wkb_ref, knw_ref, kb_ref, 1.0), decay_beta()]

    def run_tasks(task_list, extra=None):
        for n, (matmul, chunks) in enumerate(task_list):
            matmul()
            for fn in chunks:
                fn()
            if extra is not None and n == 6:
                extra()

    h_second = []
    run_tasks(tasks(normed(0), 0), extra=lambda: h_second.append(normed(half)))
    run_tasks(tasks(h_second[0], half))
    conv_buf[0:SUBLANES, :] = conv_buf[tm:tm + SUBLANES, :]


def _in_projection(x2, norm_w, w_in, conv_w, a_log, dt_bias, q_norm_w, k_norm_w, seq_len):
    m = x2.shape[0]
    tm = TM_IN
    assert m % tm == 0 and seq_len % tm == 0
    o = 0
    cols = {}
    for name, width in (("qkv", 3 * DN_WIDTH), ("za", DN_WIDTH), ("a", DN_HEADS), ("b", DN_HEADS),
                        ("qb", DA_QK_WIDTH), ("kb", DA_QK_WIDTH), ("vb", DA_WIDTH), ("zb", DA_WIDTH),
                        ("ga", D_MODEL), ("gb", D_MODEL)):
        cols[name] = w_in[:, o:o + width]
        o += width
    assert o == w_in.shape[1]
    wb = {k: v.astype(BF16) for k, v in cols.items()}
    w_ab = jnp.concatenate([cols["a"], cols["b"]], axis=1)
    w_ab_l = jnp.pad(w_ab, ((0, 0), (0, LANES - 2 * DN_HEADS))).astype(BF16)
    w_ab_s = jnp.pad(w_ab.T, ((0, AB_ROWS - 2 * DN_HEADS), (0, 0))).astype(BF16)
    alog_l = jnp.pad(a_log.astype(F32), (0, LANES - DN_HEADS)).reshape(1, LANES)
    dtb_l = jnp.pad(dt_bias.astype(F32), (0, LANES - DN_HEADS)).reshape(1, LANES)
    alog_s = jnp.pad(a_log.astype(F32), (0, AB_ROWS - DN_HEADS)).reshape(AB_ROWS, 1)
    dtb_s = jnp.pad(dt_bias.astype(F32), (0, AB_ROWS - DN_HEADS)).reshape(AB_ROWS, 1)
    qnw = jnp.tile(q_norm_w.astype(F32), 2).reshape(1, LANES)
    knw = jnp.tile(k_norm_w.astype(F32), 2).reshape(1, LANES)

    row_blk = lambda w: pl.BlockSpec((tm, w), lambda i: (i, 0))
    in_specs = [
        row_blk(D_MODEL), _resident((1, D_MODEL)),
        _resident((D_MODEL, 3 * DN_WIDTH)), _resident((CONV_WIDTH, 3 * DN_WIDTH)),
        _resident((D_MODEL, DN_WIDTH)), _resident((D_MODEL, LANES)), _resident((AB_ROWS, D_MODEL)),
        _resident((1, LANES)), _resident((1, LANES)), _resident((AB_ROWS, 1)), _resident((AB_ROWS, 1)),
        _resident((1, LANES)), _resident((1, LANES)),
        _resident((D_MODEL, DA_QK_WIDTH)), _resident((D_MODEL, DA_QK_WIDTH)),
        _resident((D_MODEL, DA_WIDTH)), _resident((D_MODEL, DA_WIDTH)),
    ]
    out_shapes = [jax.ShapeDtypeStruct((m, DN_WIDTH), BF16)] * 4 + [
        jax.ShapeDtypeStruct((m, LANES), F32), jax.ShapeDtypeStruct((AB_ROWS, m), F32)] + [
        jax.ShapeDtypeStruct((m, DA_WIDTH), BF16)] * 4
    out_specs = [row_blk(DN_WIDTH)] * 4 + [
        row_blk(LANES), pl.BlockSpec((AB_ROWS, tm), lambda i: (0, i))] + [
        row_blk(DA_WIDTH)] * 4
    outs = pl.pallas_call(
        functools.partial(_inproj_kernel, tiles_per_seq=seq_len // tm),
        grid=(m // tm,),
        in_specs=in_specs,
        out_specs=out_specs,
        out_shape=out_shapes,
        scratch_shapes=[pltpu.VMEM((tm + SUBLANES, 3 * DN_WIDTH), F32)],
        compiler_params=pltpu.CompilerParams(
            dimension_semantics=("arbitrary",), vmem_limit_bytes=VMEM_LIMIT),
        name="in_projection",
    )(x2, norm_w.astype(F32).reshape(1, D_MODEL), wb["qkv"], conv_w.astype(F32), wb["za"],
      w_ab_l, w_ab_s, alog_l, dtb_l, alog_s, dtb_s, qnw, knw,
      wb["qb"], wb["kb"], wb["vb"], wb["zb"])
    return outs, (wb["ga"], wb["gb"])


def _bmm(a, b):
    return jnp.einsum('nij,njk->nik', a, b, preferred_element_type=F32)


def _bmm_nt(a, b):
    return jnp.einsum('nid,njd->nij', a, b, preferred_element_type=F32)


def _split_bf16(x, parts):
    out = []
    for _ in range(parts - 1):
        hi = x.astype(BF16)
        out.append(hi)
        x = x - hi.astype(F32)
    out.append(x.astype(BF16))
    return out


def _deltanet_kernel(q_ref, k_ref, v_ref, z_ref, gcol_ref, grow_ref, nw_ref, o_ref,
                     state_ref, oraw_ref, lhs_ref, attn_ref, u_ref, kdec_ref, sdec_ref, *, nt):
    tb = q_ref.shape[0]
    c = CHUNK
    nc = tb // c
    nh = DN_HEADS
    t = pl.program_id(1)

    ri = lax.broadcasted_iota(jnp.int32, (c, c), 0)
    ci = lax.broadcasted_iota(jnp.int32, (c, c), 1)
    lower = ri >= ci
    strict = ri > ci
    eye = (ri == ci).astype(F32)
    tri = lower.astype(BF16)
    tri_t = (ri <= ci).astype(BF16)

    def heads_major(ref):
        return jnp.concatenate(
            [ref[:, hd * DN_HEAD_DIM:(hd + 1) * DN_HEAD_DIM].reshape(nc, c, DN_HEAD_DIM)
             for hd in range(nh)], axis=0)

    def phase1_stages():
        e = {}

        def decays():
            gb_col = gcol_ref[...].reshape(nc, c, LANES)
            tri_b = jnp.broadcast_to(tri, (nc, c, c))
            gc_col = sum(_bmm(tri_b, part) for part in _split_bf16(gb_col, 3))
            gb_row = grow_ref[...]
            gc_row = sum(jnp.dot(part, tri_t, preferred_element_type=F32)
                         for part in _split_bf16(gb_row.reshape(nc * AB_ROWS, c), 3)
                         ).reshape(nc, AB_ROWS, c)

            def col_form(src, lane):
                return jnp.concatenate(
                    [jnp.broadcast_to(src[:, :, lane + hd:lane + hd + 1], (nc, c, LANES))
                     for hd in range(nh)], axis=0)

            def row_form(src, row):
                return jnp.concatenate(
                    [src[:, row + hd:row + hd + 1, :] for hd in range(nh)], axis=0)

            e["g_c"] = col_form(gc_col, 0)
            e["beta_c"] = col_form(gb_col, nh)
            e["g_r"] = row_form(gc_row, 0)
            e["beta_r"] = row_form(gb_row, nh)
            e["g_last"] = e["g_c"][:, c - 1:c, :]
            diff = e["g_c"][:, :, :c] - e["g_r"]
            e["decay"] = jnp.where(lower, jnp.exp(jnp.where(lower, diff, 0.0)), 0.0)

        def gram_kk():
            e["q"], e["k"], e["v"] = heads_major(q_ref), heads_major(k_ref), heads_major(v_ref)
            l_strict = jnp.where(
                strict, _bmm_nt(e["k"], e["k"]) * e["decay"] * e["beta_c"][:, :, :c], 0.0)
            e["inv"] = eye - l_strict
            e["power"] = l_strict.astype(BF16)

        def gram_qk():
            e["attn"] = jnp.where(lower, _bmm_nt(e["q"], e["k"]) * e["decay"], 0.0).astype(BF16)

        def next_power():
            e["power"] = _bmm(e["power"], e["power"]).astype(BF16)

        def apply_power():
            e["inv"] = e["inv"] + _bmm(e["inv"].astype(BF16), e["power"])

        def solve_u():
            e["t_beta"] = e["inv"] * e["beta_r"]
            e["u"] = _bmm(e["t_beta"].astype(BF16), e["v"])

        def solve_w():
            e["w"] = _bmm((e["t_beta"] * jnp.exp(e["g_r"])).astype(BF16), e["k"]).astype(BF16)

        def commit():
            q_decay = (e["q"].astype(F32) * jnp.exp(e["g_c"])).astype(BF16)
            lhs_ref[...] = jnp.concatenate([e["w"], q_decay], axis=1)
            attn_ref[...] = e["attn"]
            u_ref[...] = e["u"]
            kdec_ref[...] = (e["k"].astype(F32) * jnp.exp(e["g_last"] - e["g_c"])).astype(BF16)
            sdec_ref[...] = jnp.exp(e["g_last"])

        levels = [next_power, apply_power] * (c.bit_length() - 2)
        return [decays, gram_kk, gram_qk] + levels + [solve_u, solve_w], commit

    def phase2_stages():
        def chunk_steps(n):
            idx = [hd * nc + n for hd in range(nh)]
            e = {}

            def through_state():
                e["s_old"] = [state_ref[hd] for hd in range(nh)]
                e["ws_qs"] = [jnp.dot(lhs_ref[b], e["s_old"][hd].astype(BF16),
                                      preferred_element_type=F32) for hd, b in enumerate(idx)]

            def update():
                v_new = [(u_ref[b] - e["ws_qs"][hd][:c]).astype(BF16) for hd, b in enumerate(idx)]
                for hd, b in enumerate(idx):
                    oraw_ref[n * c:(n + 1) * c, hd * DN_HEAD_DIM:(hd + 1) * DN_HEAD_DIM] = (
                        e["ws_qs"][hd][c:]
                        + jnp.dot(attn_ref[b], v_new[hd], preferred_element_type=F32))
                for hd, b in enumerate(idx):
                    state_ref[hd] = e["s_old"][hd] * sdec_ref[b] + lax.dot_general(
                        kdec_ref[b], v_new[hd], TN_DIMS, preferred_element_type=F32)
            return [through_state, update]

        def gated_norm():
            for hd in range(nh):
                hs = slice(hd * DN_HEAD_DIM, (hd + 1) * DN_HEAD_DIM)
                o = oraw_ref[:, hs]
                ms = jnp.mean(o * o, axis=-1, keepdims=True)
                o = o * lax.rsqrt(ms + EPS) * nw_ref[...]
                o_ref[:, hs] = (o * z_ref[:, hs].astype(F32)).astype(BF16)

        return [fn for n in range(nc) for fn in chunk_steps(n)] + [gated_norm]

    @pl.when(t == 0)
    def _():
        state_ref[...] = jnp.zeros(state_ref.shape, F32)
        stages, commit = phase1_stages()
        for fn in stages:
            fn()
        commit()

    @pl.when((t > 0) & (t < nt))
    def _():
        stages, commit = phase1_stages()
        others = phase2_stages()
        done = 0
        for n, fn in enumerate(stages):
            fn()
            upto = -(-(n + 1) * len(others) // len(stages))
            for other in others[done:upto]:
                other()
            done = upto
        commit()

    @pl.when(t == nt)
    def _():
        for fn in phase2_stages():
            fn()


def _deltanet(qa, ka, va, za, gcol, grow, dn_norm_w, batch, seq_len):
    tb = TB_DN
    assert seq_len % tb == 0 and tb % CHUNK == 0
    nt = seq_len // tb
    nc = tb // CHUNK
    nb = DN_HEADS * nc
    grow3 = grow.reshape(AB_ROWS, -1, CHUNK).transpose(1, 0, 2)
    cur = lambda w: pl.BlockSpec((tb, w), lambda b, t: (b * nt + jnp.minimum(t, nt - 1), 0))
    prev = lambda w: pl.BlockSpec((tb, w), lambda b, t: (b * nt + jnp.maximum(t - 1, 0), 0))
    return pl.pallas_call(
        functools.partial(_deltanet_kernel, nt=nt),
        grid=(batch, nt + 1),
        in_specs=[cur(DN_WIDTH)] * 3 + [
            prev(DN_WIDTH), cur(LANES),
            pl.BlockSpec((nc, AB_ROWS, CHUNK),
                         lambda b, t: (b * nt + jnp.minimum(t, nt - 1), 0, 0)),
            _resident((1, DN_HEAD_DIM))],
        out_specs=prev(DN_WIDTH),
        out_shape=jax.ShapeDtypeStruct((batch * seq_len, DN_WIDTH), BF16),
        scratch_shapes=[pltpu.VMEM((DN_HEADS, DN_HEAD_DIM, DN_HEAD_DIM), F32),
                        pltpu.VMEM((tb, DN_WIDTH), F32),
                        pltpu.VMEM((nb, 2 * CHUNK, DN_HEAD_DIM), BF16),
                        pltpu.VMEM((nb, CHUNK, CHUNK), BF16),
                        pltpu.VMEM((nb, CHUNK, DN_HEAD_DIM), F32),
                        pltpu.VMEM((nb, CHUNK, DN_HEAD_DIM), BF16),
                        pltpu.VMEM((nb, 1, DN_HEAD_DIM), F32)],
        compiler_params=pltpu.CompilerParams(
            dimension_semantics=("arbitrary", "arbitrary"), vmem_limit_bytes=VMEM_LIMIT),
        name="gated_deltanet",
    )(qa, ka, va, za, gcol, grow3, dn_norm_w.astype(F32).reshape(1, DN_HEAD_DIM))


def _diffattn_kernel(lq1_ref, lk1_ref, lq2_ref, lk2_ref, q_ref, k_ref, v_ref, z_ref, nw_ref,
                     o_ref, m_ref, l_ref, acc_ref):
    tq = q_ref.shape[0]
    tk = TK_DA
    qi = pl.program_id(2)
    chains = [(hd, comp) for hd in range(HEADS_DA) for comp in range(2)]

    lane = lax.broadcasted_iota(jnp.int32, (1, LANES), 1)
    first_half = lane < DA_QK_DIM
    q_chain = []
    for hd in range(HEADS_DA):
        q = q_ref[:, hd * LANES:(hd + 1) * LANES]
        zero = jnp.zeros_like(q)
        q_chain += [jnp.where(first_half, q, zero), jnp.where(first_half, zero, q)]

    m_ref[...] = jnp.full(m_ref.shape, NEG_INF, F32)
    l_ref[...] = jnp.zeros(l_ref.shape, F32)
    acc_ref[...] = jnp.zeros(acc_ref.shape, F32)

    def attend(j, masked):
        start = pl.multiple_of(j * tk, tk)
        kblk = [k_ref[pl.ds(start, tk), hd * LANES:(hd + 1) * LANES] for hd in range(HEADS_DA)]
        ones = jnp.ones((tk, LANES), BF16)
        vblk = [jnp.concatenate([v_ref[pl.ds(start, tk), hd * LANES:(hd + 1) * LANES], ones], axis=1)
                for hd in range(HEADS_DA)]
        scores = [lax.dot_general(q_chain[c], kblk[hd], NT_DIMS, preferred_element_type=F32)
                  for c, (hd, _) in enumerate(chains)]
        if masked:
            rq = lax.broadcasted_iota(jnp.int32, (tq, tk), 0)
            ck = lax.broadcasted_iota(jnp.int32, (tq, tk), 1)
            scores = [jnp.where(ck <= rq, s, NEG_INF) for s in scores]
        probs = []
        for c, s in enumerate(scores):
            m_prev = m_ref[c]
            m_new = jnp.maximum(m_prev, jnp.max(s, axis=-1, keepdims=True))
            alpha = jnp.exp2(m_prev - m_new)
            probs.append(jnp.exp2(
                (s - jnp.concatenate([m_new] * (tk // LANES), axis=1)).astype(BF16)))
            l_ref[c] = alpha * l_ref[c]
            acc_ref[c] = alpha * acc_ref[c]
            m_ref[c] = m_new
        for c, (hd, _) in enumerate(chains):
            pv = jnp.dot(probs[c], vblk[hd], preferred_element_type=F32)
            acc_ref[c] += pv[:, :DA_V_DIM]
            l_ref[c] += pv[:, DA_V_DIM:]

    def body(j, carry):
        attend(j, False)
        return carry

    lax.fori_loop(0, qi, body, 0)
    attend(qi, True)

    lam = (jnp.exp(jnp.sum(lq1_ref[...] * lk1_ref[...], axis=-1, keepdims=True))
           - jnp.exp(jnp.sum(lq2_ref[...] * lk2_ref[...], axis=-1, keepdims=True))
           + LAMBDA_INIT)
    for hd in range(HEADS_DA):
        hs = slice(hd * LANES, (hd + 1) * LANES)
        o = acc_ref[2 * hd] / l_ref[2 * hd] - lam * (acc_ref[2 * hd + 1] / l_ref[2 * hd + 1])
        ms = jnp.mean(o * o, axis=-1, keepdims=True)
        o = o * lax.rsqrt(ms + EPS) * nw_ref[...] * (1.0 - LAMBDA_INIT)
        o_ref[:, hs] = (o * z_ref[:, hs].astype(F32)).astype(BF16)


def _diff_attention(qb, kb, vb, zb, lambda_q1, lambda_k1, lambda_q2, lambda_k2, da_norm_w,
                    batch, seq_len):
    tq = TQ_DA
    assert tq == TK_DA and seq_len % tq == 0 and DA_HEADS % HEADS_DA == 0
    nq = seq_len // tq
    width = HEADS_DA * LANES
    lam_vec = lambda v: v.astype(F32).reshape(1, DA_QK_DIM)
    q_blk = pl.BlockSpec((tq, width), lambda b, h, i: (b * nq + i, h))
    kv_blk = pl.BlockSpec((seq_len, width), lambda b, h, i: (b, h))
    stat = pltpu.VMEM((2 * HEADS_DA, tq, LANES), F32)
    return pl.pallas_call(
        _diffattn_kernel,
        grid=(batch, DA_HEADS // HEADS_DA, nq),
        in_specs=[_resident((1, DA_QK_DIM))] * 4 + [q_blk, kv_blk, kv_blk, q_blk,
                                                    _resident((1, DA_V_DIM))],
        out_specs=q_blk,
        out_shape=jax.ShapeDtypeStruct((batch * seq_len, DA_WIDTH), BF16),
        scratch_shapes=[stat, stat, pltpu.VMEM((2 * HEADS_DA, tq, DA_V_DIM), F32)],
        compiler_params=pltpu.CompilerParams(
            dimension_semantics=("arbitrary", "arbitrary", "arbitrary"),
            vmem_limit_bytes=VMEM_LIMIT),
        name="diff_attention",
    )(lam_vec(lambda_q1), lam_vec(lambda_k1), lam_vec(lambda_q2), lam_vec(lambda_k2),
      qb, kb, vb, zb, da_norm_w.astype(F32).reshape(1, DA_V_DIM))


def _output_kernel(oa_ref, ob_ref, x_ref, nw_ref, wga_ref, wgb_ref, woa_ref, wob_ref, wo_ref,
                   out_ref):
    x = x_ref[...]
    ms = jnp.mean(x * x, axis=-1, keepdims=True)
    h = (x * lax.rsqrt(ms + EPS) * nw_ref[...]).astype(BF16)
    gate_a = _sigmoid(jnp.dot(h, wga_ref[...], preferred_element_type=F32))
    gate_b = _sigmoid(jnp.dot(h, wgb_ref[...], preferred_element_type=F32))
    y_a = jnp.dot(oa_ref[...], woa_ref[...], preferred_element_type=F32)
    y_b = jnp.dot(ob_ref[...], wob_ref[...], preferred_element_type=F32)
    y = gate_a * y_a + gate_b * y_b
    out_ref[...] = x + jnp.dot(y.astype(BF16), wo_ref[...], preferred_element_type=F32)


def _output_projection(oa, ob, x2, norm_w, w_gates, w_out_a, w_out_b, w_out):
    m = x2.shape[0]
    tm = TM_OUT
    assert m % tm == 0
    row_blk = lambda w: pl.BlockSpec((tm, w), lambda i: (i, 0))
    return pl.pallas_call(
        _output_kernel,
        grid=(m // tm,),
        in_specs=[row_blk(DN_WIDTH), row_blk(DA_WIDTH), row_blk(D_MODEL), _resident((1, D_MODEL)),
                  _resident((D_MODEL, D_MODEL)), _resident((D_MODEL, D_MODEL)),
                  _resident((DN_WIDTH, D_MODEL)), _resident((DA_WIDTH, D_MODEL)),
                  _resident((D_MODEL, D_MODEL))],
        out_specs=row_blk(D_MODEL),
        out_shape=jax.ShapeDtypeStruct((m, D_MODEL), F32),
        compiler_params=pltpu.CompilerParams(
            dimension_semantics=("arbitrary",), vmem_limit_bytes=VMEM_LIMIT),
        name="output_projection",
    )(oa, ob, x2, norm_w.astype(F32).reshape(1, D_MODEL), *w_gates,
      w_out_a.astype(BF16), w_out_b.astype(BF16), w_out.astype(BF16))


def kernel(x, norm_w, w_in, conv_w, a_log, dt_bias, dn_norm_w, q_norm_w, k_norm_w,
           lambda_q1, lambda_k1, lambda_q2, lambda_k2, da_norm_w, w_out_a, w_out_b, w_out):
    batch, seq_len, d_model = x.shape
    assert d_model == D_MODEL
    x2 = x.reshape(batch * seq_len, d_model)
    (qa, ka, va, za, gcol, grow, qb, kb, vb, zb), w_gates = _in_projection(
        x2, norm_w, w_in, conv_w, a_log, dt_bias, q_norm_w, k_norm_w, seq_len)
    oa = _deltanet(qa, ka, va, za, gcol, grow, dn_norm_w, batch, seq_len)
    ob = _diff_attention(qb, kb, vb, zb, lambda_q1, lambda_k1, lambda_q2, lambda_k2, da_norm_w,
                         batch, seq_len)
    out = _output_projection(oa, ob, x2, norm_w, w_gates, w_out_a, w_out_b, w_out)
    return out.reshape(batch, seq_len, d_model)
```

```python
import functools
import math

import jax
import jax.numpy as jnp
from jax import lax
from jax.experimental import pallas as pl
from jax.experimental.pallas import tpu as pltpu

F32 = jnp.float32
BF16 = jnp.bfloat16

D_MODEL = 1024
DN_HEADS = 4
DN_HEAD_DIM = 128
DN_WIDTH = DN_HEADS * DN_HEAD_DIM
CONV_WIDTH = 4
CHUNK = 64
DA_HEADS = 4
DA_QK_DIM = 64
DA_V_DIM = 2 * DA_QK_DIM
DA_WIDTH = DA_HEADS * DA_V_DIM
DA_QK_WIDTH = DA_HEADS * 2 * DA_QK_DIM
LAMBDA_INIT = 0.8 - 0.6 * math.exp(-0.3 * 0)
EPS = 1e-6
NEG_INF = -1e30
LOG2_E = math.log2(math.e)

LANES = 128
MXU_COLS = 256
SUBLANES = 8
BF16_ROWS = 16
AB_ROWS = BF16_ROWS
VMEM_LIMIT = 48 * 1024 * 1024

TM_IN = 512
TB_DN = 512
TQ_DA = 512
TK_DA = 512
HEADS_DA = 2
TM_OUT = 512

NT_DIMS = (((1,), (1,)), ((), ()))
TN_DIMS = (((0,), (0,)), ((), ()))


def _sigmoid(v):
    return 0.5 + 0.5 * jnp.tanh(0.5 * v)


def _silu(v):
    h = 0.5 * v
    return h + h * jnp.tanh(h)


def _softplus(v):
    return jnp.maximum(v, 0.0) + jnp.log1p(jnp.exp(-jnp.abs(v)))


def _resident(shape):
    zeros = (0,) * len(shape)
    return pl.BlockSpec(shape, lambda *_: zeros, pipeline_mode=pl.Buffered(1))


def _inproj_kernel(x_ref, nw_ref, wqkv_ref, convw_ref, wza_ref, wab_ref, wabt_ref,
                   alog_l_ref, dtb_l_ref, alog_s_ref, dtb_s_ref, qnw_ref, knw_ref,
                   wqb_ref, wkb_ref, wvb_ref, wzb_ref,
                   qa_ref, ka_ref, va_ref, za_ref, gcol_ref, grow_ref,
                   qb_ref, kb_ref, vb_ref, zb_ref,
                   conv_buf, *, tiles_per_seq):
    tm = x_ref.shape[0]
    half = tm // 2
    i = pl.program_id(0)
    head0 = SUBLANES - (CONV_WIDTH - 1)
    slab = 512

    @pl.when(i % tiles_per_seq == 0)
    def _():
        conv_buf[0:SUBLANES, :] = jnp.zeros((SUBLANES, 3 * DN_WIDTH), F32)

    def normed(r0):
        x = x_ref[r0:r0 + half, :]
        ms = jnp.mean(x * x, axis=-1, keepdims=True)
        return (x * lax.rsqrt(ms + EPS) * nw_ref[...]).astype(BF16)

    def tasks(h, r0):
        rows = slice(r0, r0 + half)

        def conv_qkv(s, out_ref):
            cs = slice(s * DN_WIDTH, (s + 1) * DN_WIDTH)

            res = []

            def matmul():
                res.append(jnp.dot(h, wqkv_ref[:, cs], preferred_element_type=F32))
                conv_buf[SUBLANES + r0:SUBLANES + r0 + half, cs] = res[0]

            def epilogue():
                acc = convw_ref[CONV_WIDTH - 1:CONV_WIDTH, cs] * res[0]
                for j in range(CONV_WIDTH - 1):
                    acc = acc + (convw_ref[j:j + 1, cs]
                                 * conv_buf[head0 + j + r0:head0 + j + r0 + half, cs])
                y = _silu(acc)
                if s == 2:
                    out_ref[rows, :] = y.astype(BF16)
                    return
                scale = DN_HEAD_DIM ** -0.5 if s == 0 else 1.0
                for hd in range(DN_HEADS):
                    hs = slice(hd * DN_HEAD_DIM, (hd + 1) * DN_HEAD_DIM)
                    blk = y[:, hs]
                    ss = jnp.sum(blk * blk, axis=-1, keepdims=True)
                    out_ref[rows, hs] = (blk * (lax.rsqrt(ss + EPS) * scale)).astype(BF16)
            return matmul, [epilogue]

        def plain(w_ref, c0, out_ref, act):
            res = []

            def matmul():
                res.append(jnp.dot(h, w_ref[:, c0:c0 + slab], preferred_element_type=F32))

            def chunk(n):
                def run():
                    cs = slice(n * LANES, (n + 1) * LANES)
                    out_ref[rows, c0 + n * LANES:c0 + (n + 1) * LANES] = (
                        act(res[0][:, cs]).astype(BF16))
                return run
            return matmul, [chunk(n) for n in range(slab // LANES)]

        def qk_norm(w_ref, nrm_ref, out_ref, scale):
            res = []

            def matmul():
                res.append(jnp.dot(h, w_ref[...], preferred_element_type=F32))

            def chunk(hd):
                def run():
                    lo = lax.broadcasted_iota(jnp.int32, (half, LANES), 1) < DA_QK_DIM
                    blk = res[0][:, hd * LANES:(hd + 1) * LANES]
                    sq = blk * blk
                    s_lo = jnp.sum(jnp.where(lo, sq, 0.0), axis=-1, keepdims=True)
                    s_hi = jnp.sum(jnp.where(lo, 0.0, sq), axis=-1, keepdims=True)
                    msq = jnp.where(lo, s_lo, s_hi) * (1.0 / DA_QK_DIM)
                    out_ref[rows, hd * LANES:(hd + 1) * LANES] = (
                        blk * lax.rsqrt(msq + EPS) * nrm_ref[...] * scale).astype(BF16)
                return run
            return matmul, [chunk(hd) for hd in range(DA_HEADS)]

        def decay_beta():
            def matmul():
                ab = jnp.dot(h, wab_ref[...], preferred_element_type=F32)
                lane = lax.broadcasted_iota(jnp.int32, ab.shape, 1)
                g_l = -jnp.exp(alog_l_ref[...]) * _softplus(ab + dtb_l_ref[...])
                gcol_ref[rows, :] = jnp.where(lane < DN_HEADS, g_l, _sigmoid(ab))
                abt = lax.dot_general(wabt_ref[...], h, NT_DIMS,
                                      preferred_element_type=F32)
                row = lax.broadcasted_iota(jnp.int32, abt.shape, 0)
                g_s = -jnp.exp(alog_s_ref[...]) * _softplus(abt + dtb_s_ref[...])
                grow_ref[:, rows] = jnp.where(row < DN_HEADS, g_s, _sigmoid(abt))
            return matmul, []

        ident = lambda v: v
        return [conv_qkv(0, qa_ref), plain(wvb_ref, 0, vb_ref, ident),
                conv_qkv(1, ka_ref), plain(wza_ref, 0, za_ref, _silu),
                conv_qkv(2, va_ref), plain(wzb_ref, 0, zb_ref, _silu),
                qk_norm(wqb_ref, qnw_ref, qb_ref, DA_QK_DIM ** -0.5 * LOG2_E),
                qk_norm(wkb_ref, knw_ref, kb_ref, 1.0), decay_beta()]

    def run_tasks(task_list, extra=None):
        for n, (matmul, chunks) in enumerate(task_list):
            matmul()
            for fn in chunks:
                fn()
            if extra is not None and n == 6:
                extra()

    h_second = []
    run_tasks(tasks(normed(0), 0), extra=lambda: h_second.append(normed(half)))
    run_tasks(tasks(h_second[0], half))
    conv_buf[0:SUBLANES, :] = conv_buf[tm:tm + SUBLANES, :]


def _in_projection(x2, norm_w, w_in, conv_w, a_log, dt_bias, q_norm_w, k_norm_w, seq_len):
    m = x2.shape[0]
    tm = TM_IN
    assert m % tm == 0 and seq_len % tm == 0
    o = 0
    cols = {}
    for name, width in (("qkv", 3 * DN_WIDTH), ("za", DN_WIDTH), ("a", DN_HEADS), ("b", DN_HEADS),
                        ("qb", DA_QK_WIDTH), ("kb", DA_QK_WIDTH), ("vb", DA_WIDTH), ("zb", DA_WIDTH),
                        ("ga", D_MODEL), ("gb", D_MODEL)):
        cols[name] = w_in[:, o:o + width]
        o += width
    assert o == w_in.shape[1]
    wb = {k: v.astype(BF16) for k, v in cols.items()}
    w_ab = jnp.concatenate([cols["a"], cols["b"]], axis=1)
    w_ab_l = jnp.pad(w_ab, ((0, 0), (0, LANES - 2 * DN_HEADS))).astype(BF16)
    w_ab_s = jnp.pad(w_ab.T, ((0, AB_ROWS - 2 * DN_HEADS), (0, 0))).astype(BF16)
    alog_l = jnp.pad(a_log.astype(F32), (0, LANES - DN_HEADS)).reshape(1, LANES)
    dtb_l = jnp.pad(dt_bias.astype(F32), (0, LANES - DN_HEADS)).reshape(1, LANES)
    alog_s = jnp.pad(a_log.astype(F32), (0, AB_ROWS - DN_HEADS)).reshape(AB_ROWS, 1)
    dtb_s = jnp.pad(dt_bias.astype(F32), (0, AB_ROWS - DN_HEADS)).reshape(AB_ROWS, 1)
    qnw = jnp.tile(q_norm_w.astype(F32), 2).reshape(1, LANES)
    knw = jnp.tile(k_norm_w.astype(F32), 2).reshape(1, LANES)

    row_blk = lambda w: pl.BlockSpec((tm, w), lambda i: (i, 0))
    in_specs = [
        row_blk(D_MODEL), _resident((1, D_MODEL)),
        _resident((D_MODEL, 3 * DN_WIDTH)), _resident((CONV_WIDTH, 3 * DN_WIDTH)),
        _resident((D_MODEL, DN_WIDTH)), _resident((D_MODEL, LANES)), _resident((AB_ROWS, D_MODEL)),
        _resident((1, LANES)), _resident((1, LANES)), _resident((AB_ROWS, 1)), _resident((AB_ROWS, 1)),
        _resident((1, LANES)), _resident((1, LANES)),
        _resident((D_MODEL, DA_QK_WIDTH)), _resident((D_MODEL, DA_QK_WIDTH)),
        _resident((D_MODEL, DA_WIDTH)), _resident((D_MODEL, DA_WIDTH)),
    ]
    out_shapes = [jax.ShapeDtypeStruct((m, DN_WIDTH), BF16)] * 4 + [
        jax.ShapeDtypeStruct((m, LANES), F32), jax.ShapeDtypeStruct((AB_ROWS, m), F32)] + [
        jax.ShapeDtypeStruct((m, DA_WIDTH), BF16)] * 4
    out_specs = [row_blk(DN_WIDTH)] * 4 + [
        row_blk(LANES), pl.BlockSpec((AB_ROWS, tm), lambda i: (0, i))] + [
        row_blk(DA_WIDTH)] * 4
    outs = pl.pallas_call(
        functools.partial(_inproj_kernel, tiles_per_seq=seq_len // tm),
        grid=(m // tm,),
        in_specs=in_specs,
        out_specs=out_specs,
        out_shape=out_shapes,
        scratch_shapes=[pltpu.VMEM((tm + SUBLANES, 3 * DN_WIDTH), F32)],
        compiler_params=pltpu.CompilerParams(
            dimension_semantics=("arbitrary",), vmem_limit_bytes=VMEM_LIMIT),
        name="in_projection",
    )(x2, norm_w.astype(F32).reshape(1, D_MODEL), wb["qkv"], conv_w.astype(F32), wb["za"],
      w_ab_l, w_ab_s, alog_l, dtb_l, alog_s, dtb_s, qnw, knw,
      wb["qb"], wb["kb"], wb["vb"], wb["zb"])
    return outs, (wb["ga"], wb["gb"])


def _bmm(a, b):
    return jnp.einsum('nij,njk->nik', a, b, preferred_element_type=F32)


def _bmm_nt(a, b):
    return jnp.einsum('nid,njd->nij', a, b, preferred_element_type=F32)


def _split_bf16(x, parts):
    out = []
    for _ in range(parts - 1):
        hi = x.astype(BF16)
        out.append(hi)
        x = x - hi.astype(F32)
    out.append(x.astype(BF16))
    return out


def _deltanet_kernel(q_ref, k_ref, v_ref, z_ref, gcol_ref, grow_ref, nw_ref, o_ref,
                     state_ref, oraw_ref, lhs_ref, attn_ref, u_ref, kdec_ref, sdec_ref, *, nt):
    tb = q_ref.shape[0]
    c = CHUNK
    nc = tb // c
    nh = DN_HEADS
    t = pl.program_id(1)

    ri = lax.broadcasted_iota(jnp.int32, (c, c), 0)
    ci = lax.broadcasted_iota(jnp.int32, (c, c), 1)
    lower = ri >= ci
    strict = ri > ci
    eye = (ri == ci).astype(F32)
    tri = lower.astype(BF16)
    tri_t = (ri <= ci).astype(BF16)

    def heads_major(ref):
        return jnp.concatenate(
            [ref[:, hd * DN_HEAD_DIM:(hd + 1) * DN_HEAD_DIM].reshape(nc, c, DN_HEAD_DIM)
             for hd in range(nh)], axis=0)

    def phase1_stages():
        e = {}

        def decays():
            gb_col = gcol_ref[...].reshape(nc, c, LANES)
            tri_b = jnp.broadcast_to(tri, (nc, c, c))
            gc_col = sum(_bmm(tri_b, part) for part in _split_bf16(gb_col, 3))
            gb_row = grow_ref[...]
            gc_row = sum(jnp.dot(part, tri_t, preferred_element_type=F32)
                         for part in _split_bf16(gb_row.reshape(nc * AB_ROWS, c), 3)
                         ).reshape(nc, AB_ROWS, c)

            def col_form(src, lane):
                return jnp.concatenate(
                    [jnp.broadcast_to(src[:, :, lane + hd:lane + hd + 1], (nc, c, LANES))
                     for hd in range(nh)], axis=0)

            def row_form(src, row):
                return jnp.concatenate(
                    [src[:, row + hd:row + hd + 1, :] for hd in range(nh)], axis=0)

            e["g_c"] = col_form(gc_col, 0)
            e["beta_c"] = col_form(gb_col, nh)
            e["g_r"] = row_form(gc_row, 0)
            e["beta_r"] = row_form(gb_row, nh)
            e["g_last"] = e["g_c"][:, c - 1:c, :]
            diff = e["g_c"][:, :, :c] - e["g_r"]
            e["decay"] = jnp.where(lower, jnp.exp(jnp.where(lower, diff, 0.0)), 0.0)

        def gram_kk():
            e["q"], e["k"], e["v"] = heads_major(q_ref), heads_major(k_ref), heads_major(v_ref)
            l_strict = jnp.where(
                strict, _bmm_nt(e["k"], e["k"]) * e["decay"] * e["beta_c"][:, :, :c], 0.0)
            e["inv"] = eye - l_strict
            e["power"] = l_strict.astype(BF16)

        def gram_qk():
            e["attn"] = jnp.where(lower, _bmm_nt(e["q"], e["k"]) * e["decay"], 0.0).astype(BF16)

        def next_power():
            e["power"] = _bmm(e["power"], e["power"]).astype(BF16)

        def apply_power():
            e["inv"] = e["inv"] + _bmm(e["inv"].astype(BF16), e["power"])

        def solve_u():
            e["t_beta"] = e["inv"] * e["beta_r"]
            e["u"] = _bmm(e["t_beta"].astype(BF16), e["v"])

        def solve_w():
            e["w"] = _bmm((e["t_beta"] * jnp.exp(e["g_r"])).astype(BF16), e["k"]).astype(BF16)

        def commit():
            q_decay = (e["q"].astype(F32) * jnp.exp(e["g_c"])).astype(BF16)
            lhs_ref[...] = jnp.concatenate([e["w"], q_decay], axis=1)
            attn_ref[...] = e["attn"]
            u_ref[...] = e["u"]
            kdec_ref[...] = (e["k"].astype(F32) * jnp.exp(e["g_last"] - e["g_c"])).astype(BF16)
            sdec_ref[...] = jnp.exp(e["g_last"])

        levels = [next_power, apply_power] * (c.bit_length() - 2)
        return [decays, gram_kk, gram_qk] + levels + [solve_u, solve_w], commit

    def phase2_stages():
        def chunk_steps(n):
            idx = [hd * nc + n for hd in range(nh)]
            e = {}

            def through_state():
                e["s_old"] = [state_ref[hd] for hd in range(nh)]
                e["ws_qs"] = [jnp.dot(lhs_ref[b], e["s_old"][hd].astype(BF16),
                                      preferred_element_type=F32) for hd, b in enumerate(idx)]

            def update():
                v_new = [(u_ref[b] - e["ws_qs"][hd][:c]).astype(BF16) for hd, b in enumerate(idx)]
                for hd, b in enumerate(idx):
                    oraw_ref[n * c:(n + 1) * c, hd * DN_HEAD_DIM:(hd + 1) * DN_HEAD_DIM] = (
                        e["ws_qs"][hd][c:]
                        + jnp.dot(attn_ref[b], v_new[hd], preferred_element_type=F32))
                for hd, b in enumerate(idx):
                    state_ref[hd] = e["s_old"][hd] * sdec_ref[b] + lax.dot_general(
                        kdec_ref[b], v_new[hd], TN_DIMS, preferred_element_type=F32)
            return [through_state, update]

        def gated_norm():
            for hd in range(nh):
                hs = slice(hd * DN_HEAD_DIM, (hd + 1) * DN_HEAD_DIM)
                o = oraw_ref[:, hs]
                ms = jnp.mean(o * o, axis=-1, keepdims=True)
                o = o * lax.rsqrt(ms + EPS) * nw_ref[...]
                o_ref[:, hs] = (o * z_ref[:, hs].astype(F32)).astype(BF16)

        return [fn for n in range(nc) for fn in chunk_steps(n)] + [gated_norm]

    @pl.when(t == 0)
    def _():
        state_ref[...] = jnp.zeros(state_ref.shape, F32)
        stages, commit = phase1_stages()
        for fn in stages:
            fn()
        commit()

    @pl.when((t > 0) & (t < nt))
    def _():
        stages, commit = phase1_stages()
        others = phase2_stages()
        done = 0
        for n, fn in enumerate(stages):
            fn()
            upto = -(-(n + 1) * len(others) // len(stages))
            for other in others[done:upto]:
                other()
            done = upto
        commit()

    @pl.when(t == nt)
    def _():
        for fn in phase2_stages():
            fn()


def _deltanet(qa, ka, va, za, gcol, grow, dn_norm_w, batch, seq_len):
    tb = TB_DN
    assert seq_len % tb == 0 and tb % CHUNK == 0
    nt = seq_len // tb
    nc = tb // CHUNK
    nb = DN_HEADS * nc
    grow3 = grow.reshape(AB_ROWS, -1, CHUNK).transpose(1, 0, 2)
    cur = lambda w: pl.BlockSpec((tb, w), lambda b, t: (b * nt + jnp.minimum(t, nt - 1), 0))
    prev = lambda w: pl.BlockSpec((tb, w), lambda b, t: (b * nt + jnp.maximum(t - 1, 0), 0))
    return pl.pallas_call(
        functools.partial(_deltanet_kernel, nt=nt),
        grid=(batch, nt + 1),
        in_specs=[cur(DN_WIDTH)] * 3 + [
            prev(DN_WIDTH), cur(LANES),
            pl.BlockSpec((nc, AB_ROWS, CHUNK),
                         lambda b, t: (b * nt + jnp.minimum(t, nt - 1), 0, 0)),
            _resident((1, DN_HEAD_DIM))],
        out_specs=prev(DN_WIDTH),
        out_shape=jax.ShapeDtypeStruct((batch * seq_len, DN_WIDTH), BF16),
        scratch_shapes=[pltpu.VMEM((DN_HEADS, DN_HEAD_DIM, DN_HEAD_DIM), F32),
                        pltpu.VMEM((tb, DN_WIDTH), F32),
                        pltpu.VMEM((nb, 2 * CHUNK, DN_HEAD_DIM), BF16),
                        pltpu.VMEM((nb, CHUNK, CHUNK), BF16),
                        pltpu.VMEM((nb, CHUNK, DN_HEAD_DIM), F32),
                        pltpu.VMEM((nb, CHUNK, DN_HEAD_DIM), BF16),
                        pltpu.VMEM((nb, 1, DN_HEAD_DIM), F32)],
        compiler_params=pltpu.CompilerParams(
            dimension_semantics=("arbitrary", "arbitrary"), vmem_limit_bytes=VMEM_LIMIT),
        name="gated_deltanet",
    )(qa, ka, va, za, gcol, grow3, dn_norm_w.astype(F32).reshape(1, DN_HEAD_DIM))


def _diffattn_kernel(lq1_ref, lk1_ref, lq2_ref, lk2_ref, q_ref, k_ref, v_ref, z_ref, nw_ref,
                     o_ref, m_ref, l_ref, acc_ref):
    tq = q_ref.shape[0]
    tk = TK_DA
    qi = pl.program_id(2)
    chains = [(hd, comp) for hd in range(HEADS_DA) for comp in range(2)]

    lane = lax.broadcasted_iota(jnp.int32, (1, LANES), 1)
    first_half = lane < DA_QK_DIM
    q_chain = []
    for hd in range(HEADS_DA):
        q = q_ref[:, hd * LANES:(hd + 1) * LANES]
        zero = jnp.zeros_like(q)
        q_chain += [jnp.where(first_half, q, zero), jnp.where(first_half, zero, q)]

    m_ref[...] = jnp.full(m_ref.shape, NEG_INF, F32)
    l_ref[...] = jnp.zeros(l_ref.shape, F32)
    acc_ref[...] = jnp.zeros(acc_ref.shape, F32)

    def attend(j, row0=0, nrows=None, width=None, masked=False):
        nrows = tq if nrows is None else nrows
        width = tk if width is None else width
        rows = slice(row0, row0 + nrows)
        start = pl.multiple_of(j * tk, tk)
        kblk = [k_ref[pl.ds(start, width), hd * LANES:(hd + 1) * LANES] for hd in range(HEADS_DA)]
        ones = jnp.ones((width, LANES), BF16)
        vblk = [jnp.concatenate([v_ref[pl.ds(start, width), hd * LANES:(hd + 1) * LANES], ones],
                                axis=1) for hd in range(HEADS_DA)]
        scores = [lax.dot_general(q_chain[c][rows], kblk[hd], NT_DIMS, preferred_element_type=F32)
                  for c, (hd, _) in enumerate(chains)]
        if masked:
            rq = lax.broadcasted_iota(jnp.int32, (nrows, width), 0) + row0
            ck = lax.broadcasted_iota(jnp.int32, (nrows, width), 1)
            scores = [jnp.where(ck <= rq, s, NEG_INF) for s in scores]
        probs = []
        for c, s in enumerate(scores):
            m_prev = m_ref[c, rows, :]
            m_new = jnp.maximum(m_prev, jnp.max(s, axis=-1, keepdims=True))
            alpha = jnp.exp2(m_prev - m_new)
            probs.append(jnp.exp2(
                (s - jnp.concatenate([m_new] * (width // LANES), axis=1)).astype(BF16)))
            l_ref[c, rows, :] = alpha * l_ref[c, rows, :]
            acc_ref[c, rows, :] = alpha * acc_ref[c, rows, :]
            m_ref[c, rows, :] = m_new
        for c, (hd, _) in enumerate(chains):
            pv = jnp.dot(probs[c], vblk[hd], preferred_element_type=F32)
            acc_ref[c, rows, :] += pv[:, :DA_V_DIM]
            l_ref[c, rows, :] += pv[:, DA_V_DIM:]

    def body(j, carry):
        attend(j)
        return carry

    lax.fori_loop(0, qi, body, 0)
    half = tq // 2
    attend(qi, row0=0, nrows=half, width=half, masked=True)
    attend(qi, row0=half, nrows=half, width=tk, masked=True)

    lam = (jnp.exp(jnp.sum(lq1_ref[...] * lk1_ref[...], axis=-1, keepdims=True))
           - jnp.exp(jnp.sum(lq2_ref[...] * lk2_ref[...], axis=-1, keepdims=True))
           + LAMBDA_INIT)
    for hd in range(HEADS_DA):
        hs = slice(hd * LANES, (hd + 1) * LANES)
        o = acc_ref[2 * hd] / l_ref[2 * hd] - lam * (acc_ref[2 * hd + 1] / l_ref[2 * hd + 1])
        ms = jnp.mean(o * o, axis=-1, keepdims=True)
        o = o * lax.rsqrt(ms + EPS) * nw_ref[...] * (1.0 - LAMBDA_INIT)
        o_ref[:, hs] = (o * z_ref[:, hs].astype(F32)).astype(BF16)


def _diff_attention(qb, kb, vb, zb, lambda_q1, lambda_k1, lambda_q2, lambda_k2, da_norm_w,
                    batch, seq_len):
    tq = TQ_DA
    assert tq == TK_DA and seq_len % tq == 0 and DA_HEADS % HEADS_DA == 0
    nq = seq_len // tq
    width = HEADS_DA * LANES
    lam_vec = lambda v: v.astype(F32).reshape(1, DA_QK_DIM)
    q_blk = pl.BlockSpec((tq, width), lambda b, h, i: (b * nq + i, h))
    kv_blk = pl.BlockSpec((seq_len, width), lambda b, h, i: (b, h))
    stat = pltpu.VMEM((2 * HEADS_DA, tq, LANES), F32)
    return pl.pallas_call(
        _diffattn_kernel,
        grid=(batch, DA_HEADS // HEADS_DA, nq),
        in_specs=[_resident((1, DA_QK_DIM))] * 4 + [q_blk, kv_blk, kv_blk, q_blk,
                                                    _resident((1, DA_V_DIM))],
        out_specs=q_blk,
        out_shape=jax.ShapeDtypeStruct((batch * seq_len, DA_WIDTH), BF16),
        scratch_shapes=[stat, stat, pltpu.VMEM((2 * HEADS_DA, tq, DA_V_DIM), F32)],
        compiler_params=pltpu.CompilerParams(
            dimension_semantics=("arbitrary", "arbitrary", "arbitrary"),
            vmem_limit_bytes=VMEM_LIMIT),
        name="diff_attention",
    )(lam_vec(lambda_q1), lam_vec(lambda_k1), lam_vec(lambda_q2), lam_vec(lambda_k2),
      qb, kb, vb, zb, da_norm_w.astype(F32).reshape(1, DA_V_DIM))


def _output_kernel(oa_ref, ob_ref, x_ref, nw_ref, wga_ref, wgb_ref, woa_ref, wob_ref, wo_ref,
                   out_ref):
    x = x_ref[...]
    ms = jnp.mean(x * x, axis=-1, keepdims=True)
    h = (x * lax.rsqrt(ms + EPS) * nw_ref[...]).astype(BF16)
    gate_a = _sigmoid(jnp.dot(h, wga_ref[...], preferred_element_type=F32))
    gate_b = _sigmoid(jnp.dot(h, wgb_ref[...], preferred_element_type=F32))
    y_a = jnp.dot(oa_ref[...], woa_ref[...], preferred_element_type=F32)
    y_b = jnp.dot(ob_ref[...], wob_ref[...], preferred_element_type=F32)
    y = gate_a * y_a + gate_b * y_b
    out_ref[...] = x + jnp.dot(y.astype(BF16), wo_ref[...], preferred_element_type=F32)


def _output_projection(oa, ob, x2, norm_w, w_gates, w_out_a, w_out_b, w_out):
    m = x2.shape[0]
    tm = TM_OUT
    assert m % tm == 0
    row_blk = lambda w: pl.BlockSpec((tm, w), lambda i: (i, 0))
    return pl.pallas_call(
        _output_kernel,
        grid=(m // tm,),
        in_specs=[row_blk(DN_WIDTH), row_blk(DA_WIDTH), row_blk(D_MODEL), _resident((1, D_MODEL)),
                  _resident((D_MODEL, D_MODEL)), _resident((D_MODEL, D_MODEL)),
                  _resident((DN_WIDTH, D_MODEL)), _resident((DA_WIDTH, D_MODEL)),
                  _resident((D_MODEL, D_MODEL))],
        out_specs=row_blk(D_MODEL),
        out_shape=jax.ShapeDtypeStruct((m, D_MODEL), F32),
        compiler_params=pltpu.CompilerParams(
            dimension_semantics=("arbitrary",), vmem_limit_bytes=VMEM_LIMIT),
        name="output_projection",
    )(oa, ob, x2, norm_w.astype(F32).reshape(1, D_MODEL), *w_gates,
      w_out_a.astype(BF16), w_out_b.astype(BF16), w_out.astype(BF16))


def kernel(x, norm_w, w_in, conv_w, a_log, dt_bias, dn_norm_w, q_norm_w, k_norm_w,
           lambda_q1, lambda_k1, lambda_q2, lambda_k2, da_norm_w, w_out_a, w_out_b, w_out):
    batch, seq_len, d_model = x.shape
    assert d_model == D_MODEL
    x2 = x.reshape(batch * seq_len, d_model)
    (qa, ka, va, za, gcol, grow, qb, kb, vb, zb), w_gates = _in_projection(
        x2, norm_w, w_in, conv_w, a_log, dt_bias, q_norm_w, k_norm_w, seq_len)
    oa = _deltanet(qa, ka, va, za, gcol, grow, dn_norm_w, batch, seq_len)
    ob = _diff_attention(qb, kb, vb, zb, lambda_q1, lambda_k1, lambda_q2, lambda_k2, da_norm_w,
                         batch, seq_len)
    out = _output_projection(oa, ob, x2, norm_w, w_gates, w_out_a, w_out_b, w_out)
    return out.reshape(batch, seq_len, d_model)
```

```python
import functools
import math

import jax
import jax.numpy as jnp
from jax import lax
from jax.experimental import pallas as pl
from jax.experimental.pallas import tpu as pltpu

F32 = jnp.float32
BF16 = jnp.bfloat16

D_MODEL = 1024
DN_HEADS = 4
DN_HEAD_DIM = 128
DN_WIDTH = DN_HEADS * DN_HEAD_DIM
CONV_WIDTH = 4
CHUNK = 64
DA_HEADS = 4
DA_QK_DIM = 64
DA_V_DIM = 2 * DA_QK_DIM
DA_WIDTH = DA_HEADS * DA_V_DIM
DA_QK_WIDTH = DA_HEADS * 2 * DA_QK_DIM
LAMBDA_INIT = 0.8 - 0.6 * math.exp(-0.3 * 0)
EPS = 1e-6
NEG_INF = -1e30
LOG2_E = math.log2(math.e)

LANES = 128
MXU_COLS = 256
SUBLANES = 8
BF16_ROWS = 16
AB_ROWS = BF16_ROWS
VMEM_LIMIT = 48 * 1024 * 1024

TM_IN = 512
TB_DN = 512
TQ_DA = 512
TK_DA = 512
HEADS_DA = 2
TM_OUT = 512

NT_DIMS = (((1,), (1,)), ((), ()))
TN_DIMS = (((0,), (0,)), ((), ()))


def _sigmoid(v):
    return 0.5 + 0.5 * jnp.tanh(0.5 * v)


def _silu(v):
    h = 0.5 * v
    return h + h * jnp.tanh(h)


def _softplus(v):
    return jnp.maximum(v, 0.0) + jnp.log1p(jnp.exp(-jnp.abs(v)))


def _resident(shape):
    zeros = (0,) * len(shape)
    return pl.BlockSpec(shape, lambda *_: zeros, pipeline_mode=pl.Buffered(1))


def _inproj_kernel(x_ref, nw_ref, wqkv_ref, convw_ref, wza_ref, wab_ref, wabt_ref,
                   alog_l_ref, dtb_l_ref, alog_s_ref, dtb_s_ref, qnw_ref, knw_ref,
                   wqb_ref, wkb_ref, wvb_ref, wzb_ref,
                   qa_ref, ka_ref, va_ref, za_ref, gcol_ref, grow_ref,
                   qb_ref, kb_ref, vb_ref, zb_ref,
                   conv_buf, *, tiles_per_seq):
    tm = x_ref.shape[0]
    half = tm // 2
    i = pl.program_id(0)
    head0 = SUBLANES - (CONV_WIDTH - 1)
    slab = 512

    @pl.when(i % tiles_per_seq == 0)
    def _():
        conv_buf[0:SUBLANES, :] = jnp.zeros((SUBLANES, 3 * DN_WIDTH), F32)

    def normed(r0):
        x = x_ref[r0:r0 + half, :]
        ms = jnp.mean(x * x, axis=-1, keepdims=True)
        return (x * lax.rsqrt(ms + EPS) * nw_ref[...]).astype(BF16)

    def tasks(h, r0):
        rows = slice(r0, r0 + half)

        def conv_qkv(s, out_ref):
            cs = slice(s * DN_WIDTH, (s + 1) * DN_WIDTH)

            res = []

            def matmul():
                res.append(jnp.dot(h, wqkv_ref[:, cs], preferred_element_type=F32))
                conv_buf[SUBLANES + r0:SUBLANES + r0 + half, cs] = res[0]

            def epilogue():
                acc = convw_ref[CONV_WIDTH - 1:CONV_WIDTH, cs] * res[0]
                for j in range(CONV_WIDTH - 1):
                    acc = acc + (convw_ref[j:j + 1, cs]
                                 * conv_buf[head0 + j + r0:head0 + j + r0 + half, cs])
                y = _silu(acc)
                if s == 2:
                    out_ref[rows, :] = y.astype(BF16)
                    return
                scale = DN_HEAD_DIM ** -0.5 if s == 0 else 1.0
                for hd in range(DN_HEADS):
                    hs = slice(hd * DN_HEAD_DIM, (hd + 1) * DN_HEAD_DIM)
                    blk = y[:, hs]
                    ss = jnp.sum(blk * blk, axis=-1, keepdims=True)
                    out_ref[rows, hs] = (blk * (lax.rsqrt(ss + EPS) * scale)).astype(BF16)
            return matmul, [epilogue]

        def plain(w_ref, c0, out_ref, act):
            res = []

            def matmul():
                res.append(jnp.dot(h, w_ref[:, c0:c0 + slab], preferred_element_type=F32))

            def chunk(n):
                def run():
                    cs = slice(n * LANES, (n + 1) * LANES)
                    out_ref[rows, c0 + n * LANES:c0 + (n + 1) * LANES] = (
                        act(res[0][:, cs]).astype(BF16))
                return run
            return matmul, [chunk(n) for n in range(slab // LANES)]

        def qk_norm(w_ref, nrm_ref, out_ref, scale):
            res = []

            def matmul():
                res.append(jnp.dot(h, w_ref[...], preferred_element_type=F32))

            def chunk(hd):
                def run():
                    lo = lax.broadcasted_iota(jnp.int32, (half, LANES), 1) < DA_QK_DIM
                    blk = res[0][:, hd * LANES:(hd + 1) * LANES]
                    sq = blk * blk
                    s_lo = jnp.sum(jnp.where(lo, sq, 0.0), axis=-1, keepdims=True)
                    s_hi = jnp.sum(jnp.where(lo, 0.0, sq), axis=-1, keepdims=True)
                    msq = jnp.where(lo, s_lo, s_hi) * (1.0 / DA_QK_DIM)
                    out_ref[rows, hd * LANES:(hd + 1) * LANES] = (
                        blk * lax.rsqrt(msq + EPS) * nrm_ref[...] * scale).astype(BF16)
                return run
            return matmul, [chunk(hd) for hd in range(DA_HEADS)]

        def decay_beta():
            def matmul():
                ab = jnp.dot(h, wab_ref[...], preferred_element_type=F32)
                lane = lax.broadcasted_iota(jnp.int32, ab.shape, 1)
                g_l = -jnp.exp(alog_l_ref[...]) * _softplus(ab + dtb_l_ref[...])
                gcol_ref[rows, :] = jnp.where(lane < DN_HEADS, g_l, _sigmoid(ab))
                abt = lax.dot_general(wabt_ref[...], h, NT_DIMS,
                                      preferred_element_type=F32)
                row = lax.broadcasted_iota(jnp.int32, abt.shape, 0)
                g_s = -jnp.exp(alog_s_ref[...]) * _softplus(abt + dtb_s_ref[...])
                grow_ref[:, rows] = jnp.where(row < DN_HEADS, g_s, _sigmoid(abt))
            return matmul, []

        ident = lambda v: v
        return [conv_qkv(0, qa_ref), plain(wvb_ref, 0, vb_ref, ident),
                conv_qkv(1, ka_ref), plain(wza_ref, 0, za_ref, _silu),
                conv_qkv(2, va_ref), plain(wzb_ref, 0, zb_ref, _silu),
                qk_norm(wqb_ref, qnw_ref, qb_ref, DA_QK_DIM ** -0.5 * LOG2_E),
                qk_norm(wkb_ref, knw_ref, kb_ref, 1.0), decay_beta()]

    def run_tasks(task_list, extra=None):
        for n, (matmul, chunks) in enumerate(task_list):
            matmul()
            for fn in chunks:
                fn()
            if extra is not None and n == 6:
                extra()

    h_second = []
    run_tasks(tasks(normed(0), 0), extra=lambda: h_second.append(normed(half)))
    run_tasks(tasks(h_second[0], half))
    conv_buf[0:SUBLANES, :] = conv_buf[tm:tm + SUBLANES, :]


def _in_projection(x2, norm_w, w_in, conv_w, a_log, dt_bias, q_norm_w, k_norm_w, seq_len):
    m = x2.shape[0]
    tm = TM_IN
    assert m % tm == 0 and seq_len % tm == 0
    o = 0
    cols = {}
    for name, width in (("qkv", 3 * DN_WIDTH), ("za", DN_WIDTH), ("a", DN_HEADS), ("b", DN_HEADS),
                        ("qb", DA_QK_WIDTH), ("kb", DA_QK_WIDTH), ("vb", DA_WIDTH), ("zb", DA_WIDTH),
                        ("ga", D_MODEL), ("gb", D_MODEL)):
        cols[name] = w_in[:, o:o + width]
        o += width
    assert o == w_in.shape[1]
    wb = {k: v.astype(BF16) for k, v in cols.items()}
    w_ab = jnp.concatenate([cols["a"], cols["b"]], axis=1)
    w_ab_l = jnp.pad(w_ab, ((0, 0), (0, LANES - 2 * DN_HEADS))).astype(BF16)
    w_ab_s = jnp.pad(w_ab.T, ((0, AB_ROWS - 2 * DN_HEADS), (0, 0))).astype(BF16)
    alog_l = jnp.pad(a_log.astype(F32), (0, LANES - DN_HEADS)).reshape(1, LANES)
    dtb_l = jnp.pad(dt_bias.astype(F32), (0, LANES - DN_HEADS)).reshape(1, LANES)
    alog_s = jnp.pad(a_log.astype(F32), (0, AB_ROWS - DN_HEADS)).reshape(AB_ROWS, 1)
    dtb_s = jnp.pad(dt_bias.astype(F32), (0, AB_ROWS - DN_HEADS)).reshape(AB_ROWS, 1)
    qnw = jnp.tile(q_norm_w.astype(F32), 2).reshape(1, LANES)
    knw = jnp.tile(k_norm_w.astype(F32), 2).reshape(1, LANES)

    row_blk = lambda w: pl.BlockSpec((tm, w), lambda i: (i, 0))
    in_specs = [
        row_blk(D_MODEL), _resident((1, D_MODEL)),
        _resident((D_MODEL, 3 * DN_WIDTH)), _resident((CONV_WIDTH, 3 * DN_WIDTH)),
        _resident((D_MODEL, DN_WIDTH)), _resident((D_MODEL, LANES)), _resident((AB_ROWS, D_MODEL)),
        _resident((1, LANES)), _resident((1, LANES)), _resident((AB_ROWS, 1)), _resident((AB_ROWS, 1)),
        _resident((1, LANES)), _resident((1, LANES)),
        _resident((D_MODEL, DA_QK_WIDTH)), _resident((D_MODEL, DA_QK_WIDTH)),
        _resident((D_MODEL, DA_WIDTH)), _resident((D_MODEL, DA_WIDTH)),
    ]
    out_shapes = [jax.ShapeDtypeStruct((m, DN_WIDTH), BF16)] * 4 + [
        jax.ShapeDtypeStruct((m, LANES), F32), jax.ShapeDtypeStruct((AB_ROWS, m), F32)] + [
        jax.ShapeDtypeStruct((m, DA_WIDTH), BF16)] * 4
    out_specs = [row_blk(DN_WIDTH)] * 4 + [
        row_blk(LANES), pl.BlockSpec((AB_ROWS, tm), lambda i: (0, i))] + [
        row_blk(DA_WIDTH)] * 4
    outs = pl.pallas_call(
        functools.partial(_inproj_kernel, tiles_per_seq=seq_len // tm),
        grid=(m // tm,),
        in_specs=in_specs,
        out_specs=out_specs,
        out_shape=out_shapes,
        scratch_shapes=[pltpu.VMEM((tm + SUBLANES, 3 * DN_WIDTH), F32)],
        compiler_params=pltpu.CompilerParams(
            dimension_semantics=("arbitrary",), vmem_limit_bytes=VMEM_LIMIT),
        name="in_projection",
    )(x2, norm_w.astype(F32).reshape(1, D_MODEL), wb["qkv"], conv_w.astype(F32), wb["za"],
      w_ab_l, w_ab_s, alog_l, dtb_l, alog_s, dtb_s, qnw, knw,
      wb["qb"], wb["kb"], wb["vb"], wb["zb"])
    return outs, (wb["ga"], wb["gb"])


def _bmm(a, b):
    return jnp.einsum('nij,njk->nik', a, b, preferred_element_type=F32)


def _bmm_nt(a, b):
    return jnp.einsum('nid,njd->nij', a, b, preferred_element_type=F32)


def _split_bf16(x, parts):
    out = []
    for _ in range(parts - 1):
        hi = x.astype(BF16)
        out.append(hi)
        x = x - hi.astype(F32)
    out.append(x.astype(BF16))
    return out


def _deltanet_kernel(q_ref, k_ref, v_ref, z_ref, gcol_ref, grow_ref, nw_ref, o_ref,
                     state_ref, oraw_ref, lhs_ref, attn_ref, u_ref, kdec_ref, sdec_ref, *, nt):
    tb = q_ref.shape[0]
    c = CHUNK
    nc = tb // c
    nh = DN_HEADS
    t = pl.program_id(1)

    ri = lax.broadcasted_iota(jnp.int32, (c, c), 0)
    ci = lax.broadcasted_iota(jnp.int32, (c, c), 1)
    lower = ri >= ci
    strict = ri > ci
    eye = (ri == ci).astype(F32)
    tri = lower.astype(BF16)
    tri_t = (ri <= ci).astype(BF16)

    def heads_major(ref):
        return jnp.concatenate(
            [ref[:, hd * DN_HEAD_DIM:(hd + 1) * DN_HEAD_DIM].reshape(nc, c, DN_HEAD_DIM)
             for hd in range(nh)], axis=0)

    def phase1_stages():
        e = {}

        def decays():
            gb_col = gcol_ref[...].reshape(nc, c, LANES)
            tri_b = jnp.broadcast_to(tri, (nc, c, c))
            gc_col = sum(_bmm(tri_b, part) for part in _split_bf16(gb_col, 3))
            gb_row = grow_ref[...]
            gc_row = sum(jnp.dot(part, tri_t, preferred_element_type=F32)
                         for part in _split_bf16(gb_row.reshape(nc * AB_ROWS, c), 3)
                         ).reshape(nc, AB_ROWS, c)

            def col_form(src, lane):
                return jnp.concatenate(
                    [jnp.broadcast_to(src[:, :, lane + hd:lane + hd + 1], (nc, c, LANES))
                     for hd in range(nh)], axis=0)

            def row_form(src, row):
                return jnp.concatenate(
                    [src[:, row + hd:row + hd + 1, :] for hd in range(nh)], axis=0)

            e["g_c"] = col_form(gc_col, 0)
            e["beta_c"] = col_form(gb_col, nh)
            e["g_r"] = row_form(gc_row, 0)
            e["beta_r"] = row_form(gb_row, nh)
            e["g_last"] = e["g_c"][:, c - 1:c, :]
            diff = e["g_c"][:, :, :c] - e["g_r"]
            e["decay"] = jnp.where(lower, jnp.exp(jnp.where(lower, diff, 0.0)), 0.0)

        def gram_kk():
            e["q"], e["k"], e["v"] = heads_major(q_ref), heads_major(k_ref), heads_major(v_ref)
            l_strict = jnp.where(
                strict, _bmm_nt(e["k"], e["k"]) * e["decay"] * e["beta_c"][:, :, :c], 0.0)
            e["inv"] = eye - l_strict
            e["power"] = l_strict.astype(BF16)

        def gram_qk():
            e["attn"] = jnp.where(lower, _bmm_nt(e["q"], e["k"]) * e["decay"], 0.0).astype(BF16)

        def next_power():
            e["power"] = _bmm(e["power"], e["power"]).astype(BF16)

        def apply_power():
            e["inv"] = e["inv"] + _bmm(e["inv"].astype(BF16), e["power"])

        def solve_u():
            e["t_beta"] = e["inv"] * e["beta_r"]
            e["u"] = _bmm(e["t_beta"].astype(BF16), e["v"])

        def solve_w():
            e["w"] = _bmm((e["t_beta"] * jnp.exp(e["g_r"])).astype(BF16), e["k"]).astype(BF16)

        def commit():
            q_decay = (e["q"].astype(F32) * jnp.exp(e["g_c"])).astype(BF16)
            lhs_ref[...] = jnp.concatenate([e["w"], q_decay], axis=1)
            attn_ref[...] = e["attn"]
            u_ref[...] = e["u"]
            kdec_ref[...] = (e["k"].astype(F32) * jnp.exp(e["g_last"] - e["g_c"])).astype(BF16)
            sdec_ref[...] = jnp.exp(e["g_last"])

        levels = [next_power, apply_power] * (c.bit_length() - 2)
        return [decays, gram_kk, gram_qk] + levels + [solve_u, solve_w], commit

    def phase2_stages():
        def chunk_steps(n):
            idx = [hd * nc + n for hd in range(nh)]
            e = {}

            def through_state():
                e["s_old"] = [state_ref[hd] for hd in range(nh)]
                e["ws_qs"] = [jnp.dot(lhs_ref[b], e["s_old"][hd].astype(BF16),
                                      preferred_element_type=F32) for hd, b in enumerate(idx)]

            def update():
                v_new = [(u_ref[b] - e["ws_qs"][hd][:c]).astype(BF16) for hd, b in enumerate(idx)]
                for hd, b in enumerate(idx):
                    oraw_ref[n * c:(n + 1) * c, hd * DN_HEAD_DIM:(hd + 1) * DN_HEAD_DIM] = (
                        e["ws_qs"][hd][c:]
                        + jnp.dot(attn_ref[b], v_new[hd], preferred_element_type=F32))
                for hd, b in enumerate(idx):
                    state_ref[hd] = e["s_old"][hd] * sdec_ref[b] + lax.dot_general(
                        kdec_ref[b], v_new[hd], TN_DIMS, preferred_element_type=F32)
            return [through_state, update]

        def gated_norm():
            for hd in range(nh):
                hs = slice(hd * DN_HEAD_DIM, (hd + 1) * DN_HEAD_DIM)
                o = oraw_ref[:, hs]
                ms = jnp.mean(o * o, axis=-1, keepdims=True)
                o = o * lax.rsqrt(ms + EPS) * nw_ref[...]
                o_ref[:, hs] = (o * z_ref[:, hs].astype(F32)).astype(BF16)

        return [fn for n in range(nc) for fn in chunk_steps(n)] + [gated_norm]

    @pl.when(t == 0)
    def _():
        state_ref[...] = jnp.zeros(state_ref.shape, F32)
        stages, commit = phase1_stages()
        for fn in stages:
            fn()
        commit()

    @pl.when((t > 0) & (t < nt))
    def _():
        stages, commit = phase1_stages()
        others = phase2_stages()
        done = 0
        for n, fn in enumerate(stages):
            fn()
            upto = -(-(n + 1) * len(others) // len(stages))
            for other in others[done:upto]:
                other()
            done = upto
        commit()

    @pl.when(t == nt)
    def _():
        for fn in phase2_stages():
            fn()


def _deltanet(qa, ka, va, za, gcol, grow, dn_norm_w, batch, seq_len):
    tb = TB_DN
    assert seq_len % tb == 0 and tb % CHUNK == 0
    nt = seq_len // tb
    nc = tb // CHUNK
    nb = DN_HEADS * nc
    grow3 = grow.reshape(AB_ROWS, -1, CHUNK).transpose(1, 0, 2)
    cur = lambda w: pl.BlockSpec((tb, w), lambda b, t: (b * nt + jnp.minimum(t, nt - 1), 0))
    prev = lambda w: pl.BlockSpec((tb, w), lambda b, t: (b * nt + jnp.maximum(t - 1, 0), 0))
    return pl.pallas_call(
        functools.partial(_deltanet_kernel, nt=nt),
        grid=(batch, nt + 1),
        in_specs=[cur(DN_WIDTH)] * 3 + [
            prev(DN_WIDTH), cur(LANES),
            pl.BlockSpec((nc, AB_ROWS, CHUNK),
                         lambda b, t: (b * nt + jnp.minimum(t, nt - 1), 0, 0)),
            _resident((1, DN_HEAD_DIM))],
        out_specs=prev(DN_WIDTH),
        out_shape=jax.ShapeDtypeStruct((batch * seq_len, DN_WIDTH), BF16),
        scratch_shapes=[pltpu.VMEM((DN_HEADS, DN_HEAD_DIM, DN_HEAD_DIM), F32),
                        pltpu.VMEM((tb, DN_WIDTH), F32),
                        pltpu.VMEM((nb, 2 * CHUNK, DN_HEAD_DIM), BF16),
                        pltpu.VMEM((nb, CHUNK, CHUNK), BF16),
                        pltpu.VMEM((nb, CHUNK, DN_HEAD_DIM), F32),
                        pltpu.VMEM((nb, CHUNK, DN_HEAD_DIM), BF16),
                        pltpu.VMEM((nb, 1, DN_HEAD_DIM), F32)],
        compiler_params=pltpu.CompilerParams(
            dimension_semantics=("arbitrary", "arbitrary"), vmem_limit_bytes=VMEM_LIMIT),
        name="gated_deltanet",
    )(qa, ka, va, za, gcol, grow3, dn_norm_w.astype(F32).reshape(1, DN_HEAD_DIM))


def _diffattn_kernel(lq1_ref, lk1_ref, lq2_ref, lk2_ref, q_ref, k_ref, v_ref, z_ref, nw_ref,
                     o_ref, m_ref, l_ref, acc_ref):
    tq = q_ref.shape[0]
    tk = TK_DA
    qi = pl.program_id(2)
    chains = [(hd, comp) for hd in range(HEADS_DA) for comp in range(2)]

    lane = lax.broadcasted_iota(jnp.int32, (1, LANES), 1)
    first_half = lane < DA_QK_DIM
    q_chain = []
    for hd in range(HEADS_DA):
        q = q_ref[:, hd * LANES:(hd + 1) * LANES]
        zero = jnp.zeros_like(q)
        q_chain += [jnp.where(first_half, q, zero), jnp.where(first_half, zero, q)]

    def attend(j, row0=0, nrows=None, width=None, masked=False, first=False):
        nrows = tq if nrows is None else nrows
        width = tk if width is None else width
        rows = slice(row0, row0 + nrows)
        start = pl.multiple_of(j * tk, tk)
        kblk = [k_ref[pl.ds(start, width), hd * LANES:(hd + 1) * LANES] for hd in range(HEADS_DA)]
        ones = jnp.ones((width, LANES), BF16)
        vblk = [jnp.concatenate([v_ref[pl.ds(start, width), hd * LANES:(hd + 1) * LANES], ones],
                                axis=1) for hd in range(HEADS_DA)]
        scores = [lax.dot_general(q_chain[c][rows], kblk[hd], NT_DIMS, preferred_element_type=F32)
                  for c, (hd, _) in enumerate(chains)]
        if masked:
            rq = lax.broadcasted_iota(jnp.int32, (nrows, width), 0) + row0
            ck = lax.broadcasted_iota(jnp.int32, (nrows, width), 1)
            scores = [jnp.where(ck <= rq, s, NEG_INF) for s in scores]
        probs = []
        for c, s in enumerate(scores):
            m_new = jnp.max(s, axis=-1, keepdims=True)
            if first:
                m_new = jnp.broadcast_to(m_new, (nrows, LANES))
            else:
                m_prev = m_ref[c, rows, :]
                m_new = jnp.maximum(m_prev, m_new)
                alpha = jnp.exp2(m_prev - m_new)
                l_ref[c, rows, :] = alpha * l_ref[c, rows, :]
                acc_ref[c, rows, :] = alpha * acc_ref[c, rows, :]
            probs.append(jnp.exp2(
                (s - jnp.concatenate([m_new] * (width // LANES), axis=1)).astype(BF16)))
            m_ref[c, rows, :] = m_new
        for c, (hd, _) in enumerate(chains):
            pv = jnp.dot(probs[c], vblk[hd], preferred_element_type=F32)
            if first:
                acc_ref[c, rows, :] = pv[:, :DA_V_DIM]
                l_ref[c, rows, :] = pv[:, DA_V_DIM:]
            else:
                acc_ref[c, rows, :] += pv[:, :DA_V_DIM]
                l_ref[c, rows, :] += pv[:, DA_V_DIM:]

    half = tq // 2
    attend(qi, row0=0, nrows=half, width=half, masked=True, first=True)
    attend(qi, row0=half, nrows=half, width=tk, masked=True, first=True)

    def body(j, carry):
        attend(j)
        return carry

    lax.fori_loop(0, qi, body, 0)

    lam = (jnp.exp(jnp.sum(lq1_ref[...] * lk1_ref[...], axis=-1, keepdims=True))
           - jnp.exp(jnp.sum(lq2_ref[...] * lk2_ref[...], axis=-1, keepdims=True))
           + LAMBDA_INIT)
    for hd in range(HEADS_DA):
        hs = slice(hd * LANES, (hd + 1) * LANES)
        o = acc_ref[2 * hd] / l_ref[2 * hd] - lam * (acc_ref[2 * hd + 1] / l_ref[2 * hd + 1])
        ms = jnp.mean(o * o, axis=-1, keepdims=True)
        o = o * lax.rsqrt(ms + EPS) * nw_ref[...] * (1.0 - LAMBDA_INIT)
        o_ref[:, hs] = (o * z_ref[:, hs].astype(F32)).astype(BF16)


def _diff_attention(qb, kb, vb, zb, lambda_q1, lambda_k1, lambda_q2, lambda_k2, da_norm_w,
                    batch, seq_len):
    tq = TQ_DA
    assert tq == TK_DA and seq_len % tq == 0 and DA_HEADS % HEADS_DA == 0
    nq = seq_len // tq
    width = HEADS_DA * LANES
    lam_vec = lambda v: v.astype(F32).reshape(1, DA_QK_DIM)
    q_blk = pl.BlockSpec((tq, width), lambda b, h, i: (b * nq + i, h))
    kv_blk = pl.BlockSpec((seq_len, width), lambda b, h, i: (b, h))
    stat = pltpu.VMEM((2 * HEADS_DA, tq, LANES), F32)
    return pl.pallas_call(
        _diffattn_kernel,
        grid=(batch, DA_HEADS // HEADS_DA, nq),
        in_specs=[_resident((1, DA_QK_DIM))] * 4 + [q_blk, kv_blk, kv_blk, q_blk,
                                                    _resident((1, DA_V_DIM))],
        out_specs=q_blk,
        out_shape=jax.ShapeDtypeStruct((batch * seq_len, DA_WIDTH), BF16),
        scratch_shapes=[stat, stat, pltpu.VMEM((2 * HEADS_DA, tq, DA_V_DIM), F32)],
        compiler_params=pltpu.CompilerParams(
            dimension_semantics=("arbitrary", "arbitrary", "arbitrary"),
            vmem_limit_bytes=VMEM_LIMIT),
        name="diff_attention",
    )(lam_vec(lambda_q1), lam_vec(lambda_k1), lam_vec(lambda_q2), lam_vec(lambda_k2),
      qb, kb, vb, zb, da_norm_w.astype(F32).reshape(1, DA_V_DIM))


def _output_kernel(oa_ref, ob_ref, x_ref, nw_ref, wga_ref, wgb_ref, woa_ref, wob_ref, wo_ref,
                   out_ref):
    x = x_ref[...]
    ms = jnp.mean(x * x, axis=-1, keepdims=True)
    h = (x * lax.rsqrt(ms + EPS) * nw_ref[...]).astype(BF16)
    gate_a = _sigmoid(jnp.dot(h, wga_ref[...], preferred_element_type=F32))
    gate_b = _sigmoid(jnp.dot(h, wgb_ref[...], preferred_element_type=F32))
    y_a = jnp.dot(oa_ref[...], woa_ref[...], preferred_element_type=F32)
    y_b = jnp.dot(ob_ref[...], wob_ref[...], preferred_element_type=F32)
    y = gate_a * y_a + gate_b * y_b
    out_ref[...] = x + jnp.dot(y.astype(BF16), wo_ref[...], preferred_element_type=F32)


def _output_projection(oa, ob, x2, norm_w, w_gates, w_out_a, w_out_b, w_out):
    m = x2.shape[0]
    tm = TM_OUT
    assert m % tm == 0
    row_blk = lambda w: pl.BlockSpec((tm, w), lambda i: (i, 0))
    return pl.pallas_call(
        _output_kernel,
        grid=(m // tm,),
        in_specs=[row_blk(DN_WIDTH), row_blk(DA_WIDTH), row_blk(D_MODEL), _resident((1, D_MODEL)),
                  _resident((D_MODEL, D_MODEL)), _resident((D_MODEL, D_MODEL)),
                  _resident((DN_WIDTH, D_MODEL)), _resident((DA_WIDTH, D_MODEL)),
                  _resident((D_MODEL, D_MODEL))],
        out_specs=row_blk(D_MODEL),
        out_shape=jax.ShapeDtypeStruct((m, D_MODEL), F32),
        compiler_params=pltpu.CompilerParams(
            dimension_semantics=("arbitrary",), vmem_limit_bytes=VMEM_LIMIT),
        name="output_projection",
    )(oa, ob, x2, norm_w.astype(F32).reshape(1, D_MODEL), *w_gates,
      w_out_a.astype(BF16), w_out_b.astype(BF16), w_out.astype(BF16))


def kernel(x, norm_w, w_in, conv_w, a_log, dt_bias, dn_norm_w, q_norm_w, k_norm_w,
           lambda_q1, lambda_k1, lambda_q2, lambda_k2, da_norm_w, w_out_a, w_out_b, w_out):
    batch, seq_len, d_model = x.shape
    assert d_model == D_MODEL
    x2 = x.reshape(batch * seq_len, d_model)
    (qa, ka, va, za, gcol, grow, qb, kb, vb, zb), w_gates = _in_projection(
        x2, norm_w, w_in, conv_w, a_log, dt_bias, q_norm_w, k_norm_w, seq_len)
    oa = _deltanet(qa, ka, va, za, gcol, grow, dn_norm_w, batch, seq_len)
    ob = _diff_attention(qb, kb, vb, zb, lambda_q1, lambda_k1, lambda_q2, lambda_k2, da_norm_w,
                         batch, seq_len)
    out = _output_projection(oa, ob, x2, norm_w, w_gates, w_out_a, w_out_b, w_out)
    return out.reshape(batch, seq_len, d_model)
```

```python
import functools
import math

import jax
import jax.numpy as jnp
from jax import lax
from jax.experimental import pallas as pl
from jax.experimental.pallas import tpu as pltpu

F32 = jnp.float32
BF16 = jnp.bfloat16

D_MODEL = 1024
DN_HEADS = 4
DN_HEAD_DIM = 128
DN_WIDTH = DN_HEADS * DN_HEAD_DIM
CONV_WIDTH = 4
CHUNK = 64
DA_HEADS = 4
DA_QK_DIM = 64
DA_V_DIM = 2 * DA_QK_DIM
DA_WIDTH = DA_HEADS * DA_V_DIM
DA_QK_WIDTH = DA_HEADS * 2 * DA_QK_DIM
LAMBDA_INIT = 0.8 - 0.6 * math.exp(-0.3 * 0)
EPS = 1e-6
NEG_INF = -1e30
LOG2_E = math.log2(math.e)

LANES = 128
MXU_COLS = 256
SUBLANES = 8
BF16_ROWS = 16
AB_ROWS = BF16_ROWS
VMEM_LIMIT = 48 * 1024 * 1024

TM_IN = 512
TB_DN = 512
TQ_DA = 1024
DIAG_BAND = 256
TK_DA = 512
HEADS_DA = 2
TM_OUT = 512

NT_DIMS = (((1,), (1,)), ((), ()))
TN_DIMS = (((0,), (0,)), ((), ()))


def _sigmoid(v):
    return 0.5 + 0.5 * jnp.tanh(0.5 * v)


def _silu(v):
    h = 0.5 * v
    return h + h * jnp.tanh(h)


def _softplus(v):
    return jnp.maximum(v, 0.0) + jnp.log1p(jnp.exp(-jnp.abs(v)))


def _resident(shape):
    zeros = (0,) * len(shape)
    return pl.BlockSpec(shape, lambda *_: zeros, pipeline_mode=pl.Buffered(1))


def _inproj_kernel(x_ref, nw_ref, wqkv_ref, convw_ref, wza_ref, wab_ref, wabt_ref,
                   alog_l_ref, dtb_l_ref, alog_s_ref, dtb_s_ref, qnw_ref, knw_ref,
                   wqb_ref, wkb_ref, wvb_ref, wzb_ref,
                   qa_ref, ka_ref, va_ref, za_ref, gcol_ref, grow_ref,
                   qb_ref, kb_ref, vb_ref, zb_ref,
                   conv_buf, *, tiles_per_seq):
    tm = x_ref.shape[0]
    half = tm // 2
    i = pl.program_id(0)
    head0 = SUBLANES - (CONV_WIDTH - 1)
    slab = 512

    @pl.when(i % tiles_per_seq == 0)
    def _():
        conv_buf[0:SUBLANES, :] = jnp.zeros((SUBLANES, 3 * DN_WIDTH), F32)

    def normed(r0):
        x = x_ref[r0:r0 + half, :]
        ms = jnp.mean(x * x, axis=-1, keepdims=True)
        return (x * lax.rsqrt(ms + EPS) * nw_ref[...]).astype(BF16)

    def tasks(h, r0):
        rows = slice(r0, r0 + half)

        def conv_qkv(s, out_ref):
            cs = slice(s * DN_WIDTH, (s + 1) * DN_WIDTH)

            res = []

            def matmul():
                res.append(jnp.dot(h, wqkv_ref[:, cs], preferred_element_type=F32))
                conv_buf[SUBLANES + r0:SUBLANES + r0 + half, cs] = res[0]

            def epilogue():
                acc = convw_ref[CONV_WIDTH - 1:CONV_WIDTH, cs] * res[0]
                for j in range(CONV_WIDTH - 1):
                    acc = acc + (convw_ref[j:j + 1, cs]
                                 * conv_buf[head0 + j + r0:head0 + j + r0 + half, cs])
                y = _silu(acc)
                if s == 2:
                    out_ref[rows, :] = y.astype(BF16)
                    return
                scale = DN_HEAD_DIM ** -0.5 if s == 0 else 1.0
                for hd in range(DN_HEADS):
                    hs = slice(hd * DN_HEAD_DIM, (hd + 1) * DN_HEAD_DIM)
                    blk = y[:, hs]
                    ss = jnp.sum(blk * blk, axis=-1, keepdims=True)
                    out_ref[rows, hs] = (blk * (lax.rsqrt(ss + EPS) * scale)).astype(BF16)
            return matmul, [epilogue]

        def plain(w_ref, c0, out_ref, act):
            res = []

            def matmul():
                res.append(jnp.dot(h, w_ref[:, c0:c0 + slab], preferred_element_type=F32))

            def chunk(n):
                def run():
                    cs = slice(n * LANES, (n + 1) * LANES)
                    out_ref[rows, c0 + n * LANES:c0 + (n + 1) * LANES] = (
                        act(res[0][:, cs]).astype(BF16))
                return run
            return matmul, [chunk(n) for n in range(slab // LANES)]

        def qk_norm(w_ref, nrm_ref, out_ref, scale):
            res = []

            def matmul():
                res.append(jnp.dot(h, w_ref[...], preferred_element_type=F32))

            def chunk(hd):
                def run():
                    lo = lax.broadcasted_iota(jnp.int32, (half, LANES), 1) < DA_QK_DIM
                    blk = res[0][:, hd * LANES:(hd + 1) * LANES]
                    sq = blk * blk
                    s_lo = jnp.sum(jnp.where(lo, sq, 0.0), axis=-1, keepdims=True)
                    s_hi = jnp.sum(jnp.where(lo, 0.0, sq), axis=-1, keepdims=True)
                    msq = jnp.where(lo, s_lo, s_hi) * (1.0 / DA_QK_DIM)
                    out_ref[rows, hd * LANES:(hd + 1) * LANES] = (
                        blk * lax.rsqrt(msq + EPS) * nrm_ref[...] * scale).astype(BF16)
                return run
            return matmul, [chunk(hd) for hd in range(DA_HEADS)]

        def decay_beta():
            def matmul():
                ab = jnp.dot(h, wab_ref[...], preferred_element_type=F32)
                lane = lax.broadcasted_iota(jnp.int32, ab.shape, 1)
                g_l = -jnp.exp(alog_l_ref[...]) * _softplus(ab + dtb_l_ref[...])
                gcol_ref[rows, :] = jnp.where(lane < DN_HEADS, g_l, _sigmoid(ab))
                abt = lax.dot_general(wabt_ref[...], h, NT_DIMS,
                                      preferred_element_type=F32)
                row = lax.broadcasted_iota(jnp.int32, abt.shape, 0)
                g_s = -jnp.exp(alog_s_ref[...]) * _softplus(abt + dtb_s_ref[...])
                grow_ref[:, rows] = jnp.where(row < DN_HEADS, g_s, _sigmoid(abt))
            return matmul, []

        ident = lambda v: v
        return [conv_qkv(0, qa_ref), plain(wvb_ref, 0, vb_ref, ident),
                conv_qkv(1, ka_ref), plain(wza_ref, 0, za_ref, _silu),
                conv_qkv(2, va_ref), plain(wzb_ref, 0, zb_ref, _silu),
                qk_norm(wqb_ref, qnw_ref, qb_ref, DA_QK_DIM ** -0.5 * LOG2_E),
                qk_norm(wkb_ref, knw_ref, kb_ref, 1.0), decay_beta()]

    def run_tasks(task_list, extra=None):
        for n, (matmul, chunks) in enumerate(task_list):
            matmul()
            for fn in chunks:
                fn()
            if extra is not None and n == 6:
                extra()

    h_second = []
    run_tasks(tasks(normed(0), 0), extra=lambda: h_second.append(normed(half)))
    run_tasks(tasks(h_second[0], half))
    conv_buf[0:SUBLANES, :] = conv_buf[tm:tm + SUBLANES, :]


def _in_projection(x2, norm_w, w_in, conv_w, a_log, dt_bias, q_norm_w, k_norm_w, seq_len):
    m = x2.shape[0]
    tm = TM_IN
    assert m % tm == 0 and seq_len % tm == 0
    o = 0
    cols = {}
    for name, width in (("qkv", 3 * DN_WIDTH), ("za", DN_WIDTH), ("a", DN_HEADS), ("b", DN_HEADS),
                        ("qb", DA_QK_WIDTH), ("kb", DA_QK_WIDTH), ("vb", DA_WIDTH), ("zb", DA_WIDTH),
                        ("ga", D_MODEL), ("gb", D_MODEL)):
        cols[name] = w_in[:, o:o + width]
        o += width
    assert o == w_in.shape[1]
    wb = {k: v.astype(BF16) for k, v in cols.items()}
    w_ab = jnp.concatenate([cols["a"], cols["b"]], axis=1)
    w_ab_l = jnp.pad(w_ab, ((0, 0), (0, LANES - 2 * DN_HEADS))).astype(BF16)
    w_ab_s = jnp.pad(w_ab.T, ((0, AB_ROWS - 2 * DN_HEADS), (0, 0))).astype(BF16)
    alog_l = jnp.pad(a_log.astype(F32), (0, LANES - DN_HEADS)).reshape(1, LANES)
    dtb_l = jnp.pad(dt_bias.astype(F32), (0, LANES - DN_HEADS)).reshape(1, LANES)
    alog_s = jnp.pad(a_log.astype(F32), (0, AB_ROWS - DN_HEADS)).reshape(AB_ROWS, 1)
    dtb_s = jnp.pad(dt_bias.astype(F32), (0, AB_ROWS - DN_HEADS)).reshape(AB_ROWS, 1)
    qnw = jnp.tile(q_norm_w.astype(F32), 2).reshape(1, LANES)
    knw = jnp.tile(k_norm_w.astype(F32), 2).reshape(1, LANES)

    row_blk = lambda w: pl.BlockSpec((tm, w), lambda i: (i, 0))
    in_specs = [
        row_blk(D_MODEL), _resident((1, D_MODEL)),
        _resident((D_MODEL, 3 * DN_WIDTH)), _resident((CONV_WIDTH, 3 * DN_WIDTH)),
        _resident((D_MODEL, DN_WIDTH)), _resident((D_MODEL, LANES)), _resident((AB_ROWS, D_MODEL)),
        _resident((1, LANES)), _resident((1, LANES)), _resident((AB_ROWS, 1)), _resident((AB_ROWS, 1)),
        _resident((1, LANES)), _resident((1, LANES)),
        _resident((D_MODEL, DA_QK_WIDTH)), _resident((D_MODEL, DA_QK_WIDTH)),
        _resident((D_MODEL, DA_WIDTH)), _resident((D_MODEL, DA_WIDTH)),
    ]
    out_shapes = [jax.ShapeDtypeStruct((m, DN_WIDTH), BF16)] * 4 + [
        jax.ShapeDtypeStruct((m, LANES), F32), jax.ShapeDtypeStruct((AB_ROWS, m), F32)] + [
        jax.ShapeDtypeStruct((m, DA_WIDTH), BF16)] * 4
    out_specs = [row_blk(DN_WIDTH)] * 4 + [
        row_blk(LANES), pl.BlockSpec((AB_ROWS, tm), lambda i: (0, i))] + [
        row_blk(DA_WIDTH)] * 4
    outs = pl.pallas_call(
        functools.partial(_inproj_kernel, tiles_per_seq=seq_len // tm),
        grid=(m // tm,),
        in_specs=in_specs,
        out_specs=out_specs,
        out_shape=out_shapes,
        scratch_shapes=[pltpu.VMEM((tm + SUBLANES, 3 * DN_WIDTH), F32)],
        compiler_params=pltpu.CompilerParams(
            dimension_semantics=("arbitrary",), vmem_limit_bytes=VMEM_LIMIT),
        name="in_projection",
    )(x2, norm_w.astype(F32).reshape(1, D_MODEL), wb["qkv"], conv_w.astype(F32), wb["za"],
      w_ab_l, w_ab_s, alog_l, dtb_l, alog_s, dtb_s, qnw, knw,
      wb["qb"], wb["kb"], wb["vb"], wb["zb"])
    return outs, (wb["ga"], wb["gb"])


def _bmm(a, b):
    return jnp.einsum('nij,njk->nik', a, b, preferred_element_type=F32)


def _bmm_nt(a, b):
    return jnp.einsum('nid,njd->nij', a, b, preferred_element_type=F32)


def _split_bf16(x, parts):
    out = []
    for _ in range(parts - 1):
        hi = x.astype(BF16)
        out.append(hi)
        x = x - hi.astype(F32)
    out.append(x.astype(BF16))
    return out


def _deltanet_kernel(q_ref, k_ref, v_ref, z_ref, gcol_ref, grow_ref, nw_ref, o_ref,
                     state_ref, oraw_ref, lhs_ref, attn_ref, u_ref, kdec_ref, sdec_ref, *, nt):
    tb = q_ref.shape[0]
    c = CHUNK
    nc = tb // c
    nh = DN_HEADS
    t = pl.program_id(1)

    ri = lax.broadcasted_iota(jnp.int32, (c, c), 0)
    ci = lax.broadcasted_iota(jnp.int32, (c, c), 1)
    lower = ri >= ci
    strict = ri > ci
    eye = (ri == ci).astype(F32)
    tri = lower.astype(BF16)
    tri_t = (ri <= ci).astype(BF16)

    def heads_major(ref):
        return jnp.concatenate(
            [ref[:, hd * DN_HEAD_DIM:(hd + 1) * DN_HEAD_DIM].reshape(nc, c, DN_HEAD_DIM)
             for hd in range(nh)], axis=0)

    def phase1_stages():
        e = {}

        def decays():
            gb_col = gcol_ref[...].reshape(nc, c, LANES)
            tri_b = jnp.broadcast_to(tri, (nc, c, c))
            gc_col = sum(_bmm(tri_b, part) for part in _split_bf16(gb_col, 3))
            gb_row = grow_ref[...]
            gc_row = sum(jnp.dot(part, tri_t, preferred_element_type=F32)
                         for part in _split_bf16(gb_row.reshape(nc * AB_ROWS, c), 3)
                         ).reshape(nc, AB_ROWS, c)

            def col_form(src, lane):
                return jnp.concatenate(
                    [jnp.broadcast_to(src[:, :, lane + hd:lane + hd + 1], (nc, c, LANES))
                     for hd in range(nh)], axis=0)

            def row_form(src, row):
                return jnp.concatenate(
                    [src[:, row + hd:row + hd + 1, :] for hd in range(nh)], axis=0)

            e["g_c"] = col_form(gc_col, 0)
            e["beta_c"] = col_form(gb_col, nh)
            e["g_r"] = row_form(gc_row, 0)
            e["beta_r"] = row_form(gb_row, nh)
            e["g_last"] = e["g_c"][:, c - 1:c, :]
            diff = e["g_c"][:, :, :c] - e["g_r"]
            e["decay"] = jnp.where(lower, jnp.exp(jnp.where(lower, diff, 0.0)), 0.0)

        def gram_kk():
            e["q"], e["k"], e["v"] = heads_major(q_ref), heads_major(k_ref), heads_major(v_ref)
            l_strict = jnp.where(
                strict, _bmm_nt(e["k"], e["k"]) * e["decay"] * e["beta_c"][:, :, :c], 0.0)
            e["inv"] = eye - l_strict
            e["power"] = l_strict.astype(BF16)

        def gram_qk():
            e["attn"] = jnp.where(lower, _bmm_nt(e["q"], e["k"]) * e["decay"], 0.0).astype(BF16)

        def next_power():
            e["power"] = _bmm(e["power"], e["power"]).astype(BF16)

        def apply_power():
            e["inv"] = e["inv"] + _bmm(e["inv"].astype(BF16), e["power"])

        def solve_u():
            e["t_beta"] = e["inv"] * e["beta_r"]
            e["u"] = _bmm(e["t_beta"].astype(BF16), e["v"])

        def solve_w():
            e["w"] = _bmm((e["t_beta"] * jnp.exp(e["g_r"])).astype(BF16), e["k"]).astype(BF16)

        def commit():
            q_decay = (e["q"].astype(F32) * jnp.exp(e["g_c"])).astype(BF16)
            lhs_ref[...] = jnp.concatenate([e["w"], q_decay], axis=1)
            attn_ref[...] = e["attn"]
            u_ref[...] = e["u"]
            kdec_ref[...] = (e["k"].astype(F32) * jnp.exp(e["g_last"] - e["g_c"])).astype(BF16)
            sdec_ref[...] = jnp.exp(e["g_last"])

        levels = [next_power, apply_power] * (c.bit_length() - 2)
        return [decays, gram_kk, gram_qk] + levels + [solve_u, solve_w], commit

    def phase2_stages():
        def chunk_steps(n):
            idx = [hd * nc + n for hd in range(nh)]
            e = {}

            def through_state():
                e["s_old"] = [state_ref[hd] for hd in range(nh)]
                e["ws_qs"] = [jnp.dot(lhs_ref[b], e["s_old"][hd].astype(BF16),
                                      preferred_element_type=F32) for hd, b in enumerate(idx)]

            def update():
                v_new = [(u_ref[b] - e["ws_qs"][hd][:c]).astype(BF16) for hd, b in enumerate(idx)]
                for hd, b in enumerate(idx):
                    oraw_ref[n * c:(n + 1) * c, hd * DN_HEAD_DIM:(hd + 1) * DN_HEAD_DIM] = (
                        e["ws_qs"][hd][c:]
                        + jnp.dot(attn_ref[b], v_new[hd], preferred_element_type=F32))
                for hd, b in enumerate(idx):
                    state_ref[hd] = e["s_old"][hd] * sdec_ref[b] + lax.dot_general(
                        kdec_ref[b], v_new[hd], TN_DIMS, preferred_element_type=F32)
            return [through_state, update]

        def gated_norm():
            for hd in range(nh):
                hs = slice(hd * DN_HEAD_DIM, (hd + 1) * DN_HEAD_DIM)
                o = oraw_ref[:, hs]
                ms = jnp.mean(o * o, axis=-1, keepdims=True)
                o = o * lax.rsqrt(ms + EPS) * nw_ref[...]
                o_ref[:, hs] = (o * z_ref[:, hs].astype(F32)).astype(BF16)

        return [fn for n in range(nc) for fn in chunk_steps(n)] + [gated_norm]

    @pl.when(t == 0)
    def _():
        state_ref[...] = jnp.zeros(state_ref.shape, F32)
        stages, commit = phase1_stages()
        for fn in stages:
            fn()
        commit()

    @pl.when((t > 0) & (t < nt))
    def _():
        stages, commit = phase1_stages()
        others = phase2_stages()
        done = 0
        for n, fn in enumerate(stages):
            fn()
            upto = -(-(n + 1) * len(others) // len(stages))
            for other in others[done:upto]:
                other()
            done = upto
        commit()

    @pl.when(t == nt)
    def _():
        for fn in phase2_stages():
            fn()


def _deltanet(qa, ka, va, za, gcol, grow, dn_norm_w, batch, seq_len):
    tb = TB_DN
    assert seq_len % tb == 0 and tb % CHUNK == 0
    nt = seq_len // tb
    nc = tb // CHUNK
    nb = DN_HEADS * nc
    grow3 = grow.reshape(AB_ROWS, -1, CHUNK).transpose(1, 0, 2)
    cur = lambda w: pl.BlockSpec((tb, w), lambda b, t: (b * nt + jnp.minimum(t, nt - 1), 0))
    prev = lambda w: pl.BlockSpec((tb, w), lambda b, t: (b * nt + jnp.maximum(t - 1, 0), 0))
    return pl.pallas_call(
        functools.partial(_deltanet_kernel, nt=nt),
        grid=(batch, nt + 1),
        in_specs=[cur(DN_WIDTH)] * 3 + [
            prev(DN_WIDTH), cur(LANES),
            pl.BlockSpec((nc, AB_ROWS, CHUNK),
                         lambda b, t: (b * nt + jnp.minimum(t, nt - 1), 0, 0)),
            _resident((1, DN_HEAD_DIM))],
        out_specs=prev(DN_WIDTH),
        out_shape=jax.ShapeDtypeStruct((batch * seq_len, DN_WIDTH), BF16),
        scratch_shapes=[pltpu.VMEM((DN_HEADS, DN_HEAD_DIM, DN_HEAD_DIM), F32),
                        pltpu.VMEM((tb, DN_WIDTH), F32),
                        pltpu.VMEM((nb, 2 * CHUNK, DN_HEAD_DIM), BF16),
                        pltpu.VMEM((nb, CHUNK, CHUNK), BF16),
                        pltpu.VMEM((nb, CHUNK, DN_HEAD_DIM), F32),
                        pltpu.VMEM((nb, CHUNK, DN_HEAD_DIM), BF16),
                        pltpu.VMEM((nb, 1, DN_HEAD_DIM), F32)],
        compiler_params=pltpu.CompilerParams(
            dimension_semantics=("arbitrary", "arbitrary"), vmem_limit_bytes=VMEM_LIMIT),
        name="gated_deltanet",
    )(qa, ka, va, za, gcol, grow3, dn_norm_w.astype(F32).reshape(1, DN_HEAD_DIM))


def _diffattn_kernel(lq1_ref, lk1_ref, lq2_ref, lk2_ref, q_ref, k_ref, v_ref, z_ref, nw_ref,
                     o_ref, m_ref, l_ref, acc_ref):
    tq = q_ref.shape[0]
    tk = TK_DA
    qi = pl.program_id(2)
    chains = [(hd, comp) for hd in range(HEADS_DA) for comp in range(2)]

    lane = lax.broadcasted_iota(jnp.int32, (1, LANES), 1)
    first_half = lane < DA_QK_DIM
    q_chain = []
    for hd in range(HEADS_DA):
        q = q_ref[:, hd * LANES:(hd + 1) * LANES]
        zero = jnp.zeros_like(q)
        q_chain += [jnp.where(first_half, q, zero), jnp.where(first_half, zero, q)]

    def attend(start, row0=0, nrows=None, width=None, masked=False, first=False):
        nrows = tq if nrows is None else nrows
        width = tk if width is None else width
        rows = slice(row0, row0 + nrows)
        kblk = [k_ref[pl.ds(start, width), hd * LANES:(hd + 1) * LANES] for hd in range(HEADS_DA)]
        ones = jnp.ones((width, LANES), BF16)
        vblk = [jnp.concatenate([v_ref[pl.ds(start, width), hd * LANES:(hd + 1) * LANES], ones],
                                axis=1) for hd in range(HEADS_DA)]
        scores = [lax.dot_general(q_chain[c][rows], kblk[hd], NT_DIMS, preferred_element_type=F32)
                  for c, (hd, _) in enumerate(chains)]
        if masked:
            rq = lax.broadcasted_iota(jnp.int32, (nrows, width), 0) + row0
            ck = lax.broadcasted_iota(jnp.int32, (nrows, width), 1)
            scores = [jnp.where(ck <= rq, s, NEG_INF) for s in scores]
        probs = []
        for c, s in enumerate(scores):
            m_new = jnp.max(s, axis=-1, keepdims=True)
            if first:
                m_new = jnp.broadcast_to(m_new, (nrows, LANES))
            else:
                m_prev = m_ref[c, rows, :]
                m_new = jnp.maximum(m_prev, m_new)
                alpha = jnp.exp2(m_prev - m_new)
                l_ref[c, rows, :] = alpha * l_ref[c, rows, :]
                acc_ref[c, rows, :] = alpha * acc_ref[c, rows, :]
            probs.append(jnp.exp2(
                (s - jnp.concatenate([m_new] * (width // LANES), axis=1)).astype(BF16)))
            m_ref[c, rows, :] = m_new
        for c, (hd, _) in enumerate(chains):
            pv = jnp.dot(probs[c], vblk[hd], preferred_element_type=F32)
            if first:
                acc_ref[c, rows, :] = pv[:, :DA_V_DIM]
                l_ref[c, rows, :] = pv[:, DA_V_DIM:]
            else:
                acc_ref[c, rows, :] += pv[:, :DA_V_DIM]
                l_ref[c, rows, :] += pv[:, DA_V_DIM:]

    diag0 = pl.multiple_of(qi * tq, tq)
    for r0 in range(0, tq, DIAG_BAND):
        attend(diag0, row0=r0, nrows=DIAG_BAND, width=r0 + DIAG_BAND, masked=True, first=True)

    def body(j, carry):
        attend(pl.multiple_of(j * tk, tk))
        return carry

    lax.fori_loop(0, qi * (tq // tk), body, 0)

    lam = (jnp.exp(jnp.sum(lq1_ref[...] * lk1_ref[...], axis=-1, keepdims=True))
           - jnp.exp(jnp.sum(lq2_ref[...] * lk2_ref[...], axis=-1, keepdims=True))
           + LAMBDA_INIT)
    for hd in range(HEADS_DA):
        hs = slice(hd * LANES, (hd + 1) * LANES)
        o = acc_ref[2 * hd] / l_ref[2 * hd] - lam * (acc_ref[2 * hd + 1] / l_ref[2 * hd + 1])
        ms = jnp.mean(o * o, axis=-1, keepdims=True)
        o = o * lax.rsqrt(ms + EPS) * nw_ref[...] * (1.0 - LAMBDA_INIT)
        o_ref[:, hs] = (o * z_ref[:, hs].astype(F32)).astype(BF16)


def _diff_attention(qb, kb, vb, zb, lambda_q1, lambda_k1, lambda_q2, lambda_k2, da_norm_w,
                    batch, seq_len):
    tq = TQ_DA
    assert tq % TK_DA == 0 and tq % DIAG_BAND == 0 and seq_len % tq == 0 and DA_HEADS % HEADS_DA == 0
    nq = seq_len // tq
    width = HEADS_DA * LANES
    lam_vec = lambda v: v.astype(F32).reshape(1, DA_QK_DIM)
    q_blk = pl.BlockSpec((tq, width), lambda b, h, i: (b * nq + i, h))
    kv_blk = pl.BlockSpec((seq_len, width), lambda b, h, i: (b, h))
    stat = pltpu.VMEM((2 * HEADS_DA, tq, LANES), F32)
    return pl.pallas_call(
        _diffattn_kernel,
        grid=(batch, DA_HEADS // HEADS_DA, nq),
        in_specs=[_resident((1, DA_QK_DIM))] * 4 + [q_blk, kv_blk, kv_blk, q_blk,
                                                    _resident((1, DA_V_DIM))],
        out_specs=q_blk,
        out_shape=jax.ShapeDtypeStruct((batch * seq_len, DA_WIDTH), BF16),
        scratch_shapes=[stat, stat, pltpu.VMEM((2 * HEADS_DA, tq, DA_V_DIM), F32)],
        compiler_params=pltpu.CompilerParams(
            dimension_semantics=("arbitrary", "arbitrary", "arbitrary"),
            vmem_limit_bytes=VMEM_LIMIT),
        name="diff_attention",
    )(lam_vec(lambda_q1), lam_vec(lambda_k1), lam_vec(lambda_q2), lam_vec(lambda_k2),
      qb, kb, vb, zb, da_norm_w.astype(F32).reshape(1, DA_V_DIM))


def _output_kernel(oa_ref, ob_ref, x_ref, nw_ref, wga_ref, wgb_ref, woa_ref, wob_ref, wo_ref,
                   out_ref):
    x = x_ref[...]
    ms = jnp.mean(x * x, axis=-1, keepdims=True)
    h = (x * lax.rsqrt(ms + EPS) * nw_ref[...]).astype(BF16)
    gate_a = _sigmoid(jnp.dot(h, wga_ref[...], preferred_element_type=F32))
    gate_b = _sigmoid(jnp.dot(h, wgb_ref[...], preferred_element_type=F32))
    y_a = jnp.dot(oa_ref[...], woa_ref[...], preferred_element_type=F32)
    y_b = jnp.dot(ob_ref[...], wob_ref[...], preferred_element_type=F32)
    y = gate_a * y_a + gate_b * y_b
    out_ref[...] = x + jnp.dot(y.astype(BF16), wo_ref[...], preferred_element_type=F32)


def _output_projection(oa, ob, x2, norm_w, w_gates, w_out_a, w_out_b, w_out):
    m = x2.shape[0]
    tm = TM_OUT
    assert m % tm == 0
    row_blk = lambda w: pl.BlockSpec((tm, w), lambda i: (i, 0))
    return pl.pallas_call(
        _output_kernel,
        grid=(m // tm,),
        in_specs=[row_blk(DN_WIDTH), row_blk(DA_WIDTH), row_blk(D_MODEL), _resident((1, D_MODEL)),
                  _resident((D_MODEL, D_MODEL)), _resident((D_MODEL, D_MODEL)),
                  _resident((DN_WIDTH, D_MODEL)), _resident((DA_WIDTH, D_MODEL)),
                  _resident((D_MODEL, D_MODEL))],
        out_specs=row_blk(D_MODEL),
        out_shape=jax.ShapeDtypeStruct((m, D_MODEL), F32),
        compiler_params=pltpu.CompilerParams(
            dimension_semantics=("arbitrary",), vmem_limit_bytes=VMEM_LIMIT),
        name="output_projection",
    )(oa, ob, x2, norm_w.astype(F32).reshape(1, D_MODEL), *w_gates,
      w_out_a.astype(BF16), w_out_b.astype(BF16), w_out.astype(BF16))


def kernel(x, norm_w, w_in, conv_w, a_log, dt_bias, dn_norm_w, q_norm_w, k_norm_w,
           lambda_q1, lambda_k1, lambda_q2, lambda_k2, da_norm_w, w_out_a, w_out_b, w_out):
    batch, seq_len, d_model = x.shape
    assert d_model == D_MODEL
    x2 = x.reshape(batch * seq_len, d_model)
    (qa, ka, va, za, gcol, grow, qb, kb, vb, zb), w_gates = _in_projection(
        x2, norm_w, w_in, conv_w, a_log, dt_bias, q_norm_w, k_norm_w, seq_len)
    oa = _deltanet(qa, ka, va, za, gcol, grow, dn_norm_w, batch, seq_len)
    ob = _diff_attention(qb, kb, vb, zb, lambda_q1, lambda_k1, lambda_q2, lambda_k2, da_norm_w,
                         batch, seq_len)
    out = _output_projection(oa, ob, x2, norm_w, w_gates, w_out_a, w_out_b, w_out)
    return out.reshape(batch, seq_len, d_model)
```

```python
import functools
import math

import jax
import jax.numpy as jnp
from jax import lax
from jax.experimental import pallas as pl
from jax.experimental.pallas import tpu as pltpu

F32 = jnp.float32
BF16 = jnp.bfloat16

D_MODEL = 1024
DN_HEADS = 4
DN_HEAD_DIM = 128
DN_WIDTH = DN_HEADS * DN_HEAD_DIM
CONV_WIDTH = 4
CHUNK = 64
DA_HEADS = 4
DA_QK_DIM = 64
DA_V_DIM = 2 * DA_QK_DIM
DA_WIDTH = DA_HEADS * DA_V_DIM
DA_QK_WIDTH = DA_HEADS * 2 * DA_QK_DIM
LAMBDA_INIT = 0.8 - 0.6 * math.exp(-0.3 * 0)
EPS = 1e-6
NEG_INF = -1e30
LOG2_E = math.log2(math.e)

LANES = 128
MXU_COLS = 256
SUBLANES = 8
BF16_ROWS = 16
AB_ROWS = BF16_ROWS
VMEM_LIMIT = 48 * 1024 * 1024

TM_IN = 512
TB_DN = 512
TQ_DA = 1024
DIAG_BAND = 256
TK_DA = 1024
HEADS_DA = 2
TM_OUT = 512

NT_DIMS = (((1,), (1,)), ((), ()))
TN_DIMS = (((0,), (0,)), ((), ()))


def _sigmoid(v):
    return 0.5 + 0.5 * jnp.tanh(0.5 * v)


def _silu(v):
    h = 0.5 * v
    return h + h * jnp.tanh(h)


def _softplus(v):
    return jnp.maximum(v, 0.0) + jnp.log1p(jnp.exp(-jnp.abs(v)))


def _resident(shape):
    zeros = (0,) * len(shape)
    return pl.BlockSpec(shape, lambda *_: zeros, pipeline_mode=pl.Buffered(1))


def _inproj_kernel(x_ref, nw_ref, wqkv_ref, convw_ref, wza_ref, wab_ref, wabt_ref,
                   alog_l_ref, dtb_l_ref, alog_s_ref, dtb_s_ref, qnw_ref, knw_ref,
                   wqb_ref, wkb_ref, wvb_ref, wzb_ref,
                   qa_ref, ka_ref, va_ref, za_ref, gcol_ref, grow_ref,
                   qb_ref, kb_ref, vb_ref, zb_ref,
                   conv_buf, *, tiles_per_seq):
    tm = x_ref.shape[0]
    half = tm // 2
    i = pl.program_id(0)
    head0 = SUBLANES - (CONV_WIDTH - 1)
    slab = 512

    @pl.when(i % tiles_per_seq == 0)
    def _():
        conv_buf[0:SUBLANES, :] = jnp.zeros((SUBLANES, 3 * DN_WIDTH), F32)

    def normed(r0):
        x = x_ref[r0:r0 + half, :]
        ms = jnp.mean(x * x, axis=-1, keepdims=True)
        return (x * lax.rsqrt(ms + EPS) * nw_ref[...]).astype(BF16)

    def tasks(h, r0):
        rows = slice(r0, r0 + half)

        def conv_qkv(s, out_ref):
            cs = slice(s * DN_WIDTH, (s + 1) * DN_WIDTH)

            res = []

            def matmul():
                res.append(jnp.dot(h, wqkv_ref[:, cs], preferred_element_type=F32))
                conv_buf[SUBLANES + r0:SUBLANES + r0 + half, cs] = res[0]

            def epilogue():
                acc = convw_ref[CONV_WIDTH - 1:CONV_WIDTH, cs] * res[0]
                for j in range(CONV_WIDTH - 1):
                    acc = acc + (convw_ref[j:j + 1, cs]
                                 * conv_buf[head0 + j + r0:head0 + j + r0 + half, cs])
                y = _silu(acc)
                if s == 2:
                    out_ref[rows, :] = y.astype(BF16)
                    return
                scale = DN_HEAD_DIM ** -0.5 if s == 0 else 1.0
                for hd in range(DN_HEADS):
                    hs = slice(hd * DN_HEAD_DIM, (hd + 1) * DN_HEAD_DIM)
                    blk = y[:, hs]
                    ss = jnp.sum(blk * blk, axis=-1, keepdims=True)
                    out_ref[rows, hs] = (blk * (lax.rsqrt(ss + EPS) * scale)).astype(BF16)
            return matmul, [epilogue]

        def plain(w_ref, c0, out_ref, act):
            res = []

            def matmul():
                res.append(jnp.dot(h, w_ref[:, c0:c0 + slab], preferred_element_type=F32))

            def chunk(n):
                def run():
                    cs = slice(n * LANES, (n + 1) * LANES)
                    out_ref[rows, c0 + n * LANES:c0 + (n + 1) * LANES] = (
                        act(res[0][:, cs]).astype(BF16))
                return run
            return matmul, [chunk(n) for n in range(slab // LANES)]

        def qk_norm(w_ref, nrm_ref, out_ref, scale):
            res = []

            def matmul():
                res.append(jnp.dot(h, w_ref[...], preferred_element_type=F32))

            def chunk(hd):
                def run():
                    lo = lax.broadcasted_iota(jnp.int32, (half, LANES), 1) < DA_QK_DIM
                    blk = res[0][:, hd * LANES:(hd + 1) * LANES]
                    sq = blk * blk
                    s_lo = jnp.sum(jnp.where(lo, sq, 0.0), axis=-1, keepdims=True)
                    s_hi = jnp.sum(jnp.where(lo, 0.0, sq), axis=-1, keepdims=True)
                    msq = jnp.where(lo, s_lo, s_hi) * (1.0 / DA_QK_DIM)
                    out_ref[rows, hd * LANES:(hd + 1) * LANES] = (
                        blk * lax.rsqrt(msq + EPS) * nrm_ref[...] * scale).astype(BF16)
                return run
            return matmul, [chunk(hd) for hd in range(DA_HEADS)]

        def decay_beta():
            def matmul():
                ab = jnp.dot(h, wab_ref[...], preferred_element_type=F32)
                lane = lax.broadcasted_iota(jnp.int32, ab.shape, 1)
                g_l = -jnp.exp(alog_l_ref[...]) * _softplus(ab + dtb_l_ref[...])
                gcol_ref[rows, :] = jnp.where(lane < DN_HEADS, g_l, _sigmoid(ab))
                abt = lax.dot_general(wabt_ref[...], h, NT_DIMS,
                                      preferred_element_type=F32)
                row = lax.broadcasted_iota(jnp.int32, abt.shape, 0)
                g_s = -jnp.exp(alog_s_ref[...]) * _softplus(abt + dtb_s_ref[...])
                grow_ref[:, rows] = jnp.where(row < DN_HEADS, g_s, _sigmoid(abt))
            return matmul, []

        ident = lambda v: v
        return [conv_qkv(0, qa_ref), plain(wvb_ref, 0, vb_ref, ident),
                conv_qkv(1, ka_ref), plain(wza_ref, 0, za_ref, _silu),
                conv_qkv(2, va_ref), plain(wzb_ref, 0, zb_ref, _silu),
                qk_norm(wqb_ref, qnw_ref, qb_ref, DA_QK_DIM ** -0.5 * LOG2_E),
                qk_norm(wkb_ref, knw_ref, kb_ref, 1.0), decay_beta()]

    def run_tasks(task_list, extra=None):
        for n, (matmul, chunks) in enumerate(task_list):
            matmul()
            for fn in chunks:
                fn()
            if extra is not None and n == 6:
                extra()

    h_second = []
    run_tasks(tasks(normed(0), 0), extra=lambda: h_second.append(normed(half)))
    run_tasks(tasks(h_second[0], half))
    conv_buf[0:SUBLANES, :] = conv_buf[tm:tm + SUBLANES, :]


def _in_projection(x2, norm_w, w_in, conv_w, a_log, dt_bias, q_norm_w, k_norm_w, seq_len):
    m = x2.shape[0]
    tm = TM_IN
    assert m % tm == 0 and seq_len % tm == 0
    o = 0
    cols = {}
    for name, width in (("qkv", 3 * DN_WIDTH), ("za", DN_WIDTH), ("a", DN_HEADS), ("b", DN_HEADS),
                        ("qb", DA_QK_WIDTH), ("kb", DA_QK_WIDTH), ("vb", DA_WIDTH), ("zb", DA_WIDTH),
                        ("ga", D_MODEL), ("gb", D_MODEL)):
        cols[name] = w_in[:, o:o + width]
        o += width
    assert o == w_in.shape[1]
    wb = {k: v.astype(BF16) for k, v in cols.items()}
    w_ab = jnp.concatenate([cols["a"], cols["b"]], axis=1)
    w_ab_l = jnp.pad(w_ab, ((0, 0), (0, LANES - 2 * DN_HEADS))).astype(BF16)
    w_ab_s = jnp.pad(w_ab.T, ((0, AB_ROWS - 2 * DN_HEADS), (0, 0))).astype(BF16)
    alog_l = jnp.pad(a_log.astype(F32), (0, LANES - DN_HEADS)).reshape(1, LANES)
    dtb_l = jnp.pad(dt_bias.astype(F32), (0, LANES - DN_HEADS)).reshape(1, LANES)
    alog_s = jnp.pad(a_log.astype(F32), (0, AB_ROWS - DN_HEADS)).reshape(AB_ROWS, 1)
    dtb_s = jnp.pad(dt_bias.astype(F32), (0, AB_ROWS - DN_HEADS)).reshape(AB_ROWS, 1)
    qnw = jnp.tile(q_norm_w.astype(F32), 2).reshape(1, LANES)
    knw = jnp.tile(k_norm_w.astype(F32), 2).reshape(1, LANES)

    row_blk = lambda w: pl.BlockSpec((tm, w), lambda i: (i, 0))
    in_specs = [
        row_blk(D_MODEL), _resident((1, D_MODEL)),
        _resident((D_MODEL, 3 * DN_WIDTH)), _resident((CONV_WIDTH, 3 * DN_WIDTH)),
        _resident((D_MODEL, DN_WIDTH)), _resident((D_MODEL, LANES)), _resident((AB_ROWS, D_MODEL)),
        _resident((1, LANES)), _resident((1, LANES)), _resident((AB_ROWS, 1)), _resident((AB_ROWS, 1)),
        _resident((1, LANES)), _resident((1, LANES)),
        _resident((D_MODEL, DA_QK_WIDTH)), _resident((D_MODEL, DA_QK_WIDTH)),
        _resident((D_MODEL, DA_WIDTH)), _resident((D_MODEL, DA_WIDTH)),
    ]
    out_shapes = [jax.ShapeDtypeStruct((m, DN_WIDTH), BF16)] * 4 + [
        jax.ShapeDtypeStruct((m, LANES), F32), jax.ShapeDtypeStruct((AB_ROWS, m), F32)] + [
        jax.ShapeDtypeStruct((m, DA_WIDTH), BF16)] * 4
    out_specs = [row_blk(DN_WIDTH)] * 4 + [
        row_blk(LANES), pl.BlockSpec((AB_ROWS, tm), lambda i: (0, i))] + [
        row_blk(DA_WIDTH)] * 4
    outs = pl.pallas_call(
        functools.partial(_inproj_kernel, tiles_per_seq=seq_len // tm),
        grid=(m // tm,),
        in_specs=in_specs,
        out_specs=out_specs,
        out_shape=out_shapes,
        scratch_shapes=[pltpu.VMEM((tm + SUBLANES, 3 * DN_WIDTH), F32)],
        compiler_params=pltpu.CompilerParams(
            dimension_semantics=("arbitrary",), vmem_limit_bytes=VMEM_LIMIT),
        name="in_projection",
    )(x2, norm_w.astype(F32).reshape(1, D_MODEL), wb["qkv"], conv_w.astype(F32), wb["za"],
      w_ab_l, w_ab_s, alog_l, dtb_l, alog_s, dtb_s, qnw, knw,
      wb["qb"], wb["kb"], wb["vb"], wb["zb"])
    return outs, (wb["ga"], wb["gb"])


def _bmm(a, b):
    return jnp.einsum('nij,njk->nik', a, b, preferred_element_type=F32)


def _bmm_nt(a, b):
    return jnp.einsum('nid,njd->nij', a, b, preferred_element_type=F32)


def _split_bf16(x, parts):
    out = []
    for _ in range(parts - 1):
        hi = x.astype(BF16)
        out.append(hi)
        x = x - hi.astype(F32)
    out.append(x.astype(BF16))
    return out


def _deltanet_kernel(q_ref, k_ref, v_ref, z_ref, gcol_ref, grow_ref, nw_ref, o_ref,
                     state_ref, oraw_ref, lhs_ref, attn_ref, u_ref, kdec_ref, sdec_ref, *, nt):
    tb = q_ref.shape[0]
    c = CHUNK
    nc = tb // c
    nh = DN_HEADS
    t = pl.program_id(1)

    ri = lax.broadcasted_iota(jnp.int32, (c, c), 0)
    ci = lax.broadcasted_iota(jnp.int32, (c, c), 1)
    lower = ri >= ci
    strict = ri > ci
    eye = (ri == ci).astype(F32)
    tri = lower.astype(BF16)
    tri_t = (ri <= ci).astype(BF16)

    def heads_major(ref):
        return jnp.concatenate(
            [ref[:, hd * DN_HEAD_DIM:(hd + 1) * DN_HEAD_DIM].reshape(nc, c, DN_HEAD_DIM)
             for hd in range(nh)], axis=0)

    def phase1_stages():
        e = {}

        def decays():
            gb_col = gcol_ref[...].reshape(nc, c, LANES)
            tri_b = jnp.broadcast_to(tri, (nc, c, c))
            gc_col = sum(_bmm(tri_b, part) for part in _split_bf16(gb_col, 3))
            gb_row = grow_ref[...]
            gc_row = sum(jnp.dot(part, tri_t, preferred_element_type=F32)
                         for part in _split_bf16(gb_row.reshape(nc * AB_ROWS, c), 3)
                         ).reshape(nc, AB_ROWS, c)

            def col_form(src, lane):
                return jnp.concatenate(
                    [jnp.broadcast_to(src[:, :, lane + hd:lane + hd + 1], (nc, c, LANES))
                     for hd in range(nh)], axis=0)

            def row_form(src, row):
                return jnp.concatenate(
                    [src[:, row + hd:row + hd + 1, :] for hd in range(nh)], axis=0)

            e["g_c"] = col_form(gc_col, 0)
            e["beta_c"] = col_form(gb_col, nh)
            e["g_r"] = row_form(gc_row, 0)
            e["beta_r"] = row_form(gb_row, nh)
            e["g_last"] = e["g_c"][:, c - 1:c, :]
            diff = e["g_c"][:, :, :c] - e["g_r"]
            e["decay"] = jnp.where(lower, jnp.exp(jnp.where(lower, diff, 0.0)), 0.0)

        def gram_kk():
            e["q"], e["k"], e["v"] = heads_major(q_ref), heads_major(k_ref), heads_major(v_ref)
            l_strict = jnp.where(
                strict, _bmm_nt(e["k"], e["k"]) * e["decay"] * e["beta_c"][:, :, :c], 0.0)
            e["inv"] = eye - l_strict
            e["power"] = l_strict.astype(BF16)

        def gram_qk():
            e["attn"] = jnp.where(lower, _bmm_nt(e["q"], e["k"]) * e["decay"], 0.0).astype(BF16)

        def next_power():
            e["power"] = _bmm(e["power"], e["power"]).astype(BF16)

        def apply_power():
            e["inv"] = e["inv"] + _bmm(e["inv"].astype(BF16), e["power"])

        def solve_u():
            e["t_beta"] = e["inv"] * e["beta_r"]
            e["u"] = _bmm(e["t_beta"].astype(BF16), e["v"])

        def solve_w():
            e["w"] = _bmm((e["t_beta"] * jnp.exp(e["g_r"])).astype(BF16), e["k"]).astype(BF16)

        def commit():
            q_decay = (e["q"].astype(F32) * jnp.exp(e["g_c"])).astype(BF16)
            lhs_ref[...] = jnp.concatenate([e["w"], q_decay], axis=1)
            attn_ref[...] = e["attn"]
            u_ref[...] = e["u"]
            kdec_ref[...] = (e["k"].astype(F32) * jnp.exp(e["g_last"] - e["g_c"])).astype(BF16)
            sdec_ref[...] = jnp.exp(e["g_last"])

        levels = [next_power, apply_power] * (c.bit_length() - 2)
        return [decays, gram_kk, gram_qk] + levels + [solve_u, solve_w], commit

    def phase2_stages():
        def chunk_steps(n):
            idx = [hd * nc + n for hd in range(nh)]
            e = {}

            def through_state():
                e["s_old"] = [state_ref[hd] for hd in range(nh)]
                e["ws_qs"] = [jnp.dot(lhs_ref[b], e["s_old"][hd].astype(BF16),
                                      preferred_element_type=F32) for hd, b in enumerate(idx)]

            def update():
                v_new = [(u_ref[b] - e["ws_qs"][hd][:c]).astype(BF16) for hd, b in enumerate(idx)]
                for hd, b in enumerate(idx):
                    oraw_ref[n * c:(n + 1) * c, hd * DN_HEAD_DIM:(hd + 1) * DN_HEAD_DIM] = (
                        e["ws_qs"][hd][c:]
                        + jnp.dot(attn_ref[b], v_new[hd], preferred_element_type=F32))
                for hd, b in enumerate(idx):
                    state_ref[hd] = e["s_old"][hd] * sdec_ref[b] + lax.dot_general(
                        kdec_ref[b], v_new[hd], TN_DIMS, preferred_element_type=F32)
            return [through_state, update]

        def gated_norm():
            for hd in range(nh):
                hs = slice(hd * DN_HEAD_DIM, (hd + 1) * DN_HEAD_DIM)
                o = oraw_ref[:, hs]
                ms = jnp.mean(o * o, axis=-1, keepdims=True)
                o = o * lax.rsqrt(ms + EPS) * nw_ref[...]
                o_ref[:, hs] = (o * z_ref[:, hs].astype(F32)).astype(BF16)

        return [fn for n in range(nc) for fn in chunk_steps(n)] + [gated_norm]

    @pl.when(t == 0)
    def _():
        state_ref[...] = jnp.zeros(state_ref.shape, F32)
        stages, commit = phase1_stages()
        for fn in stages:
            fn()
        commit()

    @pl.when((t > 0) & (t < nt))
    def _():
        stages, commit = phase1_stages()
        others = phase2_stages()
        done = 0
        for n, fn in enumerate(stages):
            fn()
            upto = -(-(n + 1) * len(others) // len(stages))
            for other in others[done:upto]:
                other()
            done = upto
        commit()

    @pl.when(t == nt)
    def _():
        for fn in phase2_stages():
            fn()


def _deltanet(qa, ka, va, za, gcol, grow, dn_norm_w, batch, seq_len):
    tb = TB_DN
    assert seq_len % tb == 0 and tb % CHUNK == 0
    nt = seq_len // tb
    nc = tb // CHUNK
    nb = DN_HEADS * nc
    grow3 = grow.reshape(AB_ROWS, -1, CHUNK).transpose(1, 0, 2)
    cur = lambda w: pl.BlockSpec((tb, w), lambda b, t: (b * nt + jnp.minimum(t, nt - 1), 0))
    prev = lambda w: pl.BlockSpec((tb, w), lambda b, t: (b * nt + jnp.maximum(t - 1, 0), 0))
    return pl.pallas_call(
        functools.partial(_deltanet_kernel, nt=nt),
        grid=(batch, nt + 1),
        in_specs=[cur(DN_WIDTH)] * 3 + [
            prev(DN_WIDTH), cur(LANES),
            pl.BlockSpec((nc, AB_ROWS, CHUNK),
                         lambda b, t: (b * nt + jnp.minimum(t, nt - 1), 0, 0)),
            _resident((1, DN_HEAD_DIM))],
        out_specs=prev(DN_WIDTH),
        out_shape=jax.ShapeDtypeStruct((batch * seq_len, DN_WIDTH), BF16),
        scratch_shapes=[pltpu.VMEM((DN_HEADS, DN_HEAD_DIM, DN_HEAD_DIM), F32),
                        pltpu.VMEM((tb, DN_WIDTH), F32),
                        pltpu.VMEM((nb, 2 * CHUNK, DN_HEAD_DIM), BF16),
                        pltpu.VMEM((nb, CHUNK, CHUNK), BF16),
                        pltpu.VMEM((nb, CHUNK, DN_HEAD_DIM), F32),
                        pltpu.VMEM((nb, CHUNK, DN_HEAD_DIM), BF16),
                        pltpu.VMEM((nb, 1, DN_HEAD_DIM), F32)],
        compiler_params=pltpu.CompilerParams(
            dimension_semantics=("arbitrary", "arbitrary"), vmem_limit_bytes=VMEM_LIMIT),
        name="gated_deltanet",
    )(qa, ka, va, za, gcol, grow3, dn_norm_w.astype(F32).reshape(1, DN_HEAD_DIM))


def _diffattn_kernel(lq1_ref, lk1_ref, lq2_ref, lk2_ref, q_ref, k_ref, v_ref, z_ref, nw_ref,
                     o_ref, m_ref, l_ref, acc_ref):
    tq = q_ref.shape[0]
    tk = TK_DA
    qi = pl.program_id(2)
    chains = [(hd, comp) for hd in range(HEADS_DA) for comp in range(2)]

    lane = lax.broadcasted_iota(jnp.int32, (1, LANES), 1)
    first_half = lane < DA_QK_DIM
    q_chain = []
    for hd in range(HEADS_DA):
        q = q_ref[:, hd * LANES:(hd + 1) * LANES]
        zero = jnp.zeros_like(q)
        q_chain += [jnp.where(first_half, q, zero), jnp.where(first_half, zero, q)]

    def attend(start, row0=0, nrows=None, width=None, masked=False, first=False):
        nrows = tq if nrows is None else nrows
        width = tk if width is None else width
        rows = slice(row0, row0 + nrows)
        kblk = [k_ref[pl.ds(start, width), hd * LANES:(hd + 1) * LANES] for hd in range(HEADS_DA)]
        ones = jnp.ones((width, LANES), BF16)
        vblk = [jnp.concatenate([v_ref[pl.ds(start, width), hd * LANES:(hd + 1) * LANES], ones],
                                axis=1) for hd in range(HEADS_DA)]
        scores = [lax.dot_general(q_chain[c][rows], kblk[hd], NT_DIMS, preferred_element_type=F32)
                  for c, (hd, _) in enumerate(chains)]
        if masked:
            rq = lax.broadcasted_iota(jnp.int32, (nrows, width), 0) + row0
            ck = lax.broadcasted_iota(jnp.int32, (nrows, width), 1)
            scores = [jnp.where(ck <= rq, s, NEG_INF) for s in scores]
        probs = []
        for c, s in enumerate(scores):
            m_new = jnp.max(s, axis=-1, keepdims=True)
            if first:
                m_new = jnp.broadcast_to(m_new, (nrows, LANES))
            else:
                m_prev = m_ref[c, rows, :]
                m_new = jnp.maximum(m_prev, m_new)
                alpha = jnp.exp2(m_prev - m_new)
                l_ref[c, rows, :] = alpha * l_ref[c, rows, :]
                acc_ref[c, rows, :] = alpha * acc_ref[c, rows, :]
            probs.append(jnp.exp2(
                (s - jnp.concatenate([m_new] * (width // LANES), axis=1)).astype(BF16)))
            m_ref[c, rows, :] = m_new
        for c, (hd, _) in enumerate(chains):
            pv = jnp.dot(probs[c], vblk[hd], preferred_element_type=F32)
            if first:
                acc_ref[c, rows, :] = pv[:, :DA_V_DIM]
                l_ref[c, rows, :] = pv[:, DA_V_DIM:]
            else:
                acc_ref[c, rows, :] += pv[:, :DA_V_DIM]
                l_ref[c, rows, :] += pv[:, DA_V_DIM:]

    diag0 = pl.multiple_of(qi * tq, tq)
    for r0 in range(0, tq, DIAG_BAND):
        attend(diag0, row0=r0, nrows=DIAG_BAND, width=r0 + DIAG_BAND, masked=True, first=True)

    def body(j, carry):
        attend(pl.multiple_of(j * tk, tk))
        return carry

    lax.fori_loop(0, qi * (tq // tk), body, 0)

    lam = (jnp.exp(jnp.sum(lq1_ref[...] * lk1_ref[...], axis=-1, keepdims=True))
           - jnp.exp(jnp.sum(lq2_ref[...] * lk2_ref[...], axis=-1, keepdims=True))
           + LAMBDA_INIT)
    for hd in range(HEADS_DA):
        hs = slice(hd * LANES, (hd + 1) * LANES)
        o = acc_ref[2 * hd] / l_ref[2 * hd] - lam * (acc_ref[2 * hd + 1] / l_ref[2 * hd + 1])
        ms = jnp.mean(o * o, axis=-1, keepdims=True)
        o = o * lax.rsqrt(ms + EPS) * nw_ref[...] * (1.0 - LAMBDA_INIT)
        o_ref[:, hs] = (o * z_ref[:, hs].astype(F32)).astype(BF16)


def _diff_attention(qb, kb, vb, zb, lambda_q1, lambda_k1, lambda_q2, lambda_k2, da_norm_w,
                    batch, seq_len):
    tq = TQ_DA
    assert tq % TK_DA == 0 and tq % DIAG_BAND == 0 and seq_len % tq == 0 and DA_HEADS % HEADS_DA == 0
    nq = seq_len // tq
    width = HEADS_DA * LANES
    lam_vec = lambda v: v.astype(F32).reshape(1, DA_QK_DIM)
    q_blk = pl.BlockSpec((tq, width), lambda b, h, i: (b * nq + i, h))
    kv_blk = pl.BlockSpec((seq_len, width), lambda b, h, i: (b, h))
    stat = pltpu.VMEM((2 * HEADS_DA, tq, LANES), F32)
    return pl.pallas_call(
        _diffattn_kernel,
        grid=(batch, DA_HEADS // HEADS_DA, nq),
        in_specs=[_resident((1, DA_QK_DIM))] * 4 + [q_blk, kv_blk, kv_blk, q_blk,
                                                    _resident((1, DA_V_DIM))],
        out_specs=q_blk,
        out_shape=jax.ShapeDtypeStruct((batch * seq_len, DA_WIDTH), BF16),
        scratch_shapes=[stat, stat, pltpu.VMEM((2 * HEADS_DA, tq, DA_V_DIM), F32)],
        compiler_params=pltpu.CompilerParams(
            dimension_semantics=("arbitrary", "arbitrary", "arbitrary"),
            vmem_limit_bytes=VMEM_LIMIT),
        name="diff_attention",
    )(lam_vec(lambda_q1), lam_vec(lambda_k1), lam_vec(lambda_q2), lam_vec(lambda_k2),
      qb, kb, vb, zb, da_norm_w.astype(F32).reshape(1, DA_V_DIM))


def _output_kernel(oa_ref, ob_ref, x_ref, nw_ref, wga_ref, wgb_ref, woa_ref, wob_ref, wo_ref,
                   out_ref):
    x = x_ref[...]
    ms = jnp.mean(x * x, axis=-1, keepdims=True)
    h = (x * lax.rsqrt(ms + EPS) * nw_ref[...]).astype(BF16)
    gate_a = _sigmoid(jnp.dot(h, wga_ref[...], preferred_element_type=F32))
    gate_b = _sigmoid(jnp.dot(h, wgb_ref[...], preferred_element_type=F32))
    y_a = jnp.dot(oa_ref[...], woa_ref[...], preferred_element_type=F32)
    y_b = jnp.dot(ob_ref[...], wob_ref[...], preferred_element_type=F32)
    y = gate_a * y_a + gate_b * y_b
    out_ref[...] = x + jnp.dot(y.astype(BF16), wo_ref[...], preferred_element_type=F32)


def _output_projection(oa, ob, x2, norm_w, w_gates, w_out_a, w_out_b, w_out):
    m = x2.shape[0]
    tm = TM_OUT
    assert m % tm == 0
    row_blk = lambda w: pl.BlockSpec((tm, w), lambda i: (i, 0))
    return pl.pallas_call(
        _output_kernel,
        grid=(m // tm,),
        in_specs=[row_blk(DN_WIDTH), row_blk(DA_WIDTH), row_blk(D_MODEL), _resident((1, D_MODEL)),
                  _resident((D_MODEL, D_MODEL)), _resident((D_MODEL, D_MODEL)),
                  _resident((DN_WIDTH, D_MODEL)), _resident((DA_WIDTH, D_MODEL)),
                  _resident((D_MODEL, D_MODEL))],
        out_specs=row_blk(D_MODEL),
        out_shape=jax.ShapeDtypeStruct((m, D_MODEL), F32),
        compiler_params=pltpu.CompilerParams(
            dimension_semantics=("arbitrary",), vmem_limit_bytes=VMEM_LIMIT),
        name="output_projection",
    )(oa, ob, x2, norm_w.astype(F32).reshape(1, D_MODEL), *w_gates,
      w_out_a.astype(BF16), w_out_b.astype(BF16), w_out.astype(BF16))


def kernel(x, norm_w, w_in, conv_w, a_log, dt_bias, dn_norm_w, q_norm_w, k_norm_w,
           lambda_q1, lambda_k1, lambda_q2, lambda_k2, da_norm_w, w_out_a, w_out_b, w_out):
    batch, seq_len, d_model = x.shape
    assert d_model == D_MODEL
    x2 = x.reshape(batch * seq_len, d_model)
    (qa, ka, va, za, gcol, grow, qb, kb, vb, zb), w_gates = _in_projection(
        x2, norm_w, w_in, conv_w, a_log, dt_bias, q_norm_w, k_norm_w, seq_len)
    oa = _deltanet(qa, ka, va, za, gcol, grow, dn_norm_w, batch, seq_len)
    ob = _diff_attention(qb, kb, vb, zb, lambda_q1, lambda_k1, lambda_q2, lambda_k2, da_norm_w,
                         batch, seq_len)
    out = _output_projection(oa, ob, x2, norm_w, w_gates, w_out_a, w_out_b, w_out)
    return out.reshape(batch, seq_len, d_model)
```

```python
import functools
import math

import jax
import jax.numpy as jnp
from jax import lax
from jax.experimental import pallas as pl
from jax.experimental.pallas import tpu as pltpu

F32 = jnp.float32
BF16 = jnp.bfloat16

D_MODEL = 1024
DN_HEADS = 4
DN_HEAD_DIM = 128
DN_WIDTH = DN_HEADS * DN_HEAD_DIM
CONV_WIDTH = 4
CHUNK = 64
DA_HEADS = 4
DA_QK_DIM = 64
DA_V_DIM = 2 * DA_QK_DIM
DA_WIDTH = DA_HEADS * DA_V_DIM
DA_QK_WIDTH = DA_HEADS * 2 * DA_QK_DIM
LAMBDA_INIT = 0.8 - 0.6 * math.exp(-0.3 * 0)
EPS = 1e-6
NEG_INF = -1e30
LOG2_E = math.log2(math.e)

LANES = 128
MXU_COLS = 256
SUBLANES = 8
BF16_ROWS = 16
AB_ROWS = BF16_ROWS
VMEM_LIMIT = 48 * 1024 * 1024

TM_IN = 512
TB_DN = 512
TQ_DA = 1024
DIAG_BAND = 256
TK_DA = 1024
HEADS_DA = 2
TM_OUT = 512

NT_DIMS = (((1,), (1,)), ((), ()))
TN_DIMS = (((0,), (0,)), ((), ()))


def _sigmoid(v):
    return 0.5 + 0.5 * jnp.tanh(0.5 * v)


def _silu(v):
    h = 0.5 * v
    return h + h * jnp.tanh(h)


def _softplus(v):
    return jnp.maximum(v, 0.0) + jnp.log1p(jnp.exp(-jnp.abs(v)))


def _resident(shape):
    zeros = (0,) * len(shape)
    return pl.BlockSpec(shape, lambda *_: zeros, pipeline_mode=pl.Buffered(1))


def _inproj_kernel(x_ref, nw_ref, wqkv_ref, convw_ref, wza_ref, wab_ref, wabt_ref,
                   alog_l_ref, dtb_l_ref, alog_s_ref, dtb_s_ref, qnw_ref, knw_ref,
                   wqb_ref, wkb_ref, wvb_ref, wzb_ref,
                   qa_ref, ka_ref, va_ref, za_ref, gcol_ref, grow_ref,
                   qb_ref, kb_ref, vb_ref, zb_ref,
                   conv_buf, *, tiles_per_seq):
    tm = x_ref.shape[0]
    half = tm // 2
    i = pl.program_id(0)
    head0 = SUBLANES - (CONV_WIDTH - 1)
    slab = 512

    @pl.when(i % tiles_per_seq == 0)
    def _():
        conv_buf[0:SUBLANES, :] = jnp.zeros((SUBLANES, 3 * DN_WIDTH), F32)

    def normed(r0):
        x = x_ref[r0:r0 + half, :]
        ms = jnp.mean(x * x, axis=-1, keepdims=True)
        return (x * lax.rsqrt(ms + EPS) * nw_ref[...]).astype(BF16)

    def tasks(h, r0):
        rows = slice(r0, r0 + half)

        def conv_qkv(s, out_ref):
            cs = slice(s * DN_WIDTH, (s + 1) * DN_WIDTH)

            res = []

            def matmul():
                res.append(jnp.dot(h, wqkv_ref[:, cs], preferred_element_type=F32))
                conv_buf[SUBLANES + r0:SUBLANES + r0 + half, cs] = res[0]

            def epilogue():
                acc = convw_ref[CONV_WIDTH - 1:CONV_WIDTH, cs] * res[0]
                for j in range(CONV_WIDTH - 1):
                    acc = acc + (convw_ref[j:j + 1, cs]
                                 * conv_buf[head0 + j + r0:head0 + j + r0 + half, cs])
                y = _silu(acc)
                if s == 2:
                    out_ref[rows, :] = y.astype(BF16)
                    return
                scale = DN_HEAD_DIM ** -0.5 if s == 0 else 1.0
                for hd in range(DN_HEADS):
                    hs = slice(hd * DN_HEAD_DIM, (hd + 1) * DN_HEAD_DIM)
                    blk = y[:, hs]
                    ss = jnp.sum(blk * blk, axis=-1, keepdims=True)
                    out_ref[rows, hs] = (blk * (lax.rsqrt(ss + EPS) * scale)).astype(BF16)
            return matmul, [epilogue]

        def plain(w_ref, c0, out_ref, act):
            res = []

            def matmul():
                res.append(jnp.dot(h, w_ref[:, c0:c0 + slab], preferred_element_type=F32))

            def chunk(n):
                def run():
                    cs = slice(n * LANES, (n + 1) * LANES)
                    out_ref[rows, c0 + n * LANES:c0 + (n + 1) * LANES] = (
                        act(res[0][:, cs]).astype(BF16))
                return run
            return matmul, [chunk(n) for n in range(slab // LANES)]

        def qk_norm(w_ref, nrm_ref, out_ref, scale):
            res = []

            def matmul():
                res.append(jnp.dot(h, w_ref[...], preferred_element_type=F32))

            def chunk(hd):
                def run():
                    lo = lax.broadcasted_iota(jnp.int32, (half, LANES), 1) < DA_QK_DIM
                    blk = res[0][:, hd * LANES:(hd + 1) * LANES]
                    sq = blk * blk
                    s_lo = jnp.sum(jnp.where(lo, sq, 0.0), axis=-1, keepdims=True)
                    s_hi = jnp.sum(jnp.where(lo, 0.0, sq), axis=-1, keepdims=True)
                    msq = jnp.where(lo, s_lo, s_hi) * (1.0 / DA_QK_DIM)
                    out_ref[rows, hd * LANES:(hd + 1) * LANES] = (
                        blk * lax.rsqrt(msq + EPS) * nrm_ref[...] * scale).astype(BF16)
                return run
            return matmul, [chunk(hd) for hd in range(DA_HEADS)]

        def decay_beta():
            def matmul():
                ab = jnp.dot(h, wab_ref[...], preferred_element_type=F32)
                lane = lax.broadcasted_iota(jnp.int32, ab.shape, 1)
                g_l = -jnp.exp(alog_l_ref[...]) * _softplus(ab + dtb_l_ref[...])
                gcol_ref[rows, :] = jnp.where(lane < DN_HEADS, g_l, _sigmoid(ab))
                abt = lax.dot_general(wabt_ref[...], h, NT_DIMS,
                                      preferred_element_type=F32)
                row = lax.broadcasted_iota(jnp.int32, abt.shape, 0)
                g_s = -jnp.exp(alog_s_ref[...]) * _softplus(abt + dtb_s_ref[...])
                grow_ref[:, rows] = jnp.where(row < DN_HEADS, g_s, _sigmoid(abt))
            return matmul, []

        ident = lambda v: v
        return [conv_qkv(0, qa_ref), plain(wvb_ref, 0, vb_ref, ident),
                conv_qkv(1, ka_ref), plain(wza_ref, 0, za_ref, _silu),
                conv_qkv(2, va_ref), plain(wzb_ref, 0, zb_ref, _silu),
                qk_norm(wqb_ref, qnw_ref, qb_ref, DA_QK_DIM ** -0.5 * LOG2_E),
                qk_norm(wkb_ref, knw_ref, kb_ref, 1.0), decay_beta()]

    def run_tasks(task_list, extra=None):
        for n, (matmul, chunks) in enumerate(task_list):
            matmul()
            for fn in chunks:
                fn()
            if extra is not None and n == 6:
                extra()

    h_second = []
    run_tasks(tasks(normed(0), 0), extra=lambda: h_second.append(normed(half)))
    run_tasks(tasks(h_second[0], half))
    conv_buf[0:SUBLANES, :] = conv_buf[tm:tm + SUBLANES, :]


def _in_projection(x2, norm_w, w_in, conv_w, a_log, dt_bias, q_norm_w, k_norm_w, seq_len):
    m = x2.shape[0]
    tm = TM_IN
    assert m % tm == 0 and seq_len % tm == 0
    o = 0
    cols = {}
    for name, width in (("qkv", 3 * DN_WIDTH), ("za", DN_WIDTH), ("a", DN_HEADS), ("b", DN_HEADS),
                        ("qb", DA_QK_WIDTH), ("kb", DA_QK_WIDTH), ("vb", DA_WIDTH), ("zb", DA_WIDTH),
                        ("ga", D_MODEL), ("gb", D_MODEL)):
        cols[name] = w_in[:, o:o + width]
        o += width
    assert o == w_in.shape[1]
    wb = {k: v.astype(BF16) for k, v in cols.items()}
    w_ab = jnp.concatenate([cols["a"], cols["b"]], axis=1)
    w_ab_l = jnp.pad(w_ab, ((0, 0), (0, LANES - 2 * DN_HEADS))).astype(BF16)
    w_ab_s = jnp.pad(w_ab.T, ((0, AB_ROWS - 2 * DN_HEADS), (0, 0))).astype(BF16)
    alog_l = jnp.pad(a_log.astype(F32), (0, LANES - DN_HEADS)).reshape(1, LANES)
    dtb_l = jnp.pad(dt_bias.astype(F32), (0, LANES - DN_HEADS)).reshape(1, LANES)
    alog_s = jnp.pad(a_log.astype(F32), (0, AB_ROWS - DN_HEADS)).reshape(AB_ROWS, 1)
    dtb_s = jnp.pad(dt_bias.astype(F32), (0, AB_ROWS - DN_HEADS)).reshape(AB_ROWS, 1)
    qnw = jnp.tile(q_norm_w.astype(F32), 2).reshape(1, LANES)
    knw = jnp.tile(k_norm_w.astype(F32), 2).reshape(1, LANES)

    row_blk = lambda w: pl.BlockSpec((tm, w), lambda i: (i, 0))
    in_specs = [
        row_blk(D_MODEL), _resident((1, D_MODEL)),
        _resident((D_MODEL, 3 * DN_WIDTH)), _resident((CONV_WIDTH, 3 * DN_WIDTH)),
        _resident((D_MODEL, DN_WIDTH)), _resident((D_MODEL, LANES)), _resident((AB_ROWS, D_MODEL)),
        _resident((1, LANES)), _resident((1, LANES)), _resident((AB_ROWS, 1)), _resident((AB_ROWS, 1)),
        _resident((1, LANES)), _resident((1, LANES)),
        _resident((D_MODEL, DA_QK_WIDTH)), _resident((D_MODEL, DA_QK_WIDTH)),
        _resident((D_MODEL, DA_WIDTH)), _resident((D_MODEL, DA_WIDTH)),
    ]
    out_shapes = [jax.ShapeDtypeStruct((m, DN_WIDTH), BF16)] * 4 + [
        jax.ShapeDtypeStruct((m, LANES), F32), jax.ShapeDtypeStruct((AB_ROWS, m), F32)] + [
        jax.ShapeDtypeStruct((m, DA_WIDTH), BF16)] * 4
    out_specs = [row_blk(DN_WIDTH)] * 4 + [
        row_blk(LANES), pl.BlockSpec((AB_ROWS, tm), lambda i: (0, i))] + [
        row_blk(DA_WIDTH)] * 4
    outs = pl.pallas_call(
        functools.partial(_inproj_kernel, tiles_per_seq=seq_len // tm),
        grid=(m // tm,),
        in_specs=in_specs,
        out_specs=out_specs,
        out_shape=out_shapes,
        scratch_shapes=[pltpu.VMEM((tm + SUBLANES, 3 * DN_WIDTH), F32)],
        compiler_params=pltpu.CompilerParams(
            dimension_semantics=("arbitrary",), vmem_limit_bytes=VMEM_LIMIT),
        name="in_projection",
    )(x2, norm_w.astype(F32).reshape(1, D_MODEL), wb["qkv"], conv_w.astype(F32), wb["za"],
      w_ab_l, w_ab_s, alog_l, dtb_l, alog_s, dtb_s, qnw, knw,
      wb["qb"], wb["kb"], wb["vb"], wb["zb"])
    return outs, (wb["ga"], wb["gb"])


def _bmm(a, b):
    return jnp.einsum('nij,njk->nik', a, b, preferred_element_type=F32)


def _bmm_nt(a, b):
    return jnp.einsum('nid,njd->nij', a, b, preferred_element_type=F32)


def _split_bf16(x, parts):
    out = []
    for _ in range(parts - 1):
        hi = x.astype(BF16)
        out.append(hi)
        x = x - hi.astype(F32)
    out.append(x.astype(BF16))
    return out


def _deltanet_kernel(q_ref, k_ref, v_ref, z_ref, gcol_ref, grow_ref, nw_ref, o_ref,
                     state_ref, oraw_ref, lhs_ref, attn_ref, u_ref, kdec_ref, sdec_ref,
                     *, tiles, tiles_per_seq):
    tb = q_ref.shape[0]
    c = CHUNK
    nc = tb // c
    nh = DN_HEADS
    t = pl.program_id(0)
    keep_state = (t - 1) % tiles_per_seq != 0

    ri = lax.broadcasted_iota(jnp.int32, (c, c), 0)
    ci = lax.broadcasted_iota(jnp.int32, (c, c), 1)
    lower = ri >= ci
    strict = ri > ci
    eye = (ri == ci).astype(F32)
    tri = lower.astype(BF16)
    tri_t = (ri <= ci).astype(BF16)

    def heads_major(ref):
        return jnp.concatenate(
            [ref[:, hd * DN_HEAD_DIM:(hd + 1) * DN_HEAD_DIM].reshape(nc, c, DN_HEAD_DIM)
             for hd in range(nh)], axis=0)

    def phase1_stages():
        e = {}

        def decays():
            gb_col = gcol_ref[...].reshape(nc, c, LANES)
            tri_b = jnp.broadcast_to(tri, (nc, c, c))
            gc_col = sum(_bmm(tri_b, part) for part in _split_bf16(gb_col, 3))
            gb_row = grow_ref[...]
            gc_row = sum(jnp.dot(part, tri_t, preferred_element_type=F32)
                         for part in _split_bf16(gb_row.reshape(nc * AB_ROWS, c), 3)
                         ).reshape(nc, AB_ROWS, c)

            def col_form(src, lane):
                return jnp.concatenate(
                    [jnp.broadcast_to(src[:, :, lane + hd:lane + hd + 1], (nc, c, LANES))
                     for hd in range(nh)], axis=0)

            def row_form(src, row):
                return jnp.concatenate(
                    [src[:, row + hd:row + hd + 1, :] for hd in range(nh)], axis=0)

            e["g_c"] = col_form(gc_col, 0)
            e["beta_c"] = col_form(gb_col, nh)
            e["g_r"] = row_form(gc_row, 0)
            e["beta_r"] = row_form(gb_row, nh)
            e["g_last"] = e["g_c"][:, c - 1:c, :]
            diff = e["g_c"][:, :, :c] - e["g_r"]
            e["decay"] = jnp.where(lower, jnp.exp(jnp.where(lower, diff, 0.0)), 0.0)

        def gram_kk():
            e["q"], e["k"], e["v"] = heads_major(q_ref), heads_major(k_ref), heads_major(v_ref)
            l_strict = jnp.where(
                strict, _bmm_nt(e["k"], e["k"]) * e["decay"] * e["beta_c"][:, :, :c], 0.0)
            e["inv"] = eye - l_strict
            e["power"] = l_strict.astype(BF16)

        def gram_qk():
            e["attn"] = jnp.where(lower, _bmm_nt(e["q"], e["k"]) * e["decay"], 0.0).astype(BF16)

        def next_power():
            e["power"] = _bmm(e["power"], e["power"]).astype(BF16)

        def apply_power():
            e["inv"] = e["inv"] + _bmm(e["inv"].astype(BF16), e["power"])

        def solve_u():
            e["t_beta"] = e["inv"] * e["beta_r"]
            e["u"] = _bmm(e["t_beta"].astype(BF16), e["v"])

        def solve_w():
            e["w"] = _bmm((e["t_beta"] * jnp.exp(e["g_r"])).astype(BF16), e["k"]).astype(BF16)

        def commit():
            q_decay = (e["q"].astype(F32) * jnp.exp(e["g_c"])).astype(BF16)
            lhs_ref[...] = jnp.concatenate([e["w"], q_decay], axis=1)
            attn_ref[...] = e["attn"]
            u_ref[...] = e["u"]
            kdec_ref[...] = (e["k"].astype(F32) * jnp.exp(e["g_last"] - e["g_c"])).astype(BF16)
            sdec_ref[...] = jnp.exp(e["g_last"])

        levels = [next_power, apply_power] * (c.bit_length() - 2)
        return [decays, gram_kk, gram_qk] + levels + [solve_u, solve_w], commit

    def phase2_stages():
        def chunk_steps(n):
            idx = [hd * nc + n for hd in range(nh)]
            e = {}

            def through_state():
                e["s_old"] = [jnp.where(keep_state, state_ref[hd], 0.0) if n == 0
                              else state_ref[hd] for hd in range(nh)]
                e["ws_qs"] = [jnp.dot(lhs_ref[b], e["s_old"][hd].astype(BF16),
                                      preferred_element_type=F32) for hd, b in enumerate(idx)]

            def update():
                v_new = [(u_ref[b] - e["ws_qs"][hd][:c]).astype(BF16) for hd, b in enumerate(idx)]
                for hd, b in enumerate(idx):
                    oraw_ref[n * c:(n + 1) * c, hd * DN_HEAD_DIM:(hd + 1) * DN_HEAD_DIM] = (
                        e["ws_qs"][hd][c:]
                        + jnp.dot(attn_ref[b], v_new[hd], preferred_element_type=F32))
                for hd, b in enumerate(idx):
                    state_ref[hd] = e["s_old"][hd] * sdec_ref[b] + lax.dot_general(
                        kdec_ref[b], v_new[hd], TN_DIMS, preferred_element_type=F32)
            return [through_state, update]

        def gated_norm():
            for hd in range(nh):
                hs = slice(hd * DN_HEAD_DIM, (hd + 1) * DN_HEAD_DIM)
                o = oraw_ref[:, hs]
                ms = jnp.mean(o * o, axis=-1, keepdims=True)
                o = o * lax.rsqrt(ms + EPS) * nw_ref[...]
                o_ref[:, hs] = (o * z_ref[:, hs].astype(F32)).astype(BF16)

        return [fn for n in range(nc) for fn in chunk_steps(n)] + [gated_norm]

    @pl.when(t == 0)
    def _():
        state_ref[...] = jnp.zeros(state_ref.shape, F32)
        stages, commit = phase1_stages()
        for fn in stages:
            fn()
        commit()

    @pl.when((t > 0) & (t < tiles))
    def _():
        stages, commit = phase1_stages()
        others = phase2_stages()
        done = 0
        for n, fn in enumerate(stages):
            fn()
            upto = -(-(n + 1) * len(others) // len(stages))
            for other in others[done:upto]:
                other()
            done = upto
        commit()

    @pl.when(t == tiles)
    def _():
        for fn in phase2_stages():
            fn()


def _deltanet(qa, ka, va, za, gcol, grow, dn_norm_w, batch, seq_len):
    tb = TB_DN
    assert seq_len % tb == 0 and tb % CHUNK == 0
    nt = seq_len // tb
    tiles = batch * nt
    nc = tb // CHUNK
    nb = DN_HEADS * nc
    grow3 = grow.reshape(AB_ROWS, -1, CHUNK).transpose(1, 0, 2)
    cur = lambda w: pl.BlockSpec((tb, w), lambda t: (jnp.minimum(t, tiles - 1), 0))
    prev = lambda w: pl.BlockSpec((tb, w), lambda t: (jnp.maximum(t - 1, 0), 0))
    return pl.pallas_call(
        functools.partial(_deltanet_kernel, tiles=tiles, tiles_per_seq=nt),
        grid=(tiles + 1,),
        in_specs=[cur(DN_WIDTH)] * 3 + [
            prev(DN_WIDTH), cur(LANES),
            pl.BlockSpec((nc, AB_ROWS, CHUNK), lambda t: (jnp.minimum(t, tiles - 1), 0, 0)),
            _resident((1, DN_HEAD_DIM))],
        out_specs=prev(DN_WIDTH),
        out_shape=jax.ShapeDtypeStruct((batch * seq_len, DN_WIDTH), BF16),
        scratch_shapes=[pltpu.VMEM((DN_HEADS, DN_HEAD_DIM, DN_HEAD_DIM), F32),
                        pltpu.VMEM((tb, DN_WIDTH), F32),
                        pltpu.VMEM((nb, 2 * CHUNK, DN_HEAD_DIM), BF16),
                        pltpu.VMEM((nb, CHUNK, CHUNK), BF16),
                        pltpu.VMEM((nb, CHUNK, DN_HEAD_DIM), F32),
                        pltpu.VMEM((nb, CHUNK, DN_HEAD_DIM), BF16),
                        pltpu.VMEM((nb, 1, DN_HEAD_DIM), F32)],
        compiler_params=pltpu.CompilerParams(
            dimension_semantics=("arbitrary",), vmem_limit_bytes=VMEM_LIMIT),
        name="gated_deltanet",
    )(qa, ka, va, za, gcol, grow3, dn_norm_w.astype(F32).reshape(1, DN_HEAD_DIM))


def _diffattn_kernel(lq1_ref, lk1_ref, lq2_ref, lk2_ref, q_ref, k_ref, v_ref, z_ref, nw_ref,
                     o_ref, m_ref, l_ref, acc_ref):
    tq = q_ref.shape[0]
    tk = TK_DA
    qi = pl.program_id(2)
    chains = [(hd, comp) for hd in range(HEADS_DA) for comp in range(2)]

    lane = lax.broadcasted_iota(jnp.int32, (1, LANES), 1)
    first_half = lane < DA_QK_DIM
    q_chain = []
    for hd in range(HEADS_DA):
        q = q_ref[:, hd * LANES:(hd + 1) * LANES]
        zero = jnp.zeros_like(q)
        q_chain += [jnp.where(first_half, q, zero), jnp.where(first_half, zero, q)]

    def attend(start, row0=0, nrows=None, width=None, masked=False, first=False):
        nrows = tq if nrows is None else nrows
        width = tk if width is None else width
        rows = slice(row0, row0 + nrows)
        kblk = [k_ref[pl.ds(start, width), hd * LANES:(hd + 1) * LANES] for hd in range(HEADS_DA)]
        ones = jnp.ones((width, LANES), BF16)
        vblk = [jnp.concatenate([v_ref[pl.ds(start, width), hd * LANES:(hd + 1) * LANES], ones],
                                axis=1) for hd in range(HEADS_DA)]
        scores = [lax.dot_general(q_chain[c][rows], kblk[hd], NT_DIMS, preferred_element_type=F32)
                  for c, (hd, _) in enumerate(chains)]
        if masked:
            rq = lax.broadcasted_iota(jnp.int32, (nrows, width), 0) + row0
            ck = lax.broadcasted_iota(jnp.int32, (nrows, width), 1)
            scores = [jnp.where(ck <= rq, s, NEG_INF) for s in scores]
        probs = []
        for c, s in enumerate(scores):
            m_new = jnp.max(s, axis=-1, keepdims=True)
            if first:
                m_new = jnp.broadcast_to(m_new, (nrows, LANES))
            else:
                m_prev = m_ref[c, rows, :]
                m_new = jnp.maximum(m_prev, m_new)
                alpha = jnp.exp2(m_prev - m_new)
                l_ref[c, rows, :] = alpha * l_ref[c, rows, :]
                acc_ref[c, rows, :] = alpha * acc_ref[c, rows, :]
            probs.append(jnp.exp2(
                (s - jnp.concatenate([m_new] * (width // LANES), axis=1)).astype(BF16)))
            m_ref[c, rows, :] = m_new
        for c, (hd, _) in enumerate(chains):
            pv = jnp.dot(probs[c], vblk[hd], preferred_element_type=F32)
            if first:
                acc_ref[c, rows, :] = pv[:, :DA_V_DIM]
                l_ref[c, rows, :] = pv[:, DA_V_DIM:]
            else:
                acc_ref[c, rows, :] += pv[:, :DA_V_DIM]
                l_ref[c, rows, :] += pv[:, DA_V_DIM:]

    diag0 = pl.multiple_of(qi * tq, tq)
    for r0 in range(0, tq, DIAG_BAND):
        attend(diag0, row0=r0, nrows=DIAG_BAND, width=r0 + DIAG_BAND, masked=True, first=True)

    def body(j, carry):
        attend(pl.multiple_of(j * tk, tk))
        return carry

    lax.fori_loop(0, qi * (tq // tk), body, 0)

    lam = (jnp.exp(jnp.sum(lq1_ref[...] * lk1_ref[...], axis=-1, keepdims=True))
           - jnp.exp(jnp.sum(lq2_ref[...] * lk2_ref[...], axis=-1, keepdims=True))
           + LAMBDA_INIT)
    for hd in range(HEADS_DA):
        hs = slice(hd * LANES, (hd + 1) * LANES)
        o = acc_ref[2 * hd] / l_ref[2 * hd] - lam * (acc_ref[2 * hd + 1] / l_ref[2 * hd + 1])
        ms = jnp.mean(o * o, axis=-1, keepdims=True)
        o = o * lax.rsqrt(ms + EPS) * nw_ref[...] * (1.0 - LAMBDA_INIT)
        o_ref[:, hs] = (o * z_ref[:, hs].astype(F32)).astype(BF16)


def _diff_attention(qb, kb, vb, zb, lambda_q1, lambda_k1, lambda_q2, lambda_k2, da_norm_w,
                    batch, seq_len):
    tq = TQ_DA
    assert tq % TK_DA == 0 and tq % DIAG_BAND == 0 and seq_len % tq == 0 and DA_HEADS % HEADS_DA == 0
    nq = seq_len // tq
    width = HEADS_DA * LANES
    lam_vec = lambda v: v.astype(F32).reshape(1, DA_QK_DIM)
    q_blk = pl.BlockSpec((tq, width), lambda b, h, i: (b * nq + i, h))
    kv_blk = pl.BlockSpec((seq_len, width), lambda b, h, i: (b, h))
    stat = pltpu.VMEM((2 * HEADS_DA, tq, LANES), F32)
    return pl.pallas_call(
        _diffattn_kernel,
        grid=(batch, DA_HEADS // HEADS_DA, nq),
        in_specs=[_resident((1, DA_QK_DIM))] * 4 + [q_blk, kv_blk, kv_blk, q_blk,
                                                    _resident((1, DA_V_DIM))],
        out_specs=q_blk,
        out_shape=jax.ShapeDtypeStruct((batch * seq_len, DA_WIDTH), BF16),
        scratch_shapes=[stat, stat, pltpu.VMEM((2 * HEADS_DA, tq, DA_V_DIM), F32)],
        compiler_params=pltpu.CompilerParams(
            dimension_semantics=("arbitrary", "arbitrary", "arbitrary"),
            vmem_limit_bytes=VMEM_LIMIT),
        name="diff_attention",
    )(lam_vec(lambda_q1), lam_vec(lambda_k1), lam_vec(lambda_q2), lam_vec(lambda_k2),
      qb, kb, vb, zb, da_norm_w.astype(F32).reshape(1, DA_V_DIM))


def _output_kernel(oa_ref, ob_ref, x_ref, nw_ref, wga_ref, wgb_ref, woa_ref, wob_ref, wo_ref,
                   out_ref):
    x = x_ref[...]
    ms = jnp.mean(x * x, axis=-1, keepdims=True)
    h = (x * lax.rsqrt(ms + EPS) * nw_ref[...]).astype(BF16)
    gate_a = _sigmoid(jnp.dot(h, wga_ref[...], preferred_element_type=F32))
    gate_b = _sigmoid(jnp.dot(h, wgb_ref[...], preferred_element_type=F32))
    y_a = jnp.dot(oa_ref[...], woa_ref[...], preferred_element_type=F32)
    y_b = jnp.dot(ob_ref[...], wob_ref[...], preferred_element_type=F32)
    y = gate_a * y_a + gate_b * y_b
    out_ref[...] = x + jnp.dot(y.astype(BF16), wo_ref[...], preferred_element_type=F32)


def _output_projection(oa, ob, x2, norm_w, w_gates, w_out_a, w_out_b, w_out):
    m = x2.shape[0]
    tm = TM_OUT
    assert m % tm == 0
    row_blk = lambda w: pl.BlockSpec((tm, w), lambda i: (i, 0))
    return pl.pallas_call(
        _output_kernel,
        grid=(m // tm,),
        in_specs=[row_blk(DN_WIDTH), row_blk(DA_WIDTH), row_blk(D_MODEL), _resident((1, D_MODEL)),
                  _resident((D_MODEL, D_MODEL)), _resident((D_MODEL, D_MODEL)),
                  _resident((DN_WIDTH, D_MODEL)), _resident((DA_WIDTH, D_MODEL)),
                  _resident((D_MODEL, D_MODEL))],
        out_specs=row_blk(D_MODEL),
        out_shape=jax.ShapeDtypeStruct((m, D_MODEL), F32),
        compiler_params=pltpu.CompilerParams(
            dimension_semantics=("arbitrary",), vmem_limit_bytes=VMEM_LIMIT),
        name="output_projection",
    )(oa, ob, x2, norm_w.astype(F32).reshape(1, D_MODEL), *w_gates,
      w_out_a.astype(BF16), w_out_b.astype(BF16), w_out.astype(BF16))


def kernel(x, norm_w, w_in, conv_w, a_log, dt_bias, dn_norm_w, q_norm_w, k_norm_w,
           lambda_q1, lambda_k1, lambda_q2, lambda_k2, da_norm_w, w_out_a, w_out_b, w_out):
    batch, seq_len, d_model = x.shape
    assert d_model == D_MODEL
    x2 = x.reshape(batch * seq_len, d_model)
    (qa, ka, va, za, gcol, grow, qb, kb, vb, zb), w_gates = _in_projection(
        x2, norm_w, w_in, conv_w, a_log, dt_bias, q_norm_w, k_norm_w, seq_len)
    oa = _deltanet(qa, ka, va, za, gcol, grow, dn_norm_w, batch, seq_len)
    ob = _diff_attention(qb, kb, vb, zb, lambda_q1, lambda_k1, lambda_q2, lambda_k2, da_norm_w,
                         batch, seq_len)
    out = _output_projection(oa, ob, x2, norm_w, w_gates, w_out_a, w_out_b, w_out)
    return out.reshape(batch, seq_len, d_model)
```

```python
import functools
import math

import jax
import jax.numpy as jnp
from jax import lax
from jax.experimental import pallas as pl
from jax.experimental.pallas import tpu as pltpu

F32 = jnp.float32
BF16 = jnp.bfloat16

D_MODEL = 1024
DN_HEADS = 4
DN_HEAD_DIM = 128
DN_WIDTH = DN_HEADS * DN_HEAD_DIM
CONV_WIDTH = 4
CHUNK = 64
DA_HEADS = 4
DA_QK_DIM = 64
DA_V_DIM = 2 * DA_QK_DIM
DA_WIDTH = DA_HEADS * DA_V_DIM
DA_QK_WIDTH = DA_HEADS * 2 * DA_QK_DIM
LAMBDA_INIT = 0.8 - 0.6 * math.exp(-0.3 * 0)
EPS = 1e-6
NEG_INF = -1e30
LOG2_E = math.log2(math.e)

LANES = 128
SUBLANES = 8
BF16_ROWS = 16
AB_ROWS = BF16_ROWS
VMEM_LIMIT = 48 * 1024 * 1024

TM_IN = 512
TB_DN = 512
TQ_DA = 1024
DIAG_BAND = 256
TK_DA = 1024
HEADS_DA = 2
TM_OUT = 512

NT_DIMS = (((1,), (1,)), ((), ()))
TN_DIMS = (((0,), (0,)), ((), ()))


def _sigmoid(v):
    return 0.5 + 0.5 * jnp.tanh(0.5 * v)


def _silu(v):
    h = 0.5 * v
    return h + h * jnp.tanh(h)


def _softplus(v):
    return jnp.maximum(v, 0.0) + jnp.log1p(jnp.exp(-jnp.abs(v)))


def _resident(shape):
    zeros = (0,) * len(shape)
    return pl.BlockSpec(shape, lambda *_: zeros, pipeline_mode=pl.Buffered(1))


def _inproj_kernel(x_ref, nw_ref, wqkv_ref, convw_ref, wza_ref, wab_ref, wabt_ref,
                   alog_l_ref, dtb_l_ref, alog_s_ref, dtb_s_ref, qnw_ref, knw_ref,
                   wqb_ref, wkb_ref, wvb_ref, wzb_ref,
                   qa_ref, ka_ref, va_ref, za_ref, gcol_ref, grow_ref,
                   qb_ref, kb_ref, vb_ref, zb_ref,
                   conv_buf, *, tiles_per_seq):
    tm = x_ref.shape[0]
    half = tm // 2
    i = pl.program_id(0)
    head0 = SUBLANES - (CONV_WIDTH - 1)
    slab = 512

    @pl.when(i % tiles_per_seq == 0)
    def _():
        conv_buf[0:SUBLANES, :] = jnp.zeros((SUBLANES, 3 * DN_WIDTH), F32)

    def normed(r0):
        x = x_ref[r0:r0 + half, :]
        ms = jnp.mean(x * x, axis=-1, keepdims=True)
        return (x * lax.rsqrt(ms + EPS) * nw_ref[...]).astype(BF16)

    def tasks(h, r0):
        rows = slice(r0, r0 + half)

        def conv_qkv(s, out_ref):
            cs = slice(s * DN_WIDTH, (s + 1) * DN_WIDTH)

            res = []

            def matmul():
                res.append(jnp.dot(h, wqkv_ref[:, cs], preferred_element_type=F32))
                conv_buf[SUBLANES + r0:SUBLANES + r0 + half, cs] = res[0]

            def epilogue():
                acc = convw_ref[CONV_WIDTH - 1:CONV_WIDTH, cs] * res[0]
                for j in range(CONV_WIDTH - 1):
                    acc = acc + (convw_ref[j:j + 1, cs]
                                 * conv_buf[head0 + j + r0:head0 + j + r0 + half, cs])
                y = _silu(acc)
                if s == 2:
                    out_ref[rows, :] = y.astype(BF16)
                    return
                scale = DN_HEAD_DIM ** -0.5 if s == 0 else 1.0
                for hd in range(DN_HEADS):
                    hs = slice(hd * DN_HEAD_DIM, (hd + 1) * DN_HEAD_DIM)
                    blk = y[:, hs]
                    ss = jnp.sum(blk * blk, axis=-1, keepdims=True)
                    out_ref[rows, hs] = (blk * (lax.rsqrt(ss + EPS) * scale)).astype(BF16)
            return matmul, [epilogue]

        def plain(w_ref, c0, out_ref, act):
            res = []

            def matmul():
                res.append(jnp.dot(h, w_ref[:, c0:c0 + slab], preferred_element_type=F32))

            def chunk(n):
                def run():
                    cs = slice(n * LANES, (n + 1) * LANES)
                    out_ref[rows, c0 + n * LANES:c0 + (n + 1) * LANES] = (
                        act(res[0][:, cs]).astype(BF16))
                return run
            return matmul, [chunk(n) for n in range(slab // LANES)]

        def qk_norm(w_ref, nrm_ref, out_ref, scale):
            res = []

            def matmul():
                res.append(jnp.dot(h, w_ref[...], preferred_element_type=F32))

            def chunk(hd):
                def run():
                    lo = lax.broadcasted_iota(jnp.int32, (half, LANES), 1) < DA_QK_DIM
                    blk = res[0][:, hd * LANES:(hd + 1) * LANES]
                    sq = blk * blk
                    s_lo = jnp.sum(jnp.where(lo, sq, 0.0), axis=-1, keepdims=True)
                    s_hi = jnp.sum(jnp.where(lo, 0.0, sq), axis=-1, keepdims=True)
                    msq = jnp.where(lo, s_lo, s_hi) * (1.0 / DA_QK_DIM)
                    out_ref[rows, hd * LANES:(hd + 1) * LANES] = (
                        blk * lax.rsqrt(msq + EPS) * nrm_ref[...] * scale).astype(BF16)
                return run
            return matmul, [chunk(hd) for hd in range(DA_HEADS)]

        def decay_beta():
            def matmul():
                ab = jnp.dot(h, wab_ref[...], preferred_element_type=F32)
                lane = lax.broadcasted_iota(jnp.int32, ab.shape, 1)
                g_l = -jnp.exp(alog_l_ref[...]) * _softplus(ab + dtb_l_ref[...])
                gcol_ref[rows, :] = jnp.where(lane < DN_HEADS, g_l, _sigmoid(ab))
                abt = lax.dot_general(wabt_ref[...], h, NT_DIMS,
                                      preferred_element_type=F32)
                row = lax.broadcasted_iota(jnp.int32, abt.shape, 0)
                g_s = -jnp.exp(alog_s_ref[...]) * _softplus(abt + dtb_s_ref[...])
                grow_ref[:, rows] = jnp.where(row < DN_HEADS, g_s, _sigmoid(abt))
            return matmul, []

        ident = lambda v: v
        return [conv_qkv(0, qa_ref), plain(wvb_ref, 0, vb_ref, ident),
                conv_qkv(1, ka_ref), plain(wza_ref, 0, za_ref, _silu),
                conv_qkv(2, va_ref), plain(wzb_ref, 0, zb_ref, _silu),
                qk_norm(wqb_ref, qnw_ref, qb_ref, DA_QK_DIM ** -0.5 * LOG2_E),
                qk_norm(wkb_ref, knw_ref, kb_ref, 1.0), decay_beta()]

    def run_tasks(task_list, extra=None):
        for n, (matmul, chunks) in enumerate(task_list):
            matmul()
            for fn in chunks:
                fn()
            if extra is not None and n == 6:
                extra()

    h_second = []
    run_tasks(tasks(normed(0), 0), extra=lambda: h_second.append(normed(half)))
    run_tasks(tasks(h_second[0], half))
    conv_buf[0:SUBLANES, :] = conv_buf[tm:tm + SUBLANES, :]


def _in_projection(x2, norm_w, w_in, conv_w, a_log, dt_bias, q_norm_w, k_norm_w, seq_len):
    m = x2.shape[0]
    tm = TM_IN
    assert m % tm == 0 and seq_len % tm == 0
    o = 0
    cols = {}
    for name, width in (("qkv", 3 * DN_WIDTH), ("za", DN_WIDTH), ("a", DN_HEADS), ("b", DN_HEADS),
                        ("qb", DA_QK_WIDTH), ("kb", DA_QK_WIDTH), ("vb", DA_WIDTH), ("zb", DA_WIDTH),
                        ("ga", D_MODEL), ("gb", D_MODEL)):
        cols[name] = w_in[:, o:o + width]
        o += width
    assert o == w_in.shape[1]
    wb = {k: v.astype(BF16) for k, v in cols.items()}
    w_ab = jnp.concatenate([cols["a"], cols["b"]], axis=1)
    w_ab_l = jnp.pad(w_ab, ((0, 0), (0, LANES - 2 * DN_HEADS))).astype(BF16)
    w_ab_s = jnp.pad(w_ab.T, ((0, AB_ROWS - 2 * DN_HEADS), (0, 0))).astype(BF16)
    alog_l = jnp.pad(a_log.astype(F32), (0, LANES - DN_HEADS)).reshape(1, LANES)
    dtb_l = jnp.pad(dt_bias.astype(F32), (0, LANES - DN_HEADS)).reshape(1, LANES)
    alog_s = jnp.pad(a_log.astype(F32), (0, AB_ROWS - DN_HEADS)).reshape(AB_ROWS, 1)
    dtb_s = jnp.pad(dt_bias.astype(F32), (0, AB_ROWS - DN_HEADS)).reshape(AB_ROWS, 1)
    qnw = jnp.tile(q_norm_w.astype(F32), 2).reshape(1, LANES)
    knw = jnp.tile(k_norm_w.astype(F32), 2).reshape(1, LANES)

    row_blk = lambda w: pl.BlockSpec((tm, w), lambda i: (i, 0))
    in_specs = [
        row_blk(D_MODEL), _resident((1, D_MODEL)),
        _resident((D_MODEL, 3 * DN_WIDTH)), _resident((CONV_WIDTH, 3 * DN_WIDTH)),
        _resident((D_MODEL, DN_WIDTH)), _resident((D_MODEL, LANES)), _resident((AB_ROWS, D_MODEL)),
        _resident((1, LANES)), _resident((1, LANES)), _resident((AB_ROWS, 1)), _resident((AB_ROWS, 1)),
        _resident((1, LANES)), _resident((1, LANES)),
        _resident((D_MODEL, DA_QK_WIDTH)), _resident((D_MODEL, DA_QK_WIDTH)),
        _resident((D_MODEL, DA_WIDTH)), _resident((D_MODEL, DA_WIDTH)),
    ]
    out_shapes = [jax.ShapeDtypeStruct((m, DN_WIDTH), BF16)] * 4 + [
        jax.ShapeDtypeStruct((m, LANES), F32), jax.ShapeDtypeStruct((AB_ROWS, m), F32)] + [
        jax.ShapeDtypeStruct((m, DA_WIDTH), BF16)] * 4
    out_specs = [row_blk(DN_WIDTH)] * 4 + [
        row_blk(LANES), pl.BlockSpec((AB_ROWS, tm), lambda i: (0, i))] + [
        row_blk(DA_WIDTH)] * 4
    outs = pl.pallas_call(
        functools.partial(_inproj_kernel, tiles_per_seq=seq_len // tm),
        grid=(m // tm,),
        in_specs=in_specs,
        out_specs=out_specs,
        out_shape=out_shapes,
        scratch_shapes=[pltpu.VMEM((tm + SUBLANES, 3 * DN_WIDTH), F32)],
        compiler_params=pltpu.CompilerParams(
            dimension_semantics=("arbitrary",), vmem_limit_bytes=VMEM_LIMIT),
        name="in_projection",
    )(x2, norm_w.astype(F32).reshape(1, D_MODEL), wb["qkv"], conv_w.astype(F32), wb["za"],
      w_ab_l, w_ab_s, alog_l, dtb_l, alog_s, dtb_s, qnw, knw,
      wb["qb"], wb["kb"], wb["vb"], wb["zb"])
    return outs, (wb["ga"], wb["gb"])


def _bmm(a, b):
    return jnp.einsum('nij,njk->nik', a, b, preferred_element_type=F32)


def _bmm_nt(a, b):
    return jnp.einsum('nid,njd->nij', a, b, preferred_element_type=F32)


def _split_bf16(x, parts):
    out = []
    for _ in range(parts - 1):
        hi = x.astype(BF16)
        out.append(hi)
        x = x - hi.astype(F32)
    out.append(x.astype(BF16))
    return out


def _deltanet_kernel(q_ref, k_ref, v_ref, z_ref, gcol_ref, grow_ref, nw_ref, o_ref,
                     state_ref, oraw_ref, lhs_ref, attn_ref, u_ref, kdec_ref, sdec_ref,
                     *, tiles, tiles_per_seq):
    tb = q_ref.shape[0]
    c = CHUNK
    nc = tb // c
    nh = DN_HEADS
    t = pl.program_id(0)
    keep_state = (t - 1) % tiles_per_seq != 0

    ri = lax.broadcasted_iota(jnp.int32, (c, c), 0)
    ci = lax.broadcasted_iota(jnp.int32, (c, c), 1)
    lower = ri >= ci
    strict = ri > ci
    eye = (ri == ci).astype(F32)
    tri = lower.astype(BF16)
    tri_t = (ri <= ci).astype(BF16)

    def heads_major(ref):
        return jnp.concatenate(
            [ref[:, hd * DN_HEAD_DIM:(hd + 1) * DN_HEAD_DIM].reshape(nc, c, DN_HEAD_DIM)
             for hd in range(nh)], axis=0)

    def phase1_stages():
        e = {}

        def decays():
            gb_col = gcol_ref[...].reshape(nc, c, LANES)
            tri_b = jnp.broadcast_to(tri, (nc, c, c))
            gc_col = sum(_bmm(tri_b, part) for part in _split_bf16(gb_col, 3))
            gb_row = grow_ref[...]
            gc_row = sum(jnp.dot(part, tri_t, preferred_element_type=F32)
                         for part in _split_bf16(gb_row.reshape(nc * AB_ROWS, c), 3)
                         ).reshape(nc, AB_ROWS, c)

            def col_form(src, lane):
                return jnp.concatenate(
                    [jnp.broadcast_to(src[:, :, lane + hd:lane + hd + 1], (nc, c, LANES))
                     for hd in range(nh)], axis=0)

            def row_form(src, row):
                return jnp.concatenate(
                    [src[:, row + hd:row + hd + 1, :] for hd in range(nh)], axis=0)

            e["g_c"] = col_form(gc_col, 0)
            e["beta_c"] = col_form(gb_col, nh)
            e["g_r"] = row_form(gc_row, 0)
            e["beta_r"] = row_form(gb_row, nh)
            e["g_last"] = e["g_c"][:, c - 1:c, :]
            diff = e["g_c"][:, :, :c] - e["g_r"]
            e["decay"] = jnp.where(lower, jnp.exp(jnp.where(lower, diff, 0.0)), 0.0)

        def gram_kk():
            e["q"], e["k"], e["v"] = heads_major(q_ref), heads_major(k_ref), heads_major(v_ref)
            l_strict = jnp.where(
                strict, _bmm_nt(e["k"], e["k"]) * e["decay"] * e["beta_c"][:, :, :c], 0.0)
            e["inv"] = eye - l_strict
            e["power"] = l_strict.astype(BF16)

        def gram_qk():
            e["attn"] = jnp.where(lower, _bmm_nt(e["q"], e["k"]) * e["decay"], 0.0).astype(BF16)

        def next_power():
            e["power"] = _bmm(e["power"], e["power"]).astype(BF16)

        def apply_power():
            e["inv"] = e["inv"] + _bmm(e["inv"].astype(BF16), e["power"])

        def solve_u():
            e["t_beta"] = e["inv"] * e["beta_r"]
            e["u"] = _bmm(e["t_beta"].astype(BF16), e["v"])

        def solve_w():
            e["w"] = _bmm((e["t_beta"] * jnp.exp(e["g_r"])).astype(BF16), e["k"]).astype(BF16)

        def commit():
            q_decay = (e["q"].astype(F32) * jnp.exp(e["g_c"])).astype(BF16)
            lhs_ref[...] = jnp.concatenate([e["w"], q_decay], axis=1)
            attn_ref[...] = e["attn"]
            u_ref[...] = e["u"]
            kdec_ref[...] = (e["k"].astype(F32) * jnp.exp(e["g_last"] - e["g_c"])).astype(BF16)
            sdec_ref[...] = jnp.exp(e["g_last"])

        levels = [next_power, apply_power] * (c.bit_length() - 2)
        return [decays, gram_kk, gram_qk] + levels + [solve_u, solve_w], commit

    def phase2_stages():
        def chunk_steps(n):
            idx = [hd * nc + n for hd in range(nh)]
            e = {}

            def through_state():
                e["s_old"] = [jnp.where(keep_state, state_ref[hd], 0.0) if n == 0
                              else state_ref[hd] for hd in range(nh)]
                e["ws_qs"] = [jnp.dot(lhs_ref[b], e["s_old"][hd].astype(BF16),
                                      preferred_element_type=F32) for hd, b in enumerate(idx)]

            def update():
                v_new = [(u_ref[b] - e["ws_qs"][hd][:c]).astype(BF16) for hd, b in enumerate(idx)]
                for hd, b in enumerate(idx):
                    oraw_ref[n * c:(n + 1) * c, hd * DN_HEAD_DIM:(hd + 1) * DN_HEAD_DIM] = (
                        e["ws_qs"][hd][c:]
                        + jnp.dot(attn_ref[b], v_new[hd], preferred_element_type=F32))
                for hd, b in enumerate(idx):
                    state_ref[hd] = e["s_old"][hd] * sdec_ref[b] + lax.dot_general(
                        kdec_ref[b], v_new[hd], TN_DIMS, preferred_element_type=F32)
            return [through_state, update]

        def gated_norm():
            for hd in range(nh):
                hs = slice(hd * DN_HEAD_DIM, (hd + 1) * DN_HEAD_DIM)
                o = oraw_ref[:, hs]
                ms = jnp.mean(o * o, axis=-1, keepdims=True)
                o = o * lax.rsqrt(ms + EPS) * nw_ref[...]
                o_ref[:, hs] = (o * z_ref[:, hs].astype(F32)).astype(BF16)

        return [fn for n in range(nc) for fn in chunk_steps(n)] + [gated_norm]

    @pl.when(t == 0)
    def _():
        state_ref[...] = jnp.zeros(state_ref.shape, F32)
        stages, commit = phase1_stages()
        for fn in stages:
            fn()
        commit()

    @pl.when((t > 0) & (t < tiles))
    def _():
        stages, commit = phase1_stages()
        others = phase2_stages()
        done = 0
        for n, fn in enumerate(stages):
            fn()
            upto = -(-(n + 1) * len(others) // len(stages))
            for other in others[done:upto]:
                other()
            done = upto
        commit()

    @pl.when(t == tiles)
    def _():
        for fn in phase2_stages():
            fn()


def _deltanet(qa, ka, va, za, gcol, grow, dn_norm_w, batch, seq_len):
    tb = TB_DN
    assert seq_len % tb == 0 and tb % CHUNK == 0
    nt = seq_len // tb
    tiles = batch * nt
    nc = tb // CHUNK
    nb = DN_HEADS * nc
    grow3 = grow.reshape(AB_ROWS, -1, CHUNK).transpose(1, 0, 2)
    cur = lambda w: pl.BlockSpec((tb, w), lambda t: (jnp.minimum(t, tiles - 1), 0))
    prev = lambda w: pl.BlockSpec((tb, w), lambda t: (jnp.maximum(t - 1, 0), 0))
    return pl.pallas_call(
        functools.partial(_deltanet_kernel, tiles=tiles, tiles_per_seq=nt),
        grid=(tiles + 1,),
        in_specs=[cur(DN_WIDTH)] * 3 + [
            prev(DN_WIDTH), cur(LANES),
            pl.BlockSpec((nc, AB_ROWS, CHUNK), lambda t: (jnp.minimum(t, tiles - 1), 0, 0)),
            _resident((1, DN_HEAD_DIM))],
        out_specs=prev(DN_WIDTH),
        out_shape=jax.ShapeDtypeStruct((batch * seq_len, DN_WIDTH), BF16),
        scratch_shapes=[pltpu.VMEM((DN_HEADS, DN_HEAD_DIM, DN_HEAD_DIM), F32),
                        pltpu.VMEM((tb, DN_WIDTH), F32),
                        pltpu.VMEM((nb, 2 * CHUNK, DN_HEAD_DIM), BF16),
                        pltpu.VMEM((nb, CHUNK, CHUNK), BF16),
                        pltpu.VMEM((nb, CHUNK, DN_HEAD_DIM), F32),
                        pltpu.VMEM((nb, CHUNK, DN_HEAD_DIM), BF16),
                        pltpu.VMEM((nb, 1, DN_HEAD_DIM), F32)],
        compiler_params=pltpu.CompilerParams(
            dimension_semantics=("arbitrary",), vmem_limit_bytes=VMEM_LIMIT),
        name="gated_deltanet",
    )(qa, ka, va, za, gcol, grow3, dn_norm_w.astype(F32).reshape(1, DN_HEAD_DIM))


def _diffattn_kernel(lq1_ref, lk1_ref, lq2_ref, lk2_ref, q_ref, k_ref, v_ref, z_ref, nw_ref,
                     o_ref, m_ref, l_ref, acc_ref):
    tq = q_ref.shape[0]
    tk = TK_DA
    qi = pl.program_id(2)
    chains = [(hd, comp) for hd in range(HEADS_DA) for comp in range(2)]

    lane = lax.broadcasted_iota(jnp.int32, (1, LANES), 1)
    first_half = lane < DA_QK_DIM
    q_chain = []
    for hd in range(HEADS_DA):
        q = q_ref[:, hd * LANES:(hd + 1) * LANES]
        zero = jnp.zeros_like(q)
        q_chain += [jnp.where(first_half, q, zero), jnp.where(first_half, zero, q)]

    def attend(start, row0=0, nrows=None, width=None, masked=False, first=False):
        nrows = tq if nrows is None else nrows
        width = tk if width is None else width
        rows = slice(row0, row0 + nrows)
        kblk = [k_ref[pl.ds(start, width), hd * LANES:(hd + 1) * LANES] for hd in range(HEADS_DA)]
        ones = jnp.ones((width, LANES), BF16)
        vblk = [jnp.concatenate([v_ref[pl.ds(start, width), hd * LANES:(hd + 1) * LANES], ones],
                                axis=1) for hd in range(HEADS_DA)]
        scores = [lax.dot_general(q_chain[c][rows], kblk[hd], NT_DIMS, preferred_element_type=F32)
                  for c, (hd, _) in enumerate(chains)]
        if masked:
            rq = lax.broadcasted_iota(jnp.int32, (nrows, width), 0) + row0
            ck = lax.broadcasted_iota(jnp.int32, (nrows, width), 1)
            scores = [jnp.where(ck <= rq, s, NEG_INF) for s in scores]
        probs = []
        for c, s in enumerate(scores):
            m_new = jnp.max(s, axis=-1, keepdims=True)
            if first:
                m_new = jnp.broadcast_to(m_new, (nrows, LANES))
            else:
                m_prev = m_ref[c, rows, :]
                m_new = jnp.maximum(m_prev, m_new)
                alpha = jnp.exp2(m_prev - m_new)
                l_ref[c, rows, :] = alpha * l_ref[c, rows, :]
                acc_ref[c, rows, :] = alpha * acc_ref[c, rows, :]
            probs.append(jnp.exp2(
                (s - jnp.concatenate([m_new] * (width // LANES), axis=1)).astype(BF16)))
            m_ref[c, rows, :] = m_new
        for c, (hd, _) in enumerate(chains):
            pv = jnp.dot(probs[c], vblk[hd], preferred_element_type=F32)
            if first:
                acc_ref[c, rows, :] = pv[:, :DA_V_DIM]
                l_ref[c, rows, :] = pv[:, DA_V_DIM:]
            else:
                acc_ref[c, rows, :] += pv[:, :DA_V_DIM]
                l_ref[c, rows, :] += pv[:, DA_V_DIM:]

    diag0 = pl.multiple_of(qi * tq, tq)
    for r0 in range(0, tq, DIAG_BAND):
        attend(diag0, row0=r0, nrows=DIAG_BAND, width=r0 + DIAG_BAND, masked=True, first=True)

    def body(j, carry):
        attend(pl.multiple_of(j * tk, tk))
        return carry

    lax.fori_loop(0, qi * (tq // tk), body, 0)

    lam = (jnp.exp(jnp.sum(lq1_ref[...] * lk1_ref[...], axis=-1, keepdims=True))
           - jnp.exp(jnp.sum(lq2_ref[...] * lk2_ref[...], axis=-1, keepdims=True))
           + LAMBDA_INIT)
    for hd in range(HEADS_DA):
        hs = slice(hd * LANES, (hd + 1) * LANES)
        o = acc_ref[2 * hd] / l_ref[2 * hd] - lam * (acc_ref[2 * hd + 1] / l_ref[2 * hd + 1])
        ms = jnp.mean(o * o, axis=-1, keepdims=True)
        o = o * lax.rsqrt(ms + EPS) * nw_ref[...] * (1.0 - LAMBDA_INIT)
        o_ref[:, hs] = (o * z_ref[:, hs].astype(F32)).astype(BF16)


def _diff_attention(qb, kb, vb, zb, lambda_q1, lambda_k1, lambda_q2, lambda_k2, da_norm_w,
                    batch, seq_len):
    tq = TQ_DA
    assert tq % TK_DA == 0 and tq % DIAG_BAND == 0 and seq_len % tq == 0 and DA_HEADS % HEADS_DA == 0
    nq = seq_len // tq
    width = HEADS_DA * LANES
    lam_vec = lambda v: v.astype(F32).reshape(1, DA_QK_DIM)
    q_blk = pl.BlockSpec((tq, width), lambda b, h, i: (b * nq + i, h))
    kv_blk = pl.BlockSpec((seq_len, width), lambda b, h, i: (b, h))
    stat = pltpu.VMEM((2 * HEADS_DA, tq, LANES), F32)
    return pl.pallas_call(
        _diffattn_kernel,
        grid=(batch, DA_HEADS // HEADS_DA, nq),
        in_specs=[_resident((1, DA_QK_DIM))] * 4 + [q_blk, kv_blk, kv_blk, q_blk,
                                                    _resident((1, DA_V_DIM))],
        out_specs=q_blk,
        out_shape=jax.ShapeDtypeStruct((batch * seq_len, DA_WIDTH), BF16),
        scratch_shapes=[stat, stat, pltpu.VMEM((2 * HEADS_DA, tq, DA_V_DIM), F32)],
        compiler_params=pltpu.CompilerParams(
            dimension_semantics=("arbitrary", "arbitrary", "arbitrary"),
            vmem_limit_bytes=VMEM_LIMIT),
        name="diff_attention",
    )(lam_vec(lambda_q1), lam_vec(lambda_k1), lam_vec(lambda_q2), lam_vec(lambda_k2),
      qb, kb, vb, zb, da_norm_w.astype(F32).reshape(1, DA_V_DIM))


def _output_kernel(oa_ref, ob_ref, x_ref, nw_ref, wga_ref, wgb_ref, woa_ref, wob_ref, wo_ref,
                   out_ref):
    x = x_ref[...]
    ms = jnp.mean(x * x, axis=-1, keepdims=True)
    h = (x * lax.rsqrt(ms + EPS) * nw_ref[...]).astype(BF16)
    gate_a = _sigmoid(jnp.dot(h, wga_ref[...], preferred_element_type=F32))
    gate_b = _sigmoid(jnp.dot(h, wgb_ref[...], preferred_element_type=F32))
    y_a = jnp.dot(oa_ref[...], woa_ref[...], preferred_element_type=F32)
    y_b = jnp.dot(ob_ref[...], wob_ref[...], preferred_element_type=F32)
    y = gate_a * y_a + gate_b * y_b
    out_ref[...] = x + jnp.dot(y.astype(BF16), wo_ref[...], preferred_element_type=F32)


def _output_projection(oa, ob, x2, norm_w, w_gates, w_out_a, w_out_b, w_out):
    m = x2.shape[0]
    tm = TM_OUT
    assert m % tm == 0
    row_blk = lambda w: pl.BlockSpec((tm, w), lambda i: (i, 0))
    return pl.pallas_call(
        _output_kernel,
        grid=(m // tm,),
        in_specs=[row_blk(DN_WIDTH), row_blk(DA_WIDTH), row_blk(D_MODEL), _resident((1, D_MODEL)),
                  _resident((D_MODEL, D_MODEL)), _resident((D_MODEL, D_MODEL)),
                  _resident((DN_WIDTH, D_MODEL)), _resident((DA_WIDTH, D_MODEL)),
                  _resident((D_MODEL, D_MODEL))],
        out_specs=row_blk(D_MODEL),
        out_shape=jax.ShapeDtypeStruct((m, D_MODEL), F32),
        compiler_params=pltpu.CompilerParams(
            dimension_semantics=("arbitrary",), vmem_limit_bytes=VMEM_LIMIT),
        name="output_projection",
    )(oa, ob, x2, norm_w.astype(F32).reshape(1, D_MODEL), *w_gates,
      w_out_a.astype(BF16), w_out_b.astype(BF16), w_out.astype(BF16))


def kernel(x, norm_w, w_in, conv_w, a_log, dt_bias, dn_norm_w, q_norm_w, k_norm_w,
           lambda_q1, lambda_k1, lambda_q2, lambda_k2, da_norm_w, w_out_a, w_out_b, w_out):
    batch, seq_len, d_model = x.shape
    assert d_model == D_MODEL
    x2 = x.reshape(batch * seq_len, d_model)
    (qa, ka, va, za, gcol, grow, qb, kb, vb, zb), w_gates = _in_projection(
        x2, norm_w, w_in, conv_w, a_log, dt_bias, q_norm_w, k_norm_w, seq_len)
    oa = _deltanet(qa, ka, va, za, gcol, grow, dn_norm_w, batch, seq_len)
    ob = _diff_attention(qb, kb, vb, zb, lambda_q1, lambda_k1, lambda_q2, lambda_k2, da_norm_w,
                         batch, seq_len)
    out = _output_projection(oa, ob, x2, norm_w, w_gates, w_out_a, w_out_b, w_out)
    return out.reshape(batch, seq_len, d_model)
```

```python
import functools
import math

import jax
import jax.numpy as jnp
from jax import lax
from jax.experimental import pallas as pl
from jax.experimental.pallas import tpu as pltpu

F32 = jnp.float32
BF16 = jnp.bfloat16

D_MODEL = 1024
DN_HEADS = 4
DN_HEAD_DIM = 128
DN_WIDTH = DN_HEADS * DN_HEAD_DIM
CONV_WIDTH = 4
CHUNK = 64
DA_HEADS = 4
DA_QK_DIM = 64
DA_V_DIM = 2 * DA_QK_DIM
DA_WIDTH = DA_HEADS * DA_V_DIM
DA_QK_WIDTH = DA_HEADS * 2 * DA_QK_DIM
LAMBDA_INIT = 0.8 - 0.6 * math.exp(-0.3 * 0)
EPS = 1e-6
NEG_INF = -1e30
LOG2_E = math.log2(math.e)

LANES = 128
SUBLANES = 8
BF16_ROWS = 16
AB_ROWS = BF16_ROWS
VMEM_LIMIT = 48 * 1024 * 1024

TM_IN = 512
TB_DN = 512
PACK = 4
TQ_DA = 1024
DIAG_BAND = 256
TK_DA = 1024
HEADS_DA = 2
TM_OUT = 512

NT_DIMS = (((1,), (1,)), ((), ()))
TN_DIMS = (((0,), (0,)), ((), ()))


def _sigmoid(v):
    return 0.5 + 0.5 * jnp.tanh(0.5 * v)


def _silu(v):
    h = 0.5 * v
    return h + h * jnp.tanh(h)


def _softplus(v):
    return jnp.maximum(v, 0.0) + jnp.log1p(jnp.exp(-jnp.abs(v)))


def _resident(shape):
    zeros = (0,) * len(shape)
    return pl.BlockSpec(shape, lambda *_: zeros, pipeline_mode=pl.Buffered(1))


def _inproj_kernel(x_ref, nw_ref, wqkv_ref, convw_ref, wza_ref, wab_ref, wabt_ref,
                   alog_l_ref, dtb_l_ref, alog_s_ref, dtb_s_ref, qnw_ref, knw_ref,
                   wqb_ref, wkb_ref, wvb_ref, wzb_ref,
                   qa_ref, ka_ref, va_ref, za_ref, gcol_ref, grow_ref,
                   qb_ref, kb_ref, vb_ref, zb_ref,
                   conv_buf, *, tiles_per_seq):
    tm = x_ref.shape[0]
    half = tm // 2
    i = pl.program_id(0)
    head0 = SUBLANES - (CONV_WIDTH - 1)
    slab = 512

    @pl.when(i % tiles_per_seq == 0)
    def _():
        conv_buf[0:SUBLANES, :] = jnp.zeros((SUBLANES, 3 * DN_WIDTH), F32)

    def normed(r0):
        x = x_ref[r0:r0 + half, :]
        ms = jnp.mean(x * x, axis=-1, keepdims=True)
        return (x * lax.rsqrt(ms + EPS) * nw_ref[...]).astype(BF16)

    def tasks(h, r0):
        rows = slice(r0, r0 + half)

        def conv_qkv(s, out_ref):
            cs = slice(s * DN_WIDTH, (s + 1) * DN_WIDTH)

            res = []

            def matmul():
                res.append(jnp.dot(h, wqkv_ref[:, cs], preferred_element_type=F32))
                conv_buf[SUBLANES + r0:SUBLANES + r0 + half, cs] = res[0]

            def epilogue():
                acc = convw_ref[CONV_WIDTH - 1:CONV_WIDTH, cs] * res[0]
                for j in range(CONV_WIDTH - 1):
                    acc = acc + (convw_ref[j:j + 1, cs]
                                 * conv_buf[head0 + j + r0:head0 + j + r0 + half, cs])
                y = _silu(acc)
                if s == 2:
                    out_ref[rows, :] = y.astype(BF16)
                    return
                scale = DN_HEAD_DIM ** -0.5 if s == 0 else 1.0
                for hd in range(DN_HEADS):
                    hs = slice(hd * DN_HEAD_DIM, (hd + 1) * DN_HEAD_DIM)
                    blk = y[:, hs]
                    ss = jnp.sum(blk * blk, axis=-1, keepdims=True)
                    out_ref[rows, hs] = (blk * (lax.rsqrt(ss + EPS) * scale)).astype(BF16)
            return matmul, [epilogue]

        def plain(w_ref, c0, out_ref, act):
            res = []

            def matmul():
                res.append(jnp.dot(h, w_ref[:, c0:c0 + slab], preferred_element_type=F32))

            def chunk(n):
                def run():
                    cs = slice(n * LANES, (n + 1) * LANES)
                    out_ref[rows, c0 + n * LANES:c0 + (n + 1) * LANES] = (
                        act(res[0][:, cs]).astype(BF16))
                return run
            return matmul, [chunk(n) for n in range(slab // LANES)]

        def qk_norm(w_ref, nrm_ref, out_ref, scale):
            res = []

            def matmul():
                res.append(jnp.dot(h, w_ref[...], preferred_element_type=F32))

            def chunk(hd):
                def run():
                    lo = lax.broadcasted_iota(jnp.int32, (half, LANES), 1) < DA_QK_DIM
                    blk = res[0][:, hd * LANES:(hd + 1) * LANES]
                    sq = blk * blk
                    s_lo = jnp.sum(jnp.where(lo, sq, 0.0), axis=-1, keepdims=True)
                    s_hi = jnp.sum(jnp.where(lo, 0.0, sq), axis=-1, keepdims=True)
                    msq = jnp.where(lo, s_lo, s_hi) * (1.0 / DA_QK_DIM)
                    out_ref[rows, hd * LANES:(hd + 1) * LANES] = (
                        blk * lax.rsqrt(msq + EPS) * nrm_ref[...] * scale).astype(BF16)
                return run
            return matmul, [chunk(hd) for hd in range(DA_HEADS)]

        def decay_beta():
            def matmul():
                ab = jnp.dot(h, wab_ref[...], preferred_element_type=F32)
                lane = lax.broadcasted_iota(jnp.int32, ab.shape, 1)
                g_l = -jnp.exp(alog_l_ref[...]) * _softplus(ab + dtb_l_ref[...])
                gcol_ref[rows, :] = jnp.where(lane < DN_HEADS, g_l, _sigmoid(ab))
                abt = lax.dot_general(wabt_ref[...], h, NT_DIMS,
                                      preferred_element_type=F32)
                row = lax.broadcasted_iota(jnp.int32, abt.shape, 0)
                g_s = -jnp.exp(alog_s_ref[...]) * _softplus(abt + dtb_s_ref[...])
                grow_ref[:, rows] = jnp.where(row < DN_HEADS, g_s, _sigmoid(abt))
            return matmul, []

        ident = lambda v: v
        return [conv_qkv(0, qa_ref), plain(wvb_ref, 0, vb_ref, ident),
                conv_qkv(1, ka_ref), plain(wza_ref, 0, za_ref, _silu),
                conv_qkv(2, va_ref), plain(wzb_ref, 0, zb_ref, _silu),
                qk_norm(wqb_ref, qnw_ref, qb_ref, DA_QK_DIM ** -0.5 * LOG2_E),
                qk_norm(wkb_ref, knw_ref, kb_ref, 1.0), decay_beta()]

    def run_tasks(task_list, extra=None):
        for n, (matmul, chunks) in enumerate(task_list):
            matmul()
            for fn in chunks:
                fn()
            if extra is not None and n == 6:
                extra()

    h_second = []
    run_tasks(tasks(normed(0), 0), extra=lambda: h_second.append(normed(half)))
    run_tasks(tasks(h_second[0], half))
    conv_buf[0:SUBLANES, :] = conv_buf[tm:tm + SUBLANES, :]


def _in_projection(x2, norm_w, w_in, conv_w, a_log, dt_bias, q_norm_w, k_norm_w, seq_len):
    m = x2.shape[0]
    tm = TM_IN
    assert m % tm == 0 and seq_len % tm == 0
    o = 0
    cols = {}
    for name, width in (("qkv", 3 * DN_WIDTH), ("za", DN_WIDTH), ("a", DN_HEADS), ("b", DN_HEADS),
                        ("qb", DA_QK_WIDTH), ("kb", DA_QK_WIDTH), ("vb", DA_WIDTH), ("zb", DA_WIDTH),
                        ("ga", D_MODEL), ("gb", D_MODEL)):
        cols[name] = w_in[:, o:o + width]
        o += width
    assert o == w_in.shape[1]
    wb = {k: v.astype(BF16) for k, v in cols.items()}
    w_ab = jnp.concatenate([cols["a"], cols["b"]], axis=1)
    w_ab_l = jnp.pad(w_ab, ((0, 0), (0, LANES - 2 * DN_HEADS))).astype(BF16)
    w_ab_s = jnp.pad(w_ab.T, ((0, AB_ROWS - 2 * DN_HEADS), (0, 0))).astype(BF16)
    alog_l = jnp.pad(a_log.astype(F32), (0, LANES - DN_HEADS)).reshape(1, LANES)
    dtb_l = jnp.pad(dt_bias.astype(F32), (0, LANES - DN_HEADS)).reshape(1, LANES)
    alog_s = jnp.pad(a_log.astype(F32), (0, AB_ROWS - DN_HEADS)).reshape(AB_ROWS, 1)
    dtb_s = jnp.pad(dt_bias.astype(F32), (0, AB_ROWS - DN_HEADS)).reshape(AB_ROWS, 1)
    qnw = jnp.tile(q_norm_w.astype(F32), 2).reshape(1, LANES)
    knw = jnp.tile(k_norm_w.astype(F32), 2).reshape(1, LANES)

    row_blk = lambda w: pl.BlockSpec((tm, w), lambda i: (i, 0))
    in_specs = [
        row_blk(D_MODEL), _resident((1, D_MODEL)),
        _resident((D_MODEL, 3 * DN_WIDTH)), _resident((CONV_WIDTH, 3 * DN_WIDTH)),
        _resident((D_MODEL, DN_WIDTH)), _resident((D_MODEL, LANES)), _resident((AB_ROWS, D_MODEL)),
        _resident((1, LANES)), _resident((1, LANES)), _resident((AB_ROWS, 1)), _resident((AB_ROWS, 1)),
        _resident((1, LANES)), _resident((1, LANES)),
        _resident((D_MODEL, DA_QK_WIDTH)), _resident((D_MODEL, DA_QK_WIDTH)),
        _resident((D_MODEL, DA_WIDTH)), _resident((D_MODEL, DA_WIDTH)),
    ]
    out_shapes = [jax.ShapeDtypeStruct((m, DN_WIDTH), BF16)] * 4 + [
        jax.ShapeDtypeStruct((m, LANES), F32), jax.ShapeDtypeStruct((AB_ROWS, m), F32)] + [
        jax.ShapeDtypeStruct((m, DA_WIDTH), BF16)] * 4
    out_specs = [row_blk(DN_WIDTH)] * 4 + [
        row_blk(LANES), pl.BlockSpec((AB_ROWS, tm), lambda i: (0, i))] + [
        row_blk(DA_WIDTH)] * 4
    outs = pl.pallas_call(
        functools.partial(_inproj_kernel, tiles_per_seq=seq_len // tm),
        grid=(m // tm,),
        in_specs=in_specs,
        out_specs=out_specs,
        out_shape=out_shapes,
        scratch_shapes=[pltpu.VMEM((tm + SUBLANES, 3 * DN_WIDTH), F32)],
        compiler_params=pltpu.CompilerParams(
            dimension_semantics=("arbitrary",), vmem_limit_bytes=VMEM_LIMIT),
        name="in_projection",
    )(x2, norm_w.astype(F32).reshape(1, D_MODEL), wb["qkv"], conv_w.astype(F32), wb["za"],
      w_ab_l, w_ab_s, alog_l, dtb_l, alog_s, dtb_s, qnw, knw,
      wb["qb"], wb["kb"], wb["vb"], wb["zb"])
    return outs, (wb["ga"], wb["gb"])


def _bmm(a, b):
    return jnp.einsum('nij,njk->nik', a, b, preferred_element_type=F32)


def _bmm_nt(a, b):
    return jnp.einsum('nid,njd->nij', a, b, preferred_element_type=F32)


def _split_bf16(x, parts):
    out = []
    for _ in range(parts - 1):
        hi = x.astype(BF16)
        out.append(hi)
        x = x - hi.astype(F32)
    out.append(x.astype(BF16))
    return out


def _deltanet_kernel(q_ref, k_ref, v_ref, z_ref, gcol_ref, grow_ref, nw_ref, o_ref,
                     state_ref, oraw_ref, lhs_ref, attn_ref, u_ref, kdec_ref, sdec_ref,
                     *, tiles, tiles_per_seq):
    tb = q_ref.shape[0]
    c = CHUNK
    nc = tb // c
    nh = DN_HEADS
    t = pl.program_id(0)
    keep_state = (t - 1) % tiles_per_seq != 0

    ri = lax.broadcasted_iota(jnp.int32, (c, c), 0)
    ci = lax.broadcasted_iota(jnp.int32, (c, c), 1)
    lower = ri >= ci
    strict = ri > ci
    eye = (ri == ci).astype(F32)
    tri = lower.astype(BF16)
    tri_t = (ri <= ci).astype(BF16)

    def heads_major(ref):
        return jnp.concatenate(
            [ref[:, hd * DN_HEAD_DIM:(hd + 1) * DN_HEAD_DIM].reshape(nc, c, DN_HEAD_DIM)
             for hd in range(nh)], axis=0)

    def phase1_stages():
        e = {}
        block_of_lane = lax.broadcasted_iota(jnp.int32, (1, PACK * c), 1) // c

        def pack(x):
            g = x.shape[0] // PACK
            return jnp.concatenate([x[i * g:(i + 1) * g] for i in range(PACK)], axis=-1)

        def unpack(x):
            return jnp.concatenate([x[:, :, i * c:(i + 1) * c] for i in range(PACK)], axis=0)

        def block_diagonal(x):
            zero = jnp.zeros_like(x)
            return jnp.concatenate(
                [jnp.where(block_of_lane == i, x, zero) for i in range(PACK)], axis=1)

        def decays():
            gb_col = gcol_ref[...].reshape(nc, c, LANES)
            tri_b = jnp.broadcast_to(tri, (nc, c, c))
            gc_col = sum(_bmm(tri_b, part) for part in _split_bf16(gb_col, 3))
            gb_row = grow_ref[...]
            gc_row = sum(jnp.dot(part, tri_t, preferred_element_type=F32)
                         for part in _split_bf16(gb_row.reshape(nc * AB_ROWS, c), 3)
                         ).reshape(nc, AB_ROWS, c)

            def col_form(src, lane):
                return jnp.concatenate(
                    [jnp.broadcast_to(src[:, :, lane + hd:lane + hd + 1], (nc, c, LANES))
                     for hd in range(nh)], axis=0)

            def row_form(src, row):
                return jnp.concatenate(
                    [src[:, row + hd:row + hd + 1, :] for hd in range(nh)], axis=0)

            e["g_c"] = col_form(gc_col, 0)
            e["beta_c"] = col_form(gb_col, nh)
            e["g_r"] = row_form(gc_row, 0)
            e["beta_r"] = row_form(gb_row, nh)
            e["g_last"] = e["g_c"][:, c - 1:c, :]
            diff = e["g_c"][:, :, :c] - e["g_r"]
            e["decay"] = jnp.where(lower, jnp.exp(jnp.where(lower, diff, 0.0)), 0.0)

        def gram_kk():
            e["q"], e["k"], e["v"] = heads_major(q_ref), heads_major(k_ref), heads_major(v_ref)
            l_strict = jnp.where(
                strict, _bmm_nt(e["k"], e["k"]) * e["decay"] * e["beta_c"][:, :, :c], 0.0)
            groups = l_strict.shape[0] // PACK
            halves = (slice(0, groups // 2), slice(groups // 2, groups))
            e["inv"] = [pack(eye - l_strict)[hf] for hf in halves]
            e["power"] = [pack(l_strict).astype(BF16)[hf] for hf in halves]

        def gram_qk():
            e["attn"] = jnp.where(lower, _bmm_nt(e["q"], e["k"]) * e["decay"], 0.0).astype(BF16)

        def square(hf):
            def run():
                e["power"][hf] = _bmm(e["power"][hf],
                                      block_diagonal(e["power"][hf])).astype(BF16)
            return run

        def inverse_level(last, hf):
            def run():
                power = e["power"][hf]
                lhs = e["inv"][hf].astype(BF16)
                if not last:
                    lhs = jnp.concatenate([lhs, power], axis=1)
                out = _bmm(lhs, block_diagonal(power))
                e["inv"][hf] = e["inv"][hf] + out[:, :c]
                if not last:
                    e["power"][hf] = out[:, c:].astype(BF16)
            return run

        def solve_u():
            t_inv = unpack(jnp.concatenate(e["inv"], axis=0))
            e["t_beta"] = t_inv * e["beta_r"]
            e["u"] = _bmm(e["t_beta"].astype(BF16), e["v"])

        def solve_w():
            e["w"] = _bmm((e["t_beta"] * jnp.exp(e["g_r"])).astype(BF16), e["k"]).astype(BF16)

        def commit():
            q_decay = (e["q"].astype(F32) * jnp.exp(e["g_c"])).astype(BF16)
            lhs_ref[...] = jnp.concatenate([e["w"], q_decay], axis=1)
            attn_ref[...] = e["attn"]
            u_ref[...] = e["u"]
            kdec_ref[...] = (e["k"].astype(F32) * jnp.exp(e["g_last"] - e["g_c"])).astype(BF16)
            sdec_ref[...] = jnp.exp(e["g_last"])

        n_levels = c.bit_length() - 2
        levels = [inverse_level(n == n_levels - 1, hf) for n in range(n_levels) for hf in (0, 1)]
        return ([decays, gram_kk, square(0), square(1), gram_qk] + levels
                + [solve_u, solve_w]), commit

    def phase2_stages():
        def chunk_steps(n):
            idx = [hd * nc + n for hd in range(nh)]
            e = {}

            def through_state():
                e["s_old"] = [jnp.where(keep_state, state_ref[hd], 0.0) if n == 0
                              else state_ref[hd] for hd in range(nh)]
                e["ws_qs"] = [jnp.dot(lhs_ref[b], e["s_old"][hd].astype(BF16),
                                      preferred_element_type=F32) for hd, b in enumerate(idx)]

            def update():
                v_new = [(u_ref[b] - e["ws_qs"][hd][:c]).astype(BF16) for hd, b in enumerate(idx)]
                for hd, b in enumerate(idx):
                    oraw_ref[n * c:(n + 1) * c, hd * DN_HEAD_DIM:(hd + 1) * DN_HEAD_DIM] = (
                        e["ws_qs"][hd][c:]
                        + jnp.dot(attn_ref[b], v_new[hd], preferred_element_type=F32))
                for hd, b in enumerate(idx):
                    state_ref[hd] = e["s_old"][hd] * sdec_ref[b] + lax.dot_general(
                        kdec_ref[b], v_new[hd], TN_DIMS, preferred_element_type=F32)
            return [through_state, update]

        def gated_norm():
            for hd in range(nh):
                hs = slice(hd * DN_HEAD_DIM, (hd + 1) * DN_HEAD_DIM)
                o = oraw_ref[:, hs]
                ms = jnp.mean(o * o, axis=-1, keepdims=True)
                o = o * lax.rsqrt(ms + EPS) * nw_ref[...]
                o_ref[:, hs] = (o * z_ref[:, hs].astype(F32)).astype(BF16)

        return [fn for n in range(nc) for fn in chunk_steps(n)] + [gated_norm]

    @pl.when(t == 0)
    def _():
        state_ref[...] = jnp.zeros(state_ref.shape, F32)
        stages, commit = phase1_stages()
        for fn in stages:
            fn()
        commit()

    @pl.when((t > 0) & (t < tiles))
    def _():
        stages, commit = phase1_stages()
        others = phase2_stages()
        done = 0
        for n, fn in enumerate(stages):
            fn()
            upto = -(-(n + 1) * len(others) // len(stages))
            for other in others[done:upto]:
                other()
            done = upto
        commit()

    @pl.when(t == tiles)
    def _():
        for fn in phase2_stages():
            fn()


def _deltanet(qa, ka, va, za, gcol, grow, dn_norm_w, batch, seq_len):
    tb = TB_DN
    assert seq_len % tb == 0 and tb % CHUNK == 0
    nt = seq_len // tb
    tiles = batch * nt
    nc = tb // CHUNK
    nb = DN_HEADS * nc
    grow3 = grow.reshape(AB_ROWS, -1, CHUNK).transpose(1, 0, 2)
    cur = lambda w: pl.BlockSpec((tb, w), lambda t: (jnp.minimum(t, tiles - 1), 0))
    prev = lambda w: pl.BlockSpec((tb, w), lambda t: (jnp.maximum(t - 1, 0), 0))
    return pl.pallas_call(
        functools.partial(_deltanet_kernel, tiles=tiles, tiles_per_seq=nt),
        grid=(tiles + 1,),
        in_specs=[cur(DN_WIDTH)] * 3 + [
            prev(DN_WIDTH), cur(LANES),
            pl.BlockSpec((nc, AB_ROWS, CHUNK), lambda t: (jnp.minimum(t, tiles - 1), 0, 0)),
            _resident((1, DN_HEAD_DIM))],
        out_specs=prev(DN_WIDTH),
        out_shape=jax.ShapeDtypeStruct((batch * seq_len, DN_WIDTH), BF16),
        scratch_shapes=[pltpu.VMEM((DN_HEADS, DN_HEAD_DIM, DN_HEAD_DIM), F32),
                        pltpu.VMEM((tb, DN_WIDTH), F32),
                        pltpu.VMEM((nb, 2 * CHUNK, DN_HEAD_DIM), BF16),
                        pltpu.VMEM((nb, CHUNK, CHUNK), BF16),
                        pltpu.VMEM((nb, CHUNK, DN_HEAD_DIM), F32),
                        pltpu.VMEM((nb, CHUNK, DN_HEAD_DIM), BF16),
                        pltpu.VMEM((nb, 1, DN_HEAD_DIM), F32)],
        compiler_params=pltpu.CompilerParams(
            dimension_semantics=("arbitrary",), vmem_limit_bytes=VMEM_LIMIT),
        name="gated_deltanet",
    )(qa, ka, va, za, gcol, grow3, dn_norm_w.astype(F32).reshape(1, DN_HEAD_DIM))


def _diffattn_kernel(lq1_ref, lk1_ref, lq2_ref, lk2_ref, q_ref, k_ref, v_ref, z_ref, nw_ref,
                     o_ref, m_ref, l_ref, acc_ref):
    tq = q_ref.shape[0]
    tk = TK_DA
    qi = pl.program_id(2)
    chains = [(hd, comp) for hd in range(HEADS_DA) for comp in range(2)]

    lane = lax.broadcasted_iota(jnp.int32, (1, LANES), 1)
    first_half = lane < DA_QK_DIM
    q_chain = []
    for hd in range(HEADS_DA):
        q = q_ref[:, hd * LANES:(hd + 1) * LANES]
        zero = jnp.zeros_like(q)
        q_chain += [jnp.where(first_half, q, zero), jnp.where(first_half, zero, q)]

    def attend(start, row0=0, nrows=None, width=None, masked=False, first=False):
        nrows = tq if nrows is None else nrows
        width = tk if width is None else width
        rows = slice(row0, row0 + nrows)
        kblk = [k_ref[pl.ds(start, width), hd * LANES:(hd + 1) * LANES] for hd in range(HEADS_DA)]
        ones = jnp.ones((width, LANES), BF16)
        vblk = [jnp.concatenate([v_ref[pl.ds(start, width), hd * LANES:(hd + 1) * LANES], ones],
                                axis=1) for hd in range(HEADS_DA)]
        scores = [lax.dot_general(q_chain[c][rows], kblk[hd], NT_DIMS, preferred_element_type=F32)
                  for c, (hd, _) in enumerate(chains)]
        if masked:
            rq = lax.broadcasted_iota(jnp.int32, (nrows, width), 0) + row0
            ck = lax.broadcasted_iota(jnp.int32, (nrows, width), 1)
            scores = [jnp.where(ck <= rq, s, NEG_INF) for s in scores]
        probs = []
        for c, s in enumerate(scores):
            m_new = jnp.max(s, axis=-1, keepdims=True)
            if first:
                m_new = jnp.broadcast_to(m_new, (nrows, LANES))
            else:
                m_prev = m_ref[c, rows, :]
                m_new = jnp.maximum(m_prev, m_new)
                alpha = jnp.exp2(m_prev - m_new)
                l_ref[c, rows, :] = alpha * l_ref[c, rows, :]
                acc_ref[c, rows, :] = alpha * acc_ref[c, rows, :]
            probs.append(jnp.exp2(
                (s - jnp.concatenate([m_new] * (width // LANES), axis=1)).astype(BF16)))
            m_ref[c, rows, :] = m_new
        for c, (hd, _) in enumerate(chains):
            pv = jnp.dot(probs[c], vblk[hd], preferred_element_type=F32)
            if first:
                acc_ref[c, rows, :] = pv[:, :DA_V_DIM]
                l_ref[c, rows, :] = pv[:, DA_V_DIM:]
            else:
                acc_ref[c, rows, :] += pv[:, :DA_V_DIM]
                l_ref[c, rows, :] += pv[:, DA_V_DIM:]

    diag0 = pl.multiple_of(qi * tq, tq)
    for r0 in range(0, tq, DIAG_BAND):
        attend(diag0, row0=r0, nrows=DIAG_BAND, width=r0 + DIAG_BAND, masked=True, first=True)

    def body(j, carry):
        attend(pl.multiple_of(j * tk, tk))
        return carry

    lax.fori_loop(0, qi * (tq // tk), body, 0)

    lam = (jnp.exp(jnp.sum(lq1_ref[...] * lk1_ref[...], axis=-1, keepdims=True))
           - jnp.exp(jnp.sum(lq2_ref[...] * lk2_ref[...], axis=-1, keepdims=True))
           + LAMBDA_INIT)
    for hd in range(HEADS_DA):
        hs = slice(hd * LANES, (hd + 1) * LANES)
        o = acc_ref[2 * hd] / l_ref[2 * hd] - lam * (acc_ref[2 * hd + 1] / l_ref[2 * hd + 1])
        ms = jnp.mean(o * o, axis=-1, keepdims=True)
        o = o * lax.rsqrt(ms + EPS) * nw_ref[...] * (1.0 - LAMBDA_INIT)
        o_ref[:, hs] = (o * z_ref[:, hs].astype(F32)).astype(BF16)


def _diff_attention(qb, kb, vb, zb, lambda_q1, lambda_k1, lambda_q2, lambda_k2, da_norm_w,
                    batch, seq_len):
    tq = TQ_DA
    assert tq % TK_DA == 0 and tq % DIAG_BAND == 0 and seq_len % tq == 0 and DA_HEADS % HEADS_DA == 0
    nq = seq_len // tq
    width = HEADS_DA * LANES
    lam_vec = lambda v: v.astype(F32).reshape(1, DA_QK_DIM)
    q_blk = pl.BlockSpec((tq, width), lambda b, h, i: (b * nq + i, h))
    kv_blk = pl.BlockSpec((seq_len, width), lambda b, h, i: (b, h))
    stat = pltpu.VMEM((2 * HEADS_DA, tq, LANES), F32)
    return pl.pallas_call(
        _diffattn_kernel,
        grid=(batch, DA_HEADS // HEADS_DA, nq),
        in_specs=[_resident((1, DA_QK_DIM))] * 4 + [q_blk, kv_blk, kv_blk, q_blk,
                                                    _resident((1, DA_V_DIM))],
        out_specs=q_blk,
        out_shape=jax.ShapeDtypeStruct((batch * seq_len, DA_WIDTH), BF16),
        scratch_shapes=[stat, stat, pltpu.VMEM((2 * HEADS_DA, tq, DA_V_DIM), F32)],
        compiler_params=pltpu.CompilerParams(
            dimension_semantics=("arbitrary", "arbitrary", "arbitrary"),
            vmem_limit_bytes=VMEM_LIMIT),
        name="diff_attention",
    )(lam_vec(lambda_q1), lam_vec(lambda_k1), lam_vec(lambda_q2), lam_vec(lambda_k2),
      qb, kb, vb, zb, da_norm_w.astype(F32).reshape(1, DA_V_DIM))


def _output_kernel(oa_ref, ob_ref, x_ref, nw_ref, wga_ref, wgb_ref, woa_ref, wob_ref, wo_ref,
                   out_ref):
    x = x_ref[...]
    ms = jnp.mean(x * x, axis=-1, keepdims=True)
    h = (x * lax.rsqrt(ms + EPS) * nw_ref[...]).astype(BF16)
    gate_a = _sigmoid(jnp.dot(h, wga_ref[...], preferred_element_type=F32))
    gate_b = _sigmoid(jnp.dot(h, wgb_ref[...], preferred_element_type=F32))
    y_a = jnp.dot(oa_ref[...], woa_ref[...], preferred_element_type=F32)
    y_b = jnp.dot(ob_ref[...], wob_ref[...], preferred_element_type=F32)
    y = gate_a * y_a + gate_b * y_b
    out_ref[...] = x + jnp.dot(y.astype(BF16), wo_ref[...], preferred_element_type=F32)


def _output_projection(oa, ob, x2, norm_w, w_gates, w_out_a, w_out_b, w_out):
    m = x2.shape[0]
    tm = TM_OUT
    assert m % tm == 0
    row_blk = lambda w: pl.BlockSpec((tm, w), lambda i: (i, 0))
    return pl.pallas_call(
        _output_kernel,
        grid=(m // tm,),
        in_specs=[row_blk(DN_WIDTH), row_blk(DA_WIDTH), row_blk(D_MODEL), _resident((1, D_MODEL)),
                  _resident((D_MODEL, D_MODEL)), _resident((D_MODEL, D_MODEL)),
                  _resident((DN_WIDTH, D_MODEL)), _resident((DA_WIDTH, D_MODEL)),
                  _resident((D_MODEL, D_MODEL))],
        out_specs=row_blk(D_MODEL),
        out_shape=jax.ShapeDtypeStruct((m, D_MODEL), F32),
        compiler_params=pltpu.CompilerParams(
            dimension_semantics=("arbitrary",), vmem_limit_bytes=VMEM_LIMIT),
        name="output_projection",
    )(oa, ob, x2, norm_w.astype(F32).reshape(1, D_MODEL), *w_gates,
      w_out_a.astype(BF16), w_out_b.astype(BF16), w_out.astype(BF16))


def kernel(x, norm_w, w_in, conv_w, a_log, dt_bias, dn_norm_w, q_norm_w, k_norm_w,
           lambda_q1, lambda_k1, lambda_q2, lambda_k2, da_norm_w, w_out_a, w_out_b, w_out):
    batch, seq_len, d_model = x.shape
    assert d_model == D_MODEL
    x2 = x.reshape(batch * seq_len, d_model)
    (qa, ka, va, za, gcol, grow, qb, kb, vb, zb), w_gates = _in_projection(
        x2, norm_w, w_in, conv_w, a_log, dt_bias, q_norm_w, k_norm_w, seq_len)
    oa = _deltanet(qa, ka, va, za, gcol, grow, dn_norm_w, batch, seq_len)
    ob = _diff_attention(qb, kb, vb, zb, lambda_q1, lambda_k1, lambda_q2, lambda_k2, da_norm_w,
                         batch, seq_len)
    out = _output_projection(oa, ob, x2, norm_w, w_gates, w_out_a, w_out_b, w_out)
    return out.reshape(batch, seq_len, d_model)
```

```python
import functools
import math

import jax
import jax.numpy as jnp
from jax import lax
from jax.experimental import pallas as pl
from jax.experimental.pallas import tpu as pltpu

F32 = jnp.float32
BF16 = jnp.bfloat16

D_MODEL = 1024
DN_HEADS = 4
DN_HEAD_DIM = 128
DN_WIDTH = DN_HEADS * DN_HEAD_DIM
CONV_WIDTH = 4
CHUNK = 64
DA_HEADS = 4
DA_QK_DIM = 64
DA_V_DIM = 2 * DA_QK_DIM
DA_WIDTH = DA_HEADS * DA_V_DIM
DA_QK_WIDTH = DA_HEADS * 2 * DA_QK_DIM
LAMBDA_INIT = 0.8 - 0.6 * math.exp(-0.3 * 0)
EPS = 1e-6
NEG_INF = -1e30
LOG2_E = math.log2(math.e)

LANES = 128
SUBLANES = 8
BF16_ROWS = 16
AB_ROWS = BF16_ROWS
VMEM_LIMIT = 48 * 1024 * 1024

TM_IN = 512
TB_DN = 512
TQ_DA = 1024
DIAG_BAND = 256
TK_DA = 1024
HEADS_DA = 2
TM_OUT = 512

NT_DIMS = (((1,), (1,)), ((), ()))
TN_DIMS = (((0,), (0,)), ((), ()))


def _sigmoid(v):
    return 0.5 + 0.5 * jnp.tanh(0.5 * v)


def _silu(v):
    h = 0.5 * v
    return h + h * jnp.tanh(h)


def _softplus(v):
    return jnp.maximum(v, 0.0) + jnp.log1p(jnp.exp(-jnp.abs(v)))


def _resident(shape):
    zeros = (0,) * len(shape)
    return pl.BlockSpec(shape, lambda *_: zeros, pipeline_mode=pl.Buffered(1))


def _inproj_kernel(x_ref, nw_ref, wqkv_ref, convw_ref, wza_ref, wab_ref, wabt_ref,
                   alog_l_ref, dtb_l_ref, alog_s_ref, dtb_s_ref, qnw_ref, knw_ref,
                   wqb_ref, wkb_ref, wvb_ref, wzb_ref,
                   qa_ref, ka_ref, va_ref, za_ref, gcol_ref, grow_ref,
                   qb_ref, kb_ref, vb_ref, zb_ref,
                   conv_buf, *, tiles_per_seq):
    tm = x_ref.shape[0]
    half = tm // 2
    i = pl.program_id(0)
    head0 = SUBLANES - (CONV_WIDTH - 1)
    slab = 512

    @pl.when(i % tiles_per_seq == 0)
    def _():
        conv_buf[0:SUBLANES, :] = jnp.zeros((SUBLANES, 3 * DN_WIDTH), F32)

    def normed(r0):
        x = x_ref[r0:r0 + half, :]
        ms = jnp.mean(x * x, axis=-1, keepdims=True)
        return (x * lax.rsqrt(ms + EPS) * nw_ref[...]).astype(BF16)

    def tasks(h, r0):
        rows = slice(r0, r0 + half)

        def conv_qkv(s, out_ref):
            cs = slice(s * DN_WIDTH, (s + 1) * DN_WIDTH)

            res = []

            def matmul():
                res.append(jnp.dot(h, wqkv_ref[:, cs], preferred_element_type=F32))
                conv_buf[SUBLANES + r0:SUBLANES + r0 + half, cs] = res[0]

            def epilogue():
                acc = convw_ref[CONV_WIDTH - 1:CONV_WIDTH, cs] * res[0]
                for j in range(CONV_WIDTH - 1):
                    acc = acc + (convw_ref[j:j + 1, cs]
                                 * conv_buf[head0 + j + r0:head0 + j + r0 + half, cs])
                y = _silu(acc)
                if s == 2:
                    out_ref[rows, :] = y.astype(BF16)
                    return
                scale = DN_HEAD_DIM ** -0.5 if s == 0 else 1.0
                for hd in range(DN_HEADS):
                    hs = slice(hd * DN_HEAD_DIM, (hd + 1) * DN_HEAD_DIM)
                    blk = y[:, hs]
                    ss = jnp.sum(blk * blk, axis=-1, keepdims=True)
                    out_ref[rows, hs] = (blk * (lax.rsqrt(ss + EPS) * scale)).astype(BF16)
            return matmul, [epilogue]

        def plain(w_ref, c0, out_ref, act):
            res = []

            def matmul():
                res.append(jnp.dot(h, w_ref[:, c0:c0 + slab], preferred_element_type=F32))

            def chunk(n):
                def run():
                    cs = slice(n * LANES, (n + 1) * LANES)
                    out_ref[rows, c0 + n * LANES:c0 + (n + 1) * LANES] = (
                        act(res[0][:, cs]).astype(BF16))
                return run
            return matmul, [chunk(n) for n in range(slab // LANES)]

        def qk_norm(w_ref, nrm_ref, out_ref, scale):
            res = []

            def matmul():
                res.append(jnp.dot(h, w_ref[...], preferred_element_type=F32))

            def chunk(hd):
                def run():
                    lo = lax.broadcasted_iota(jnp.int32, (half, LANES), 1) < DA_QK_DIM
                    blk = res[0][:, hd * LANES:(hd + 1) * LANES]
                    sq = blk * blk
                    s_lo = jnp.sum(jnp.where(lo, sq, 0.0), axis=-1, keepdims=True)
                    s_hi = jnp.sum(jnp.where(lo, 0.0, sq), axis=-1, keepdims=True)
                    msq = jnp.where(lo, s_lo, s_hi) * (1.0 / DA_QK_DIM)
                    out_ref[rows, hd * LANES:(hd + 1) * LANES] = (
                        blk * lax.rsqrt(msq + EPS) * nrm_ref[...] * scale).astype(BF16)
                return run
            return matmul, [chunk(hd) for hd in range(DA_HEADS)]

        def decay_beta():
            def matmul():
                ab = jnp.dot(h, wab_ref[...], preferred_element_type=F32)
                lane = lax.broadcasted_iota(jnp.int32, ab.shape, 1)
                g_l = -jnp.exp(alog_l_ref[...]) * _softplus(ab + dtb_l_ref[...])
                gcol_ref[rows, :] = jnp.where(lane < DN_HEADS, g_l, _sigmoid(ab))
                abt = lax.dot_general(wabt_ref[...], h, NT_DIMS,
                                      preferred_element_type=F32)
                row = lax.broadcasted_iota(jnp.int32, abt.shape, 0)
                g_s = -jnp.exp(alog_s_ref[...]) * _softplus(abt + dtb_s_ref[...])
                g_beta_s = jnp.where(row < DN_HEADS, g_s, _sigmoid(abt))
                for n in range(half // CHUNK):
                    grow_ref[r0 // CHUNK + n] = g_beta_s[:, n * CHUNK:(n + 1) * CHUNK]
            return matmul, []

        ident = lambda v: v
        return [conv_qkv(0, qa_ref), plain(wvb_ref, 0, vb_ref, ident),
                conv_qkv(1, ka_ref), plain(wza_ref, 0, za_ref, _silu),
                conv_qkv(2, va_ref), plain(wzb_ref, 0, zb_ref, _silu),
                qk_norm(wqb_ref, qnw_ref, qb_ref, DA_QK_DIM ** -0.5 * LOG2_E),
                qk_norm(wkb_ref, knw_ref, kb_ref, 1.0), decay_beta()]

    def run_tasks(task_list, extra=None):
        for n, (matmul, chunks) in enumerate(task_list):
            matmul()
            for fn in chunks:
                fn()
            if extra is not None and n == 6:
                extra()

    h_second = []
    run_tasks(tasks(normed(0), 0), extra=lambda: h_second.append(normed(half)))
    run_tasks(tasks(h_second[0], half))
    conv_buf[0:SUBLANES, :] = conv_buf[tm:tm + SUBLANES, :]


def _in_projection(x2, norm_w, w_in, conv_w, a_log, dt_bias, q_norm_w, k_norm_w, seq_len):
    m = x2.shape[0]
    tm = TM_IN
    assert m % tm == 0 and seq_len % tm == 0
    o = 0
    cols = {}
    for name, width in (("qkv", 3 * DN_WIDTH), ("za", DN_WIDTH), ("a", DN_HEADS), ("b", DN_HEADS),
                        ("qb", DA_QK_WIDTH), ("kb", DA_QK_WIDTH), ("vb", DA_WIDTH), ("zb", DA_WIDTH),
                        ("ga", D_MODEL), ("gb", D_MODEL)):
        cols[name] = w_in[:, o:o + width]
        o += width
    assert o == w_in.shape[1]
    wb = {k: v.astype(BF16) for k, v in cols.items()}
    w_ab = jnp.concatenate([cols["a"], cols["b"]], axis=1)
    w_ab_l = jnp.pad(w_ab, ((0, 0), (0, LANES - 2 * DN_HEADS))).astype(BF16)
    w_ab_s = jnp.pad(w_ab.T, ((0, AB_ROWS - 2 * DN_HEADS), (0, 0))).astype(BF16)
    alog_l = jnp.pad(a_log.astype(F32), (0, LANES - DN_HEADS)).reshape(1, LANES)
    dtb_l = jnp.pad(dt_bias.astype(F32), (0, LANES - DN_HEADS)).reshape(1, LANES)
    alog_s = jnp.pad(a_log.astype(F32), (0, AB_ROWS - DN_HEADS)).reshape(AB_ROWS, 1)
    dtb_s = jnp.pad(dt_bias.astype(F32), (0, AB_ROWS - DN_HEADS)).reshape(AB_ROWS, 1)
    qnw = jnp.tile(q_norm_w.astype(F32), 2).reshape(1, LANES)
    knw = jnp.tile(k_norm_w.astype(F32), 2).reshape(1, LANES)

    row_blk = lambda w: pl.BlockSpec((tm, w), lambda i: (i, 0))
    in_specs = [
        row_blk(D_MODEL), _resident((1, D_MODEL)),
        _resident((D_MODEL, 3 * DN_WIDTH)), _resident((CONV_WIDTH, 3 * DN_WIDTH)),
        _resident((D_MODEL, DN_WIDTH)), _resident((D_MODEL, LANES)), _resident((AB_ROWS, D_MODEL)),
        _resident((1, LANES)), _resident((1, LANES)), _resident((AB_ROWS, 1)), _resident((AB_ROWS, 1)),
        _resident((1, LANES)), _resident((1, LANES)),
        _resident((D_MODEL, DA_QK_WIDTH)), _resident((D_MODEL, DA_QK_WIDTH)),
        _resident((D_MODEL, DA_WIDTH)), _resident((D_MODEL, DA_WIDTH)),
    ]
    out_shapes = [jax.ShapeDtypeStruct((m, DN_WIDTH), BF16)] * 4 + [
        jax.ShapeDtypeStruct((m, LANES), F32),
        jax.ShapeDtypeStruct((m // CHUNK, AB_ROWS, CHUNK), F32)] + [
        jax.ShapeDtypeStruct((m, DA_WIDTH), BF16)] * 4
    out_specs = [row_blk(DN_WIDTH)] * 4 + [
        row_blk(LANES), pl.BlockSpec((tm // CHUNK, AB_ROWS, CHUNK), lambda i: (i, 0, 0))] + [
        row_blk(DA_WIDTH)] * 4
    outs = pl.pallas_call(
        functools.partial(_inproj_kernel, tiles_per_seq=seq_len // tm),
        grid=(m // tm,),
        in_specs=in_specs,
        out_specs=out_specs,
        out_shape=out_shapes,
        scratch_shapes=[pltpu.VMEM((tm + SUBLANES, 3 * DN_WIDTH), F32)],
        compiler_params=pltpu.CompilerParams(
            dimension_semantics=("arbitrary",), vmem_limit_bytes=VMEM_LIMIT),
        name="in_projection",
    )(x2, norm_w.astype(F32).reshape(1, D_MODEL), wb["qkv"], conv_w.astype(F32), wb["za"],
      w_ab_l, w_ab_s, alog_l, dtb_l, alog_s, dtb_s, qnw, knw,
      wb["qb"], wb["kb"], wb["vb"], wb["zb"])
    return outs, (wb["ga"], wb["gb"])


def _bmm(a, b):
    return jnp.einsum('nij,njk->nik', a, b, preferred_element_type=F32)


def _bmm_nt(a, b):
    return jnp.einsum('nid,njd->nij', a, b, preferred_element_type=F32)


def _split_bf16(x, parts):
    out = []
    for _ in range(parts - 1):
        hi = x.astype(BF16)
        out.append(hi)
        x = x - hi.astype(F32)
    out.append(x.astype(BF16))
    return out


def _deltanet_kernel(q_ref, k_ref, v_ref, z_ref, gcol_ref, grow_ref, nw_ref, o_ref,
                     state_ref, oraw_ref, lhs_ref, attn_ref, u_ref, kdec_ref, sdec_ref,
                     *, tiles, tiles_per_seq):
    tb = q_ref.shape[0]
    c = CHUNK
    nc = tb // c
    nh = DN_HEADS
    t = pl.program_id(0)
    keep_state = (t - 1) % tiles_per_seq != 0

    ri = lax.broadcasted_iota(jnp.int32, (c, c), 0)
    ci = lax.broadcasted_iota(jnp.int32, (c, c), 1)
    lower = ri >= ci
    strict = ri > ci
    eye = (ri == ci).astype(F32)
    tri = lower.astype(BF16)
    tri_t = (ri <= ci).astype(BF16)

    def heads_major(ref):
        return jnp.concatenate(
            [ref[:, hd * DN_HEAD_DIM:(hd + 1) * DN_HEAD_DIM].reshape(nc, c, DN_HEAD_DIM)
             for hd in range(nh)], axis=0)

    def phase1_stages():
        e = {}

        def decays():
            gb_col = gcol_ref[...].reshape(nc, c, LANES)
            tri_b = jnp.broadcast_to(tri, (nc, c, c))
            gc_col = sum(_bmm(tri_b, part) for part in _split_bf16(gb_col, 3))
            gb_row = grow_ref[...]
            gc_row = sum(jnp.dot(part, tri_t, preferred_element_type=F32)
                         for part in _split_bf16(gb_row.reshape(nc * AB_ROWS, c), 3)
                         ).reshape(nc, AB_ROWS, c)

            def col_form(src, lane):
                return jnp.concatenate(
                    [jnp.broadcast_to(src[:, :, lane + hd:lane + hd + 1], (nc, c, LANES))
                     for hd in range(nh)], axis=0)

            def row_form(src, row):
                return jnp.concatenate(
                    [src[:, row + hd:row + hd + 1, :] for hd in range(nh)], axis=0)

            e["g_c"] = col_form(gc_col, 0)
            e["beta_c"] = col_form(gb_col, nh)
            e["g_r"] = row_form(gc_row, 0)
            e["beta_r"] = row_form(gb_row, nh)
            e["g_last"] = e["g_c"][:, c - 1:c, :]
            diff = e["g_c"][:, :, :c] - e["g_r"]
            e["decay"] = jnp.where(lower, jnp.exp(jnp.where(lower, diff, 0.0)), 0.0)

        def gram_kk():
            e["q"], e["k"], e["v"] = heads_major(q_ref), heads_major(k_ref), heads_major(v_ref)
            l_strict = jnp.where(
                strict, _bmm_nt(e["k"], e["k"]) * e["decay"] * e["beta_c"][:, :, :c], 0.0)
            e["inv"] = eye - l_strict
            e["power"] = l_strict.astype(BF16)

        def gram_qk():
            e["attn"] = jnp.where(lower, _bmm_nt(e["q"], e["k"]) * e["decay"], 0.0).astype(BF16)

        def next_power():
            e["power"] = _bmm(e["power"], e["power"]).astype(BF16)

        def apply_power():
            e["inv"] = e["inv"] + _bmm(e["inv"].astype(BF16), e["power"])

        def solve_u():
            e["t_beta"] = e["inv"] * e["beta_r"]
            e["u"] = _bmm(e["t_beta"].astype(BF16), e["v"])

        def solve_w():
            e["w"] = _bmm((e["t_beta"] * jnp.exp(e["g_r"])).astype(BF16), e["k"]).astype(BF16)

        def commit():
            q_decay = (e["q"].astype(F32) * jnp.exp(e["g_c"])).astype(BF16)
            lhs_ref[...] = jnp.concatenate([e["w"], q_decay], axis=1)
            attn_ref[...] = e["attn"]
            u_ref[...] = e["u"]
            kdec_ref[...] = (e["k"].astype(F32) * jnp.exp(e["g_last"] - e["g_c"])).astype(BF16)
            sdec_ref[...] = jnp.exp(e["g_last"])

        levels = [next_power, apply_power] * (c.bit_length() - 2)
        return [decays, gram_kk, gram_qk] + levels + [solve_u, solve_w], commit

    def phase2_stages():
        def chunk_steps(n):
            idx = [hd * nc + n for hd in range(nh)]
            e = {}

            def through_state():
                e["s_old"] = [jnp.where(keep_state, state_ref[hd], 0.0) if n == 0
                              else state_ref[hd] for hd in range(nh)]
                e["ws_qs"] = [jnp.dot(lhs_ref[b], e["s_old"][hd].astype(BF16),
                                      preferred_element_type=F32) for hd, b in enumerate(idx)]

            def update():
                v_new = [(u_ref[b] - e["ws_qs"][hd][:c]).astype(BF16) for hd, b in enumerate(idx)]
                for hd, b in enumerate(idx):
                    oraw_ref[n * c:(n + 1) * c, hd * DN_HEAD_DIM:(hd + 1) * DN_HEAD_DIM] = (
                        e["ws_qs"][hd][c:]
                        + jnp.dot(attn_ref[b], v_new[hd], preferred_element_type=F32))
                for hd, b in enumerate(idx):
                    state_ref[hd] = e["s_old"][hd] * sdec_ref[b] + lax.dot_general(
                        kdec_ref[b], v_new[hd], TN_DIMS, preferred_element_type=F32)
            return [through_state, update]

        def gated_norm():
            for hd in range(nh):
                hs = slice(hd * DN_HEAD_DIM, (hd + 1) * DN_HEAD_DIM)
                o = oraw_ref[:, hs]
                ms = jnp.mean(o * o, axis=-1, keepdims=True)
                o = o * lax.rsqrt(ms + EPS) * nw_ref[...]
                o_ref[:, hs] = (o * z_ref[:, hs].astype(F32)).astype(BF16)

        return [fn for n in range(nc) for fn in chunk_steps(n)] + [gated_norm]

    @pl.when(t == 0)
    def _():
        state_ref[...] = jnp.zeros(state_ref.shape, F32)
        stages, commit = phase1_stages()
        for fn in stages:
            fn()
        commit()

    @pl.when((t > 0) & (t < tiles))
    def _():
        stages, commit = phase1_stages()
        others = phase2_stages()
        done = 0
        for n, fn in enumerate(stages):
            fn()
            upto = -(-(n + 1) * len(others) // len(stages))
            for other in others[done:upto]:
                other()
            done = upto
        commit()

    @pl.when(t == tiles)
    def _():
        for fn in phase2_stages():
            fn()


def _deltanet(qa, ka, va, za, gcol, grow, dn_norm_w, batch, seq_len):
    tb = TB_DN
    assert seq_len % tb == 0 and tb % CHUNK == 0
    nt = seq_len // tb
    tiles = batch * nt
    nc = tb // CHUNK
    nb = DN_HEADS * nc
    cur = lambda w: pl.BlockSpec((tb, w), lambda t: (jnp.minimum(t, tiles - 1), 0))
    prev = lambda w: pl.BlockSpec((tb, w), lambda t: (jnp.maximum(t - 1, 0), 0))
    return pl.pallas_call(
        functools.partial(_deltanet_kernel, tiles=tiles, tiles_per_seq=nt),
        grid=(tiles + 1,),
        in_specs=[cur(DN_WIDTH)] * 3 + [
            prev(DN_WIDTH), cur(LANES),
            pl.BlockSpec((nc, AB_ROWS, CHUNK), lambda t: (jnp.minimum(t, tiles - 1), 0, 0)),
            _resident((1, DN_HEAD_DIM))],
        out_specs=prev(DN_WIDTH),
        out_shape=jax.ShapeDtypeStruct((batch * seq_len, DN_WIDTH), BF16),
        scratch_shapes=[pltpu.VMEM((DN_HEADS, DN_HEAD_DIM, DN_HEAD_DIM), F32),
                        pltpu.VMEM((tb, DN_WIDTH), F32),
                        pltpu.VMEM((nb, 2 * CHUNK, DN_HEAD_DIM), BF16),
                        pltpu.VMEM((nb, CHUNK, CHUNK), BF16),
                        pltpu.VMEM((nb, CHUNK, DN_HEAD_DIM), F32),
                        pltpu.VMEM((nb, CHUNK, DN_HEAD_DIM), BF16),
                        pltpu.VMEM((nb, 1, DN_HEAD_DIM), F32)],
        compiler_params=pltpu.CompilerParams(
            dimension_semantics=("arbitrary",), vmem_limit_bytes=VMEM_LIMIT),
        name="gated_deltanet",
    )(qa, ka, va, za, gcol, grow, dn_norm_w.astype(F32).reshape(1, DN_HEAD_DIM))


def _diffattn_kernel(lq1_ref, lk1_ref, lq2_ref, lk2_ref, q_ref, k_ref, v_ref, z_ref, nw_ref,
                     o_ref, m_ref, l_ref, acc_ref):
    tq = q_ref.shape[0]
    tk = TK_DA
    qi = pl.program_id(2)
    chains = [(hd, comp) for hd in range(HEADS_DA) for comp in range(2)]

    lane = lax.broadcasted_iota(jnp.int32, (1, LANES), 1)
    first_half = lane < DA_QK_DIM
    q_chain = []
    for hd in range(HEADS_DA):
        q = q_ref[:, hd * LANES:(hd + 1) * LANES]
        zero = jnp.zeros_like(q)
        q_chain += [jnp.where(first_half, q, zero), jnp.where(first_half, zero, q)]

    def attend(start, row0=0, nrows=None, width=None, masked=False, first=False):
        nrows = tq if nrows is None else nrows
        width = tk if width is None else width
        rows = slice(row0, row0 + nrows)
        kblk = [k_ref[pl.ds(start, width), hd * LANES:(hd + 1) * LANES] for hd in range(HEADS_DA)]
        ones = jnp.ones((width, LANES), BF16)
        vblk = [jnp.concatenate([v_ref[pl.ds(start, width), hd * LANES:(hd + 1) * LANES], ones],
                                axis=1) for hd in range(HEADS_DA)]
        scores = [lax.dot_general(q_chain[c][rows], kblk[hd], NT_DIMS, preferred_element_type=F32)
                  for c, (hd, _) in enumerate(chains)]
        if masked:
            rq = lax.broadcasted_iota(jnp.int32, (nrows, width), 0) + row0
            ck = lax.broadcasted_iota(jnp.int32, (nrows, width), 1)
            scores = [jnp.where(ck <= rq, s, NEG_INF) for s in scores]
        probs = []
        for c, s in enumerate(scores):
            m_new = jnp.max(s, axis=-1, keepdims=True)
            if first:
                m_new = jnp.broadcast_to(m_new, (nrows, LANES))
            else:
                m_prev = m_ref[c, rows, :]
                m_new = jnp.maximum(m_prev, m_new)
                alpha = jnp.exp2(m_prev - m_new)
                l_ref[c, rows, :] = alpha * l_ref[c, rows, :]
                acc_ref[c, rows, :] = alpha * acc_ref[c, rows, :]
            probs.append(jnp.exp2(
                (s - jnp.concatenate([m_new] * (width // LANES), axis=1)).astype(BF16)))
            m_ref[c, rows, :] = m_new
        for c, (hd, _) in enumerate(chains):
            pv = jnp.dot(probs[c], vblk[hd], preferred_element_type=F32)
            if first:
                acc_ref[c, rows, :] = pv[:, :DA_V_DIM]
                l_ref[c, rows, :] = pv[:, DA_V_DIM:]
            else:
                acc_ref[c, rows, :] += pv[:, :DA_V_DIM]
                l_ref[c, rows, :] += pv[:, DA_V_DIM:]

    diag0 = pl.multiple_of(qi * tq, tq)
    for r0 in range(0, tq, DIAG_BAND):
        attend(diag0, row0=r0, nrows=DIAG_BAND, width=r0 + DIAG_BAND, masked=True, first=True)

    def body(j, carry):
        attend(pl.multiple_of(j * tk, tk))
        return carry

    lax.fori_loop(0, qi * (tq // tk), body, 0)

    lam = (jnp.exp(jnp.sum(lq1_ref[...] * lk1_ref[...], axis=-1, keepdims=True))
           - jnp.exp(jnp.sum(lq2_ref[...] * lk2_ref[...], axis=-1, keepdims=True))
           + LAMBDA_INIT)
    for hd in range(HEADS_DA):
        hs = slice(hd * LANES, (hd + 1) * LANES)
        o = acc_ref[2 * hd] / l_ref[2 * hd] - lam * (acc_ref[2 * hd + 1] / l_ref[2 * hd + 1])
        ms = jnp.mean(o * o, axis=-1, keepdims=True)
        o = o * lax.rsqrt(ms + EPS) * nw_ref[...] * (1.0 - LAMBDA_INIT)
        o_ref[:, hs] = (o * z_ref[:, hs].astype(F32)).astype(BF16)


def _diff_attention(qb, kb, vb, zb, lambda_q1, lambda_k1, lambda_q2, lambda_k2, da_norm_w,
                    batch, seq_len):
    tq = TQ_DA
    assert tq % TK_DA == 0 and tq % DIAG_BAND == 0 and seq_len % tq == 0 and DA_HEADS % HEADS_DA == 0
    nq = seq_len // tq
    width = HEADS_DA * LANES
    lam_vec = lambda v: v.astype(F32).reshape(1, DA_QK_DIM)
    q_blk = pl.BlockSpec((tq, width), lambda b, h, i: (b * nq + i, h))
    kv_blk = pl.BlockSpec((seq_len, width), lambda b, h, i: (b, h))
    stat = pltpu.VMEM((2 * HEADS_DA, tq, LANES), F32)
    return pl.pallas_call(
        _diffattn_kernel,
        grid=(batch, DA_HEADS // HEADS_DA, nq),
        in_specs=[_resident((1, DA_QK_DIM))] * 4 + [q_blk, kv_blk, kv_blk, q_blk,
                                                    _resident((1, DA_V_DIM))],
        out_specs=q_blk,
        out_shape=jax.ShapeDtypeStruct((batch * seq_len, DA_WIDTH), BF16),
        scratch_shapes=[stat, stat, pltpu.VMEM((2 * HEADS_DA, tq, DA_V_DIM), F32)],
        compiler_params=pltpu.CompilerParams(
            dimension_semantics=("arbitrary", "arbitrary", "arbitrary"),
            vmem_limit_bytes=VMEM_LIMIT),
        name="diff_attention",
    )(lam_vec(lambda_q1), lam_vec(lambda_k1), lam_vec(lambda_q2), lam_vec(lambda_k2),
      qb, kb, vb, zb, da_norm_w.astype(F32).reshape(1, DA_V_DIM))


def _output_kernel(oa_ref, ob_ref, x_ref, nw_ref, wga_ref, wgb_ref, woa32_ref, wob32_ref, wo32_ref,
                   out_ref, woa_ref, wob_ref, wo_ref):
    @pl.when(pl.program_id(0) == 0)
    def _():
        woa_ref[...] = woa32_ref[...].astype(BF16)
        wob_ref[...] = wob32_ref[...].astype(BF16)
        wo_ref[...] = wo32_ref[...].astype(BF16)

    x = x_ref[...]
    ms = jnp.mean(x * x, axis=-1, keepdims=True)
    h = (x * lax.rsqrt(ms + EPS) * nw_ref[...]).astype(BF16)
    gate_a = _sigmoid(jnp.dot(h, wga_ref[...], preferred_element_type=F32))
    gate_b = _sigmoid(jnp.dot(h, wgb_ref[...], preferred_element_type=F32))
    y_a = jnp.dot(oa_ref[...], woa_ref[...], preferred_element_type=F32)
    y_b = jnp.dot(ob_ref[...], wob_ref[...], preferred_element_type=F32)
    y = gate_a * y_a + gate_b * y_b
    out_ref[...] = x + jnp.dot(y.astype(BF16), wo_ref[...], preferred_element_type=F32)


def _output_projection(oa, ob, x2, norm_w, w_gates, w_out_a, w_out_b, w_out):
    m = x2.shape[0]
    tm = TM_OUT
    assert m % tm == 0
    row_blk = lambda w: pl.BlockSpec((tm, w), lambda i: (i, 0))
    return pl.pallas_call(
        _output_kernel,
        grid=(m // tm,),
        in_specs=[row_blk(DN_WIDTH), row_blk(DA_WIDTH), row_blk(D_MODEL), _resident((1, D_MODEL)),
                  _resident((D_MODEL, D_MODEL)), _resident((D_MODEL, D_MODEL)),
                  _resident((DN_WIDTH, D_MODEL)), _resident((DA_WIDTH, D_MODEL)),
                  _resident((D_MODEL, D_MODEL))],
        out_specs=row_blk(D_MODEL),
        out_shape=jax.ShapeDtypeStruct((m, D_MODEL), F32),
        scratch_shapes=[pltpu.VMEM((DN_WIDTH, D_MODEL), BF16), pltpu.VMEM((DA_WIDTH, D_MODEL), BF16),
                        pltpu.VMEM((D_MODEL, D_MODEL), BF16)],
        compiler_params=pltpu.CompilerParams(
            dimension_semantics=("arbitrary",), vmem_limit_bytes=VMEM_LIMIT),
        name="output_projection",
    )(oa, ob, x2, norm_w.astype(F32).reshape(1, D_MODEL), *w_gates,
      w_out_a.astype(F32), w_out_b.astype(F32), w_out.astype(F32))


def kernel(x, norm_w, w_in, conv_w, a_log, dt_bias, dn_norm_w, q_norm_w, k_norm_w,
           lambda_q1, lambda_k1, lambda_q2, lambda_k2, da_norm_w, w_out_a, w_out_b, w_out):
    batch, seq_len, d_model = x.shape
    assert d_model == D_MODEL
    x2 = x.reshape(batch * seq_len, d_model)
    (qa, ka, va, za, gcol, grow, qb, kb, vb, zb), w_gates = _in_projection(
        x2, norm_w, w_in, conv_w, a_log, dt_bias, q_norm_w, k_norm_w, seq_len)
    oa = _deltanet(qa, ka, va, za, gcol, grow, dn_norm_w, batch, seq_len)
    ob = _diff_attention(qb, kb, vb, zb, lambda_q1, lambda_k1, lambda_q2, lambda_k2, da_norm_w,
                         batch, seq_len)
    out = _output_projection(oa, ob, x2, norm_w, w_gates, w_out_a, w_out_b, w_out)
    return out.reshape(batch, seq_len, d_model)
```

```python
import functools
import math

import jax
import jax.numpy as jnp
from jax import lax
from jax.experimental import pallas as pl
from jax.experimental.pallas import tpu as pltpu

F32 = jnp.float32
BF16 = jnp.bfloat16

D_MODEL = 1024
DN_HEADS = 4
DN_HEAD_DIM = 128
DN_WIDTH = DN_HEADS * DN_HEAD_DIM
CONV_WIDTH = 4
CHUNK = 64
DA_HEADS = 4
DA_QK_DIM = 64
DA_V_DIM = 2 * DA_QK_DIM
DA_WIDTH = DA_HEADS * DA_V_DIM
DA_QK_WIDTH = DA_HEADS * 2 * DA_QK_DIM
LAMBDA_INIT = 0.8 - 0.6 * math.exp(-0.3 * 0)
EPS = 1e-6
NEG_INF = -1e30
LOG2_E = math.log2(math.e)

LANES = 128
SUBLANES = 8
BF16_ROWS = 16
AB_ROWS = BF16_ROWS
VMEM_LIMIT = 48 * 1024 * 1024

TM_IN = 512
TB_DN = 512
TQ_DA = 1024
DIAG_BAND = 256
TK_DA = 1024
HEADS_DA = 2
TM_OUT = 512

NT_DIMS = (((1,), (1,)), ((), ()))
TN_DIMS = (((0,), (0,)), ((), ()))


def _sigmoid(v):
    return 0.5 + 0.5 * jnp.tanh(0.5 * v)


def _silu(v):
    h = 0.5 * v
    return h + h * jnp.tanh(h)


def _softplus(v):
    return jnp.maximum(v, 0.0) + jnp.log1p(jnp.exp(-jnp.abs(v)))


def _resident(shape):
    zeros = (0,) * len(shape)
    return pl.BlockSpec(shape, lambda *_: zeros, pipeline_mode=pl.Buffered(1))


def _inproj_kernel(x_ref, nw_ref, wqkv_ref, wza_ref, wab_ref, wabt_ref,
                   alog_l_ref, dtb_l_ref, alog_s_ref, dtb_s_ref, qnw_ref, knw_ref,
                   wqb_ref, wkb_ref, wvb_ref, wzb_ref,
                   pre_ref, za_ref, gcol_ref, grow_ref,
                   qb_ref, kb_ref, vb_ref, zb_ref):
    tm = x_ref.shape[0]
    half = tm // 2
    slab = 512

    def normed(r0):
        x = x_ref[r0:r0 + half, :]
        ms = jnp.mean(x * x, axis=-1, keepdims=True)
        return (x * lax.rsqrt(ms + EPS) * nw_ref[...]).astype(BF16)

    def tasks(h, r0):
        rows = slice(r0, r0 + half)

        def pre_conv(s):
            cs = slice(s * DN_WIDTH, (s + 1) * DN_WIDTH)

            def matmul():
                pre_ref[rows, cs] = jnp.dot(h, wqkv_ref[:, cs], preferred_element_type=F32)
            return matmul, []

        def plain(w_ref, c0, out_ref, act):
            res = []

            def matmul():
                res.append(jnp.dot(h, w_ref[:, c0:c0 + slab], preferred_element_type=F32))

            def chunk(n):
                def run():
                    cs = slice(n * LANES, (n + 1) * LANES)
                    out_ref[rows, c0 + n * LANES:c0 + (n + 1) * LANES] = (
                        act(res[0][:, cs]).astype(BF16))
                return run
            return matmul, [chunk(n) for n in range(slab // LANES)]

        def qk_norm(w_ref, nrm_ref, out_ref, scale):
            res = []

            def matmul():
                res.append(jnp.dot(h, w_ref[...], preferred_element_type=F32))

            def chunk(hd):
                def run():
                    lo = lax.broadcasted_iota(jnp.int32, (half, LANES), 1) < DA_QK_DIM
                    blk = res[0][:, hd * LANES:(hd + 1) * LANES]
                    sq = blk * blk
                    s_lo = jnp.sum(jnp.where(lo, sq, 0.0), axis=-1, keepdims=True)
                    s_hi = jnp.sum(jnp.where(lo, 0.0, sq), axis=-1, keepdims=True)
                    msq = jnp.where(lo, s_lo, s_hi) * (1.0 / DA_QK_DIM)
                    out_ref[rows, hd * LANES:(hd + 1) * LANES] = (
                        blk * lax.rsqrt(msq + EPS) * nrm_ref[...] * scale).astype(BF16)
                return run
            return matmul, [chunk(hd) for hd in range(DA_HEADS)]

        def decay_beta():
            def matmul():
                ab = jnp.dot(h, wab_ref[...], preferred_element_type=F32)
                lane = lax.broadcasted_iota(jnp.int32, ab.shape, 1)
                g_l = -jnp.exp(alog_l_ref[...]) * _softplus(ab + dtb_l_ref[...])
                gcol_ref[rows, :] = jnp.where(lane < DN_HEADS, g_l, _sigmoid(ab))
                abt = lax.dot_general(wabt_ref[...], h, NT_DIMS,
                                      preferred_element_type=F32)
                row = lax.broadcasted_iota(jnp.int32, abt.shape, 0)
                g_s = -jnp.exp(alog_s_ref[...]) * _softplus(abt + dtb_s_ref[...])
                g_beta_s = jnp.where(row < DN_HEADS, g_s, _sigmoid(abt))
                for n in range(half // CHUNK):
                    grow_ref[r0 // CHUNK + n] = g_beta_s[:, n * CHUNK:(n + 1) * CHUNK]
            return matmul, []

        ident = lambda v: v
        return [pre_conv(0), plain(wvb_ref, 0, vb_ref, ident),
                pre_conv(1), plain(wza_ref, 0, za_ref, _silu),
                pre_conv(2), plain(wzb_ref, 0, zb_ref, _silu),
                qk_norm(wqb_ref, qnw_ref, qb_ref, DA_QK_DIM ** -0.5 * LOG2_E),
                qk_norm(wkb_ref, knw_ref, kb_ref, 1.0), decay_beta()]

    def run_tasks(task_list, extra=None):
        for n, (matmul, chunks) in enumerate(task_list):
            matmul()
            for fn in chunks:
                fn()
            if extra is not None and n == 6:
                extra()

    h_second = []
    run_tasks(tasks(normed(0), 0), extra=lambda: h_second.append(normed(half)))
    run_tasks(tasks(h_second[0], half))


def _in_projection(x2, norm_w, w_in, a_log, dt_bias, q_norm_w, k_norm_w):
    m = x2.shape[0]
    tm = TM_IN
    assert m % tm == 0
    o = 0
    cols = {}
    for name, width in (("qkv", 3 * DN_WIDTH), ("za", DN_WIDTH), ("a", DN_HEADS), ("b", DN_HEADS),
                        ("qb", DA_QK_WIDTH), ("kb", DA_QK_WIDTH), ("vb", DA_WIDTH), ("zb", DA_WIDTH),
                        ("ga", D_MODEL), ("gb", D_MODEL)):
        cols[name] = w_in[:, o:o + width]
        o += width
    assert o == w_in.shape[1]
    wb = {k: v.astype(BF16) for k, v in cols.items()}
    w_ab = jnp.concatenate([cols["a"], cols["b"]], axis=1)
    w_ab_l = jnp.pad(w_ab, ((0, 0), (0, LANES - 2 * DN_HEADS))).astype(BF16)
    w_ab_s = jnp.pad(w_ab.T, ((0, AB_ROWS - 2 * DN_HEADS), (0, 0))).astype(BF16)
    alog_l = jnp.pad(a_log.astype(F32), (0, LANES - DN_HEADS)).reshape(1, LANES)
    dtb_l = jnp.pad(dt_bias.astype(F32), (0, LANES - DN_HEADS)).reshape(1, LANES)
    alog_s = jnp.pad(a_log.astype(F32), (0, AB_ROWS - DN_HEADS)).reshape(AB_ROWS, 1)
    dtb_s = jnp.pad(dt_bias.astype(F32), (0, AB_ROWS - DN_HEADS)).reshape(AB_ROWS, 1)
    qnw = jnp.tile(q_norm_w.astype(F32), 2).reshape(1, LANES)
    knw = jnp.tile(k_norm_w.astype(F32), 2).reshape(1, LANES)

    row_blk = lambda w: pl.BlockSpec((tm, w), lambda i: (i, 0))
    in_specs = [
        row_blk(D_MODEL), _resident((1, D_MODEL)),
        _resident((D_MODEL, 3 * DN_WIDTH)),
        _resident((D_MODEL, DN_WIDTH)), _resident((D_MODEL, LANES)), _resident((AB_ROWS, D_MODEL)),
        _resident((1, LANES)), _resident((1, LANES)), _resident((AB_ROWS, 1)), _resident((AB_ROWS, 1)),
        _resident((1, LANES)), _resident((1, LANES)),
        _resident((D_MODEL, DA_QK_WIDTH)), _resident((D_MODEL, DA_QK_WIDTH)),
        _resident((D_MODEL, DA_WIDTH)), _resident((D_MODEL, DA_WIDTH)),
    ]
    out_shapes = [jax.ShapeDtypeStruct((m, 3 * DN_WIDTH), F32),
                  jax.ShapeDtypeStruct((m, DN_WIDTH), BF16),
                  jax.ShapeDtypeStruct((m, LANES), F32),
                  jax.ShapeDtypeStruct((m // CHUNK, AB_ROWS, CHUNK), F32)] + [
        jax.ShapeDtypeStruct((m, DA_WIDTH), BF16)] * 4
    out_specs = [row_blk(3 * DN_WIDTH), row_blk(DN_WIDTH),
                 row_blk(LANES), pl.BlockSpec((tm // CHUNK, AB_ROWS, CHUNK), lambda i: (i, 0, 0))] + [
        row_blk(DA_WIDTH)] * 4
    outs = pl.pallas_call(
        _inproj_kernel,
        grid=(m // tm,),
        in_specs=in_specs,
        out_specs=out_specs,
        out_shape=out_shapes,
        compiler_params=pltpu.CompilerParams(
            dimension_semantics=("arbitrary",), vmem_limit_bytes=VMEM_LIMIT),
        name="in_projection",
    )(x2, norm_w.astype(F32).reshape(1, D_MODEL), wb["qkv"], wb["za"],
      w_ab_l, w_ab_s, alog_l, dtb_l, alog_s, dtb_s, qnw, knw,
      wb["qb"], wb["kb"], wb["vb"], wb["zb"])
    return outs, (wb["ga"], wb["gb"])


def _bmm(a, b):
    return jnp.einsum('nij,njk->nik', a, b, preferred_element_type=F32)


def _bmm_nt(a, b):
    return jnp.einsum('nid,njd->nij', a, b, preferred_element_type=F32)


def _split_bf16(x, parts):
    out = []
    for _ in range(parts - 1):
        hi = x.astype(BF16)
        out.append(hi)
        x = x - hi.astype(F32)
    out.append(x.astype(BF16))
    return out


def _deltanet_kernel(pre_ref, convw_ref, z_ref, gcol_ref, grow_ref, nw_ref, o_ref,
                     state_ref, oraw_ref, lhs_ref, attn_ref, u_ref, kdec_ref, sdec_ref, conv_buf,
                     *, tiles, tiles_per_seq):
    tb = pre_ref.shape[0]
    c = CHUNK
    nc = tb // c
    nh = DN_HEADS
    t = pl.program_id(0)
    keep_state = (t - 1) % tiles_per_seq != 0

    ri = lax.broadcasted_iota(jnp.int32, (c, c), 0)
    ci = lax.broadcasted_iota(jnp.int32, (c, c), 1)
    lower = ri >= ci
    strict = ri > ci
    eye = (ri == ci).astype(F32)
    tri = lower.astype(BF16)
    tri_t = (ri <= ci).astype(BF16)

    head0 = SUBLANES - (CONV_WIDTH - 1)

    def fill_conv_buf():
        conv_buf[0:SUBLANES, :] = jnp.where(t % tiles_per_seq == 0, 0.0, conv_buf[0:SUBLANES, :])
        conv_buf[SUBLANES:SUBLANES + tb, :] = pre_ref[...]

    def conv_heads(s):
        scale = DN_HEAD_DIM ** -0.5 if s == 0 else 1.0
        blocks = []
        for hd in range(nh):
            cs = slice(s * DN_WIDTH + hd * DN_HEAD_DIM, s * DN_WIDTH + (hd + 1) * DN_HEAD_DIM)
            acc = convw_ref[CONV_WIDTH - 1:CONV_WIDTH, cs] * pre_ref[:, cs]
            for j in range(CONV_WIDTH - 1):
                acc = acc + convw_ref[j:j + 1, cs] * conv_buf[head0 + j:head0 + j + tb, cs]
            y = _silu(acc)
            if s < 2:
                ss = jnp.sum(y * y, axis=-1, keepdims=True)
                y = y * (lax.rsqrt(ss + EPS) * scale)
            blocks.append(y.astype(BF16).reshape(nc, c, DN_HEAD_DIM))
        return jnp.concatenate(blocks, axis=0)

    def phase1_stages():
        e = {}

        def decays():
            gb_col = gcol_ref[...].reshape(nc, c, LANES)
            tri_b = jnp.broadcast_to(tri, (nc, c, c))
            gc_col = sum(_bmm(tri_b, part) for part in _split_bf16(gb_col, 3))
            gb_row = grow_ref[...]
            gc_row = sum(jnp.dot(part, tri_t, preferred_element_type=F32)
                         for part in _split_bf16(gb_row.reshape(nc * AB_ROWS, c), 3)
                         ).reshape(nc, AB_ROWS, c)

            def col_form(src, lane):
                return jnp.concatenate(
                    [jnp.broadcast_to(src[:, :, lane + hd:lane + hd + 1], (nc, c, LANES))
                     for hd in range(nh)], axis=0)

            def row_form(src, row):
                return jnp.concatenate(
                    [src[:, row + hd:row + hd + 1, :] for hd in range(nh)], axis=0)

            e["g_c"] = col_form(gc_col, 0)
            e["beta_c"] = col_form(gb_col, nh)
            e["g_r"] = row_form(gc_row, 0)
            e["beta_r"] = row_form(gb_row, nh)
            e["g_last"] = e["g_c"][:, c - 1:c, :]
            diff = e["g_c"][:, :, :c] - e["g_r"]
            e["decay"] = jnp.where(lower, jnp.exp(jnp.where(lower, diff, 0.0)), 0.0)

        def conv_k():
            fill_conv_buf()
            e["k"] = conv_heads(1)

        def conv_q():
            e["q"] = conv_heads(0)

        def conv_v():
            e["v"] = conv_heads(2)

        def gram_kk():
            l_strict = jnp.where(
                strict, _bmm_nt(e["k"], e["k"]) * e["decay"] * e["beta_c"][:, :, :c], 0.0)
            e["inv"] = eye - l_strict
            e["power"] = l_strict.astype(BF16)

        def gram_qk():
            e["attn"] = jnp.where(lower, _bmm_nt(e["q"], e["k"]) * e["decay"], 0.0).astype(BF16)

        def next_power():
            e["power"] = _bmm(e["power"], e["power"]).astype(BF16)

        def apply_power():
            e["inv"] = e["inv"] + _bmm(e["inv"].astype(BF16), e["power"])

        def solve_u():
            e["t_beta"] = e["inv"] * e["beta_r"]
            e["u"] = _bmm(e["t_beta"].astype(BF16), e["v"])

        def solve_w():
            e["w"] = _bmm((e["t_beta"] * jnp.exp(e["g_r"])).astype(BF16), e["k"]).astype(BF16)

        def commit():
            q_decay = (e["q"].astype(F32) * jnp.exp(e["g_c"])).astype(BF16)
            lhs_ref[...] = jnp.concatenate([e["w"], q_decay], axis=1)
            attn_ref[...] = e["attn"]
            u_ref[...] = e["u"]
            kdec_ref[...] = (e["k"].astype(F32) * jnp.exp(e["g_last"] - e["g_c"])).astype(BF16)
            sdec_ref[...] = jnp.exp(e["g_last"])
            conv_buf[0:SUBLANES, :] = conv_buf[tb:tb + SUBLANES, :]

        levels = [next_power, apply_power] * (c.bit_length() - 2)
        return ([conv_k, decays, gram_kk, conv_q, gram_qk, conv_v] + levels
                + [solve_u, solve_w]), commit

    def phase2_stages():
        def chunk_steps(n):
            idx = [hd * nc + n for hd in range(nh)]
            e = {}

            def through_state():
                e["s_old"] = [jnp.where(keep_state, state_ref[hd], 0.0) if n == 0
                              else state_ref[hd] for hd in range(nh)]
                e["ws_qs"] = [jnp.dot(lhs_ref[b], e["s_old"][hd].astype(BF16),
                                      preferred_element_type=F32) for hd, b in enumerate(idx)]

            def update():
                v_new = [(u_ref[b] - e["ws_qs"][hd][:c]).astype(BF16) for hd, b in enumerate(idx)]
                for hd, b in enumerate(idx):
                    oraw_ref[n * c:(n + 1) * c, hd * DN_HEAD_DIM:(hd + 1) * DN_HEAD_DIM] = (
                        e["ws_qs"][hd][c:]
                        + jnp.dot(attn_ref[b], v_new[hd], preferred_element_type=F32))
                for hd, b in enumerate(idx):
                    state_ref[hd] = e["s_old"][hd] * sdec_ref[b] + lax.dot_general(
                        kdec_ref[b], v_new[hd], TN_DIMS, preferred_element_type=F32)
            return [through_state, update]

        def gated_norm():
            for hd in range(nh):
                hs = slice(hd * DN_HEAD_DIM, (hd + 1) * DN_HEAD_DIM)
                o = oraw_ref[:, hs]
                ms = jnp.mean(o * o, axis=-1, keepdims=True)
                o = o * lax.rsqrt(ms + EPS) * nw_ref[...]
                o_ref[:, hs] = (o * z_ref[:, hs].astype(F32)).astype(BF16)

        return [fn for n in range(nc) for fn in chunk_steps(n)] + [gated_norm]

    @pl.when(t == 0)
    def _():
        state_ref[...] = jnp.zeros(state_ref.shape, F32)
        conv_buf[0:SUBLANES, :] = jnp.zeros((SUBLANES, 3 * DN_WIDTH), F32)
        stages, commit = phase1_stages()
        for fn in stages:
            fn()
        commit()

    @pl.when((t > 0) & (t < tiles))
    def _():
        stages, commit = phase1_stages()
        others = phase2_stages()
        done = 0
        for n, fn in enumerate(stages):
            fn()
            upto = -(-(n + 1) * len(others) // len(stages))
            for other in others[done:upto]:
                other()
            done = upto
        commit()

    @pl.when(t == tiles)
    def _():
        for fn in phase2_stages():
            fn()


def _deltanet(pre, conv_w, za, gcol, grow, dn_norm_w, batch, seq_len):
    tb = TB_DN
    assert seq_len % tb == 0 and tb % CHUNK == 0
    nt = seq_len // tb
    tiles = batch * nt
    nc = tb // CHUNK
    nb = DN_HEADS * nc
    cur = lambda w: pl.BlockSpec((tb, w), lambda t: (jnp.minimum(t, tiles - 1), 0))
    prev = lambda w: pl.BlockSpec((tb, w), lambda t: (jnp.maximum(t - 1, 0), 0))
    return pl.pallas_call(
        functools.partial(_deltanet_kernel, tiles=tiles, tiles_per_seq=nt),
        grid=(tiles + 1,),
        in_specs=[
            cur(3 * DN_WIDTH), _resident((CONV_WIDTH, 3 * DN_WIDTH)),
            prev(DN_WIDTH), cur(LANES),
            pl.BlockSpec((nc, AB_ROWS, CHUNK), lambda t: (jnp.minimum(t, tiles - 1), 0, 0)),
            _resident((1, DN_HEAD_DIM))],
        out_specs=prev(DN_WIDTH),
        out_shape=jax.ShapeDtypeStruct((batch * seq_len, DN_WIDTH), BF16),
        scratch_shapes=[pltpu.VMEM((DN_HEADS, DN_HEAD_DIM, DN_HEAD_DIM), F32),
                        pltpu.VMEM((tb, DN_WIDTH), F32),
                        pltpu.VMEM((nb, 2 * CHUNK, DN_HEAD_DIM), BF16),
                        pltpu.VMEM((nb, CHUNK, CHUNK), BF16),
                        pltpu.VMEM((nb, CHUNK, DN_HEAD_DIM), F32),
                        pltpu.VMEM((nb, CHUNK, DN_HEAD_DIM), BF16),
                        pltpu.VMEM((nb, 1, DN_HEAD_DIM), F32),
                        pltpu.VMEM((tb + SUBLANES, 3 * DN_WIDTH), F32)],
        compiler_params=pltpu.CompilerParams(
            dimension_semantics=("arbitrary",), vmem_limit_bytes=VMEM_LIMIT),
        name="gated_deltanet",
    )(pre, conv_w.astype(F32), za, gcol, grow, dn_norm_w.astype(F32).reshape(1, DN_HEAD_DIM))


def _diffattn_kernel(lq1_ref, lk1_ref, lq2_ref, lk2_ref, q_ref, k_ref, v_ref, z_ref, nw_ref,
                     o_ref, m_ref, l_ref, acc_ref):
    tq = q_ref.shape[0]
    tk = TK_DA
    qi = pl.program_id(2)
    chains = [(hd, comp) for hd in range(HEADS_DA) for comp in range(2)]

    lane = lax.broadcasted_iota(jnp.int32, (1, LANES), 1)
    first_half = lane < DA_QK_DIM
    q_chain = []
    for hd in range(HEADS_DA):
        q = q_ref[:, hd * LANES:(hd + 1) * LANES]
        zero = jnp.zeros_like(q)
        q_chain += [jnp.where(first_half, q, zero), jnp.where(first_half, zero, q)]

    def attend(start, row0=0, nrows=None, width=None, masked=False, first=False):
        nrows = tq if nrows is None else nrows
        width = tk if width is None else width
        rows = slice(row0, row0 + nrows)
        kblk = [k_ref[pl.ds(start, width), hd * LANES:(hd + 1) * LANES] for hd in range(HEADS_DA)]
        ones = jnp.ones((width, LANES), BF16)
        vblk = [jnp.concatenate([v_ref[pl.ds(start, width), hd * LANES:(hd + 1) * LANES], ones],
                                axis=1) for hd in range(HEADS_DA)]
        scores = [lax.dot_general(q_chain[c][rows], kblk[hd], NT_DIMS, preferred_element_type=F32)
                  for c, (hd, _) in enumerate(chains)]
        if masked:
            rq = lax.broadcasted_iota(jnp.int32, (nrows, width), 0) + row0
            ck = lax.broadcasted_iota(jnp.int32, (nrows, width), 1)
            scores = [jnp.where(ck <= rq, s, NEG_INF) for s in scores]
        probs = []
        for c, s in enumerate(scores):
            m_new = jnp.max(s, axis=-1, keepdims=True)
            if first:
                m_new = jnp.broadcast_to(m_new, (nrows, LANES))
            else:
                m_prev = m_ref[c, rows, :]
                m_new = jnp.maximum(m_prev, m_new)
                alpha = jnp.exp2(m_prev - m_new)
                l_ref[c, rows, :] = alpha * l_ref[c, rows, :]
                acc_ref[c, rows, :] = alpha * acc_ref[c, rows, :]
            probs.append(jnp.exp2(
                (s - jnp.concatenate([m_new] * (width // LANES), axis=1)).astype(BF16)))
            m_ref[c, rows, :] = m_new
        for c, (hd, _) in enumerate(chains):
            pv = jnp.dot(probs[c], vblk[hd], preferred_element_type=F32)
            if first:
                acc_ref[c, rows, :] = pv[:, :DA_V_DIM]
                l_ref[c, rows, :] = pv[:, DA_V_DIM:]
            else:
                acc_ref[c, rows, :] += pv[:, :DA_V_DIM]
                l_ref[c, rows, :] += pv[:, DA_V_DIM:]

    diag0 = pl.multiple_of(qi * tq, tq)
    for r0 in range(0, tq, DIAG_BAND):
        attend(diag0, row0=r0, nrows=DIAG_BAND, width=r0 + DIAG_BAND, masked=True, first=True)

    def body(j, carry):
        attend(pl.multiple_of(j * tk, tk))
        return carry

    lax.fori_loop(0, qi * (tq // tk), body, 0)

    lam = (jnp.exp(jnp.sum(lq1_ref[...] * lk1_ref[...], axis=-1, keepdims=True))
           - jnp.exp(jnp.sum(lq2_ref[...] * lk2_ref[...], axis=-1, keepdims=True))
           + LAMBDA_INIT)
    for hd in range(HEADS_DA):
        hs = slice(hd * LANES, (hd + 1) * LANES)
        o = acc_ref[2 * hd] / l_ref[2 * hd] - lam * (acc_ref[2 * hd + 1] / l_ref[2 * hd + 1])
        ms = jnp.mean(o * o, axis=-1, keepdims=True)
        o = o * lax.rsqrt(ms + EPS) * nw_ref[...] * (1.0 - LAMBDA_INIT)
        o_ref[:, hs] = (o * z_ref[:, hs].astype(F32)).astype(BF16)


def _diff_attention(qb, kb, vb, zb, lambda_q1, lambda_k1, lambda_q2, lambda_k2, da_norm_w,
                    batch, seq_len):
    tq = TQ_DA
    assert tq % TK_DA == 0 and tq % DIAG_BAND == 0 and seq_len % tq == 0 and DA_HEADS % HEADS_DA == 0
    nq = seq_len // tq
    width = HEADS_DA * LANES
    lam_vec = lambda v: v.astype(F32).reshape(1, DA_QK_DIM)
    q_blk = pl.BlockSpec((tq, width), lambda b, h, i: (b * nq + i, h))
    kv_blk = pl.BlockSpec((seq_len, width), lambda b, h, i: (b, h))
    stat = pltpu.VMEM((2 * HEADS_DA, tq, LANES), F32)
    return pl.pallas_call(
        _diffattn_kernel,
        grid=(batch, DA_HEADS // HEADS_DA, nq),
        in_specs=[_resident((1, DA_QK_DIM))] * 4 + [q_blk, kv_blk, kv_blk, q_blk,
                                                    _resident((1, DA_V_DIM))],
        out_specs=q_blk,
        out_shape=jax.ShapeDtypeStruct((batch * seq_len, DA_WIDTH), BF16),
        scratch_shapes=[stat, stat, pltpu.VMEM((2 * HEADS_DA, tq, DA_V_DIM), F32)],
        compiler_params=pltpu.CompilerParams(
            dimension_semantics=("arbitrary", "arbitrary", "arbitrary"),
            vmem_limit_bytes=VMEM_LIMIT),
        name="diff_attention",
    )(lam_vec(lambda_q1), lam_vec(lambda_k1), lam_vec(lambda_q2), lam_vec(lambda_k2),
      qb, kb, vb, zb, da_norm_w.astype(F32).reshape(1, DA_V_DIM))


def _output_kernel(oa_ref, ob_ref, x_ref, nw_ref, wga_ref, wgb_ref, woa32_ref, wob32_ref, wo32_ref,
                   out_ref, woa_ref, wob_ref, wo_ref):
    @pl.when(pl.program_id(0) == 0)
    def _():
        woa_ref[...] = woa32_ref[...].astype(BF16)
        wob_ref[...] = wob32_ref[...].astype(BF16)
        wo_ref[...] = wo32_ref[...].astype(BF16)

    x = x_ref[...]
    ms = jnp.mean(x * x, axis=-1, keepdims=True)
    h = (x * lax.rsqrt(ms + EPS) * nw_ref[...]).astype(BF16)
    gate_a = _sigmoid(jnp.dot(h, wga_ref[...], preferred_element_type=F32))
    gate_b = _sigmoid(jnp.dot(h, wgb_ref[...], preferred_element_type=F32))
    y_a = jnp.dot(oa_ref[...], woa_ref[...], preferred_element_type=F32)
    y_b = jnp.dot(ob_ref[...], wob_ref[...], preferred_element_type=F32)
    y = gate_a * y_a + gate_b * y_b
    out_ref[...] = x + jnp.dot(y.astype(BF16), wo_ref[...], preferred_element_type=F32)


def _output_projection(oa, ob, x2, norm_w, w_gates, w_out_a, w_out_b, w_out):
    m = x2.shape[0]
    tm = TM_OUT
    assert m % tm == 0
    row_blk = lambda w: pl.BlockSpec((tm, w), lambda i: (i, 0))
    return pl.pallas_call(
        _output_kernel,
        grid=(m // tm,),
        in_specs=[row_blk(DN_WIDTH), row_blk(DA_WIDTH), row_blk(D_MODEL), _resident((1, D_MODEL)),
                  _resident((D_MODEL, D_MODEL)), _resident((D_MODEL, D_MODEL)),
                  _resident((DN_WIDTH, D_MODEL)), _resident((DA_WIDTH, D_MODEL)),
                  _resident((D_MODEL, D_MODEL))],
        out_specs=row_blk(D_MODEL),
        out_shape=jax.ShapeDtypeStruct((m, D_MODEL), F32),
        scratch_shapes=[pltpu.VMEM((DN_WIDTH, D_MODEL), BF16), pltpu.VMEM((DA_WIDTH, D_MODEL), BF16),
                        pltpu.VMEM((D_MODEL, D_MODEL), BF16)],
        compiler_params=pltpu.CompilerParams(
            dimension_semantics=("arbitrary",), vmem_limit_bytes=VMEM_LIMIT),
        name="output_projection",
    )(oa, ob, x2, norm_w.astype(F32).reshape(1, D_MODEL), *w_gates,
      w_out_a.astype(F32), w_out_b.astype(F32), w_out.astype(F32))


def kernel(x, norm_w, w_in, conv_w, a_log, dt_bias, dn_norm_w, q_norm_w, k_norm_w,
           lambda_q1, lambda_k1, lambda_q2, lambda_k2, da_norm_w, w_out_a, w_out_b, w_out):
    batch, seq_len, d_model = x.shape
    assert d_model == D_MODEL
    x2 = x.reshape(batch * seq_len, d_model)
    (pre, za, gcol, grow, qb, kb, vb, zb), w_gates = _in_projection(
        x2, norm_w, w_in, a_log, dt_bias, q_norm_w, k_norm_w)
    oa = _deltanet(pre, conv_w, za, gcol, grow, dn_norm_w, batch, seq_len)
    ob = _diff_attention(qb, kb, vb, zb, lambda_q1, lambda_k1, lambda_q2, lambda_k2, da_norm_w,
                         batch, seq_len)
    out = _output_projection(oa, ob, x2, norm_w, w_gates, w_out_a, w_out_b, w_out)
    return out.reshape(batch, seq_len, d_model)
```

```python
import functools
import math

import jax
import jax.numpy as jnp
from jax import lax
from jax.experimental import pallas as pl
from jax.experimental.pallas import tpu as pltpu

F32 = jnp.float32
BF16 = jnp.bfloat16

D_MODEL = 1024
DN_HEADS = 4
DN_HEAD_DIM = 128
DN_WIDTH = DN_HEADS * DN_HEAD_DIM
CONV_WIDTH = 4
CHUNK = 64
DA_HEADS = 4
DA_QK_DIM = 64
DA_V_DIM = 2 * DA_QK_DIM
DA_WIDTH = DA_HEADS * DA_V_DIM
DA_QK_WIDTH = DA_HEADS * 2 * DA_QK_DIM
LAMBDA_INIT = 0.8 - 0.6 * math.exp(-0.3 * 0)
EPS = 1e-6
NEG_INF = -1e30
LOG2_E = math.log2(math.e)

LANES = 128
SUBLANES = 8
BF16_ROWS = 16
AB_ROWS = BF16_ROWS
VMEM_LIMIT = 48 * 1024 * 1024

TM_IN = 512
TB_DN = 512
TQ_DA = 1024
DIAG_BAND = 256
TK_DA = 1024
HEADS_DA = 2
TM_OUT = 1024

NT_DIMS = (((1,), (1,)), ((), ()))
TN_DIMS = (((0,), (0,)), ((), ()))


def _sigmoid(v):
    return 0.5 + 0.5 * jnp.tanh(0.5 * v)


def _silu(v):
    h = 0.5 * v
    return h + h * jnp.tanh(h)


def _softplus(v):
    return jnp.maximum(v, 0.0) + jnp.log1p(jnp.exp(-jnp.abs(v)))


def _resident(shape):
    zeros = (0,) * len(shape)
    return pl.BlockSpec(shape, lambda *_: zeros, pipeline_mode=pl.Buffered(1))


def _inproj_kernel(x_ref, nw_ref, wqkv_ref, convw_ref, wza_ref, wab_ref, wabt_ref,
                   alog_l_ref, dtb_l_ref, alog_s_ref, dtb_s_ref, qnw_ref, knw_ref,
                   wqb_ref, wkb_ref, wvb_ref, wzb_ref,
                   qa_ref, ka_ref, va_ref, za_ref, gcol_ref, grow_ref,
                   qb_ref, kb_ref, vb_ref, zb_ref,
                   conv_buf, *, tiles_per_seq):
    tm = x_ref.shape[0]
    half = tm // 2
    i = pl.program_id(0)
    head0 = SUBLANES - (CONV_WIDTH - 1)
    slab = 512

    @pl.when(i % tiles_per_seq == 0)
    def _():
        conv_buf[0:SUBLANES, :] = jnp.zeros((SUBLANES, 3 * DN_WIDTH), F32)

    def normed(r0):
        x = x_ref[r0:r0 + half, :]
        ms = jnp.mean(x * x, axis=-1, keepdims=True)
        return (x * lax.rsqrt(ms + EPS) * nw_ref[...]).astype(BF16)

    def tasks(h, r0):
        rows = slice(r0, r0 + half)

        def conv_qkv(s, out_ref):
            cs = slice(s * DN_WIDTH, (s + 1) * DN_WIDTH)

            res = []

            def matmul():
                res.append(jnp.dot(h, wqkv_ref[:, cs], preferred_element_type=F32))
                conv_buf[SUBLANES + r0:SUBLANES + r0 + half, cs] = res[0]

            def epilogue():
                acc = convw_ref[CONV_WIDTH - 1:CONV_WIDTH, cs] * res[0]
                for j in range(CONV_WIDTH - 1):
                    acc = acc + (convw_ref[j:j + 1, cs]
                                 * conv_buf[head0 + j + r0:head0 + j + r0 + half, cs])
                y = _silu(acc)
                if s == 2:
                    out_ref[rows, :] = y.astype(BF16)
                    return
                scale = DN_HEAD_DIM ** -0.5 if s == 0 else 1.0
                for hd in range(DN_HEADS):
                    hs = slice(hd * DN_HEAD_DIM, (hd + 1) * DN_HEAD_DIM)
                    blk = y[:, hs]
                    ss = jnp.sum(blk * blk, axis=-1, keepdims=True)
                    out_ref[rows, hs] = (blk * (lax.rsqrt(ss + EPS) * scale)).astype(BF16)
            return matmul, [epilogue]

        def plain(w_ref, c0, out_ref, act):
            res = []

            def matmul():
                res.append(jnp.dot(h, w_ref[:, c0:c0 + slab], preferred_element_type=F32))

            def chunk(n):
                def run():
                    cs = slice(n * LANES, (n + 1) * LANES)
                    out_ref[rows, c0 + n * LANES:c0 + (n + 1) * LANES] = (
                        act(res[0][:, cs]).astype(BF16))
                return run
            return matmul, [chunk(n) for n in range(slab // LANES)]

        def qk_norm(w_ref, nrm_ref, out_ref, scale):
            res = []

            def matmul():
                res.append(jnp.dot(h, w_ref[...], preferred_element_type=F32))

            def chunk(hd):
                def run():
                    lo = lax.broadcasted_iota(jnp.int32, (half, LANES), 1) < DA_QK_DIM
                    blk = res[0][:, hd * LANES:(hd + 1) * LANES]
                    sq = blk * blk
                    s_lo = jnp.sum(jnp.where(lo, sq, 0.0), axis=-1, keepdims=True)
                    s_hi = jnp.sum(jnp.where(lo, 0.0, sq), axis=-1, keepdims=True)
                    msq = jnp.where(lo, s_lo, s_hi) * (1.0 / DA_QK_DIM)
                    out_ref[rows, hd * LANES:(hd + 1) * LANES] = (
                        blk * lax.rsqrt(msq + EPS) * nrm_ref[...] * scale).astype(BF16)
                return run
            return matmul, [chunk(hd) for hd in range(DA_HEADS)]

        def decay_beta():
            def matmul():
                ab = jnp.dot(h, wab_ref[...], preferred_element_type=F32)
                lane = lax.broadcasted_iota(jnp.int32, ab.shape, 1)
                g_l = -jnp.exp(alog_l_ref[...]) * _softplus(ab + dtb_l_ref[...])
                gcol_ref[rows, :] = jnp.where(lane < DN_HEADS, g_l, _sigmoid(ab))
                abt = lax.dot_general(wabt_ref[...], h, NT_DIMS,
                                      preferred_element_type=F32)
                row = lax.broadcasted_iota(jnp.int32, abt.shape, 0)
                g_s = -jnp.exp(alog_s_ref[...]) * _softplus(abt + dtb_s_ref[...])
                g_beta_s = jnp.where(row < DN_HEADS, g_s, _sigmoid(abt))
                for n in range(half // CHUNK):
                    grow_ref[r0 // CHUNK + n] = g_beta_s[:, n * CHUNK:(n + 1) * CHUNK]
            return matmul, []

        ident = lambda v: v
        return [conv_qkv(0, qa_ref), plain(wvb_ref, 0, vb_ref, ident),
                conv_qkv(1, ka_ref), plain(wza_ref, 0, za_ref, _silu),
                conv_qkv(2, va_ref), plain(wzb_ref, 0, zb_ref, _silu),
                qk_norm(wqb_ref, qnw_ref, qb_ref, DA_QK_DIM ** -0.5 * LOG2_E),
                qk_norm(wkb_ref, knw_ref, kb_ref, 1.0), decay_beta()]

    def run_tasks(task_list, extra=None):
        for n, (matmul, chunks) in enumerate(task_list):
            matmul()
            for fn in chunks:
                fn()
            if extra is not None and n == 6:
                extra()

    h_second = []
    run_tasks(tasks(normed(0), 0), extra=lambda: h_second.append(normed(half)))
    run_tasks(tasks(h_second[0], half))
    conv_buf[0:SUBLANES, :] = conv_buf[tm:tm + SUBLANES, :]


def _in_projection(x2, norm_w, w_in, conv_w, a_log, dt_bias, q_norm_w, k_norm_w, seq_len):
    m = x2.shape[0]
    tm = TM_IN
    assert m % tm == 0 and seq_len % tm == 0
    o = 0
    cols = {}
    for name, width in (("qkv", 3 * DN_WIDTH), ("za", DN_WIDTH), ("a", DN_HEADS), ("b", DN_HEADS),
                        ("qb", DA_QK_WIDTH), ("kb", DA_QK_WIDTH), ("vb", DA_WIDTH), ("zb", DA_WIDTH),
                        ("ga", D_MODEL), ("gb", D_MODEL)):
        cols[name] = w_in[:, o:o + width]
        o += width
    assert o == w_in.shape[1]
    wb = {k: v.astype(BF16) for k, v in cols.items()}
    w_ab = jnp.concatenate([cols["a"], cols["b"]], axis=1)
    w_ab_l = jnp.pad(w_ab, ((0, 0), (0, LANES - 2 * DN_HEADS))).astype(BF16)
    w_ab_s = jnp.pad(w_ab.T, ((0, AB_ROWS - 2 * DN_HEADS), (0, 0))).astype(BF16)
    alog_l = jnp.pad(a_log.astype(F32), (0, LANES - DN_HEADS)).reshape(1, LANES)
    dtb_l = jnp.pad(dt_bias.astype(F32), (0, LANES - DN_HEADS)).reshape(1, LANES)
    alog_s = jnp.pad(a_log.astype(F32), (0, AB_ROWS - DN_HEADS)).reshape(AB_ROWS, 1)
    dtb_s = jnp.pad(dt_bias.astype(F32), (0, AB_ROWS - DN_HEADS)).reshape(AB_ROWS, 1)
    qnw = jnp.tile(q_norm_w.astype(F32), 2).reshape(1, LANES)
    knw = jnp.tile(k_norm_w.astype(F32), 2).reshape(1, LANES)

    row_blk = lambda w: pl.BlockSpec((tm, w), lambda i: (i, 0))
    in_specs = [
        row_blk(D_MODEL), _resident((1, D_MODEL)),
        _resident((D_MODEL, 3 * DN_WIDTH)), _resident((CONV_WIDTH, 3 * DN_WIDTH)),
        _resident((D_MODEL, DN_WIDTH)), _resident((D_MODEL, LANES)), _resident((AB_ROWS, D_MODEL)),
        _resident((1, LANES)), _resident((1, LANES)), _resident((AB_ROWS, 1)), _resident((AB_ROWS, 1)),
        _resident((1, LANES)), _resident((1, LANES)),
        _resident((D_MODEL, DA_QK_WIDTH)), _resident((D_MODEL, DA_QK_WIDTH)),
        _resident((D_MODEL, DA_WIDTH)), _resident((D_MODEL, DA_WIDTH)),
    ]
    out_shapes = [jax.ShapeDtypeStruct((m, DN_WIDTH), BF16)] * 4 + [
        jax.ShapeDtypeStruct((m, LANES), F32),
        jax.ShapeDtypeStruct((m // CHUNK, AB_ROWS, CHUNK), F32)] + [
        jax.ShapeDtypeStruct((m, DA_WIDTH), BF16)] * 4
    out_specs = [row_blk(DN_WIDTH)] * 4 + [
        row_blk(LANES), pl.BlockSpec((tm // CHUNK, AB_ROWS, CHUNK), lambda i: (i, 0, 0))] + [
        row_blk(DA_WIDTH)] * 4
    outs = pl.pallas_call(
        functools.partial(_inproj_kernel, tiles_per_seq=seq_len // tm),
        grid=(m // tm,),
        in_specs=in_specs,
        out_specs=out_specs,
        out_shape=out_shapes,
        scratch_shapes=[pltpu.VMEM((tm + SUBLANES, 3 * DN_WIDTH), F32)],
        compiler_params=pltpu.CompilerParams(
            dimension_semantics=("arbitrary",), vmem_limit_bytes=VMEM_LIMIT),
        name="in_projection",
    )(x2, norm_w.astype(F32).reshape(1, D_MODEL), wb["qkv"], conv_w.astype(F32), wb["za"],
      w_ab_l, w_ab_s, alog_l, dtb_l, alog_s, dtb_s, qnw, knw,
      wb["qb"], wb["kb"], wb["vb"], wb["zb"])
    return outs, (wb["ga"], wb["gb"])


def _bmm(a, b):
    return jnp.einsum('nij,njk->nik', a, b, preferred_element_type=F32)


def _bmm_nt(a, b):
    return jnp.einsum('nid,njd->nij', a, b, preferred_element_type=F32)


def _split_bf16(x, parts):
    out = []
    for _ in range(parts - 1):
        hi = x.astype(BF16)
        out.append(hi)
        x = x - hi.astype(F32)
    out.append(x.astype(BF16))
    return out


def _deltanet_kernel(q_ref, k_ref, v_ref, z_ref, gcol_ref, grow_ref, nw_ref, o_ref,
                     state_ref, oraw_ref, lhs_ref, attn_ref, u_ref, kdec_ref, sdec_ref,
                     *, tiles, tiles_per_seq):
    tb = q_ref.shape[0]
    c = CHUNK
    nc = tb // c
    nh = DN_HEADS
    t = pl.program_id(0)
    keep_state = (t - 1) % tiles_per_seq != 0

    ri = lax.broadcasted_iota(jnp.int32, (c, c), 0)
    ci = lax.broadcasted_iota(jnp.int32, (c, c), 1)
    lower = ri >= ci
    strict = ri > ci
    eye = (ri == ci).astype(F32)
    tri = lower.astype(BF16)
    tri_t = (ri <= ci).astype(BF16)

    def heads_major(ref):
        return jnp.concatenate(
            [ref[:, hd * DN_HEAD_DIM:(hd + 1) * DN_HEAD_DIM].reshape(nc, c, DN_HEAD_DIM)
             for hd in range(nh)], axis=0)

    def phase1_stages():
        e = {}

        def decays():
            gb_col = gcol_ref[...].reshape(nc, c, LANES)
            tri_b = jnp.broadcast_to(tri, (nc, c, c))
            gc_col = sum(_bmm(tri_b, part) for part in _split_bf16(gb_col, 3))
            gb_row = grow_ref[...]
            gc_row = sum(jnp.dot(part, tri_t, preferred_element_type=F32)
                         for part in _split_bf16(gb_row.reshape(nc * AB_ROWS, c), 3)
                         ).reshape(nc, AB_ROWS, c)

            def col_form(src, lane):
                return jnp.concatenate(
                    [jnp.broadcast_to(src[:, :, lane + hd:lane + hd + 1], (nc, c, LANES))
                     for hd in range(nh)], axis=0)

            def row_form(src, row):
                return jnp.concatenate(
                    [src[:, row + hd:row + hd + 1, :] for hd in range(nh)], axis=0)

            e["g_c"] = col_form(gc_col, 0)
            e["beta_c"] = col_form(gb_col, nh)
            e["g_r"] = row_form(gc_row, 0)
            e["beta_r"] = row_form(gb_row, nh)
            e["g_last"] = e["g_c"][:, c - 1:c, :]
            diff = e["g_c"][:, :, :c] - e["g_r"]
            e["decay"] = jnp.where(lower, jnp.exp(jnp.where(lower, diff, 0.0)), 0.0)

        def gram_kk():
            e["q"], e["k"], e["v"] = heads_major(q_ref), heads_major(k_ref), heads_major(v_ref)
            l_strict = jnp.where(
                strict, _bmm_nt(e["k"], e["k"]) * e["decay"] * e["beta_c"][:, :, :c], 0.0)
            e["inv"] = eye - l_strict
            e["power"] = l_strict.astype(BF16)

        def gram_qk():
            e["attn"] = jnp.where(lower, _bmm_nt(e["q"], e["k"]) * e["decay"], 0.0).astype(BF16)

        def next_power():
            e["power"] = _bmm(e["power"], e["power"]).astype(BF16)

        def apply_power():
            e["inv"] = e["inv"] + _bmm(e["inv"].astype(BF16), e["power"])

        def solve_u():
            e["t_beta"] = e["inv"] * e["beta_r"]
            e["u"] = _bmm(e["t_beta"].astype(BF16), e["v"])

        def solve_w():
            e["w"] = _bmm((e["t_beta"] * jnp.exp(e["g_r"])).astype(BF16), e["k"]).astype(BF16)

        def commit():
            q_decay = (e["q"].astype(F32) * jnp.exp(e["g_c"])).astype(BF16)
            lhs_ref[...] = jnp.concatenate([e["w"], q_decay], axis=1)
            attn_ref[...] = e["attn"]
            u_ref[...] = e["u"]
            kdec_ref[...] = (e["k"].astype(F32) * jnp.exp(e["g_last"] - e["g_c"])).astype(BF16)
            sdec_ref[...] = jnp.exp(e["g_last"])

        levels = [next_power, apply_power] * (c.bit_length() - 2)
        return [decays, gram_kk, gram_qk] + levels + [solve_u, solve_w], commit

    def phase2_stages():
        def chunk_steps(n):
            idx = [hd * nc + n for hd in range(nh)]
            e = {}

            def through_state():
                e["s_old"] = [jnp.where(keep_state, state_ref[hd], 0.0) if n == 0
                              else state_ref[hd] for hd in range(nh)]
                e["ws_qs"] = [jnp.dot(lhs_ref[b], e["s_old"][hd].astype(BF16),
                                      preferred_element_type=F32) for hd, b in enumerate(idx)]

            def update():
                v_new = [(u_ref[b] - e["ws_qs"][hd][:c]).astype(BF16) for hd, b in enumerate(idx)]
                for hd, b in enumerate(idx):
                    oraw_ref[n * c:(n + 1) * c, hd * DN_HEAD_DIM:(hd + 1) * DN_HEAD_DIM] = (
                        e["ws_qs"][hd][c:]
                        + jnp.dot(attn_ref[b], v_new[hd], preferred_element_type=F32))
                for hd, b in enumerate(idx):
                    state_ref[hd] = e["s_old"][hd] * sdec_ref[b] + lax.dot_general(
                        kdec_ref[b], v_new[hd], TN_DIMS, preferred_element_type=F32)
            return [through_state, update]

        def gated_norm():
            for hd in range(nh):
                hs = slice(hd * DN_HEAD_DIM, (hd + 1) * DN_HEAD_DIM)
                o = oraw_ref[:, hs]
                ms = jnp.mean(o * o, axis=-1, keepdims=True)
                o = o * lax.rsqrt(ms + EPS) * nw_ref[...]
                o_ref[:, hs] = (o * z_ref[:, hs].astype(F32)).astype(BF16)

        return [fn for n in range(nc) for fn in chunk_steps(n)] + [gated_norm]

    @pl.when(t == 0)
    def _():
        state_ref[...] = jnp.zeros(state_ref.shape, F32)
        stages, commit = phase1_stages()
        for fn in stages:
            fn()
        commit()

    @pl.when((t > 0) & (t < tiles))
    def _():
        stages, commit = phase1_stages()
        others = phase2_stages()
        done = 0
        for n, fn in enumerate(stages):
            fn()
            upto = -(-(n + 1) * len(others) // len(stages))
            for other in others[done:upto]:
                other()
            done = upto
        commit()

    @pl.when(t == tiles)
    def _():
        for fn in phase2_stages():
            fn()


def _deltanet(qa, ka, va, za, gcol, grow, dn_norm_w, batch, seq_len):
    tb = TB_DN
    assert seq_len % tb == 0 and tb % CHUNK == 0
    nt = seq_len // tb
    tiles = batch * nt
    nc = tb // CHUNK
    nb = DN_HEADS * nc
    cur = lambda w: pl.BlockSpec((tb, w), lambda t: (jnp.minimum(t, tiles - 1), 0))
    prev = lambda w: pl.BlockSpec((tb, w), lambda t: (jnp.maximum(t - 1, 0), 0))
    return pl.pallas_call(
        functools.partial(_deltanet_kernel, tiles=tiles, tiles_per_seq=nt),
        grid=(tiles + 1,),
        in_specs=[cur(DN_WIDTH)] * 3 + [
            prev(DN_WIDTH), cur(LANES),
            pl.BlockSpec((nc, AB_ROWS, CHUNK), lambda t: (jnp.minimum(t, tiles - 1), 0, 0)),
            _resident((1, DN_HEAD_DIM))],
        out_specs=prev(DN_WIDTH),
        out_shape=jax.ShapeDtypeStruct((batch * seq_len, DN_WIDTH), BF16),
        scratch_shapes=[pltpu.VMEM((DN_HEADS, DN_HEAD_DIM, DN_HEAD_DIM), F32),
                        pltpu.VMEM((tb, DN_WIDTH), F32),
                        pltpu.VMEM((nb, 2 * CHUNK, DN_HEAD_DIM), BF16),
                        pltpu.VMEM((nb, CHUNK, CHUNK), BF16),
                        pltpu.VMEM((nb, CHUNK, DN_HEAD_DIM), F32),
                        pltpu.VMEM((nb, CHUNK, DN_HEAD_DIM), BF16),
                        pltpu.VMEM((nb, 1, DN_HEAD_DIM), F32)],
        compiler_params=pltpu.CompilerParams(
            dimension_semantics=("arbitrary",), vmem_limit_bytes=VMEM_LIMIT),
        name="gated_deltanet",
    )(qa, ka, va, za, gcol, grow, dn_norm_w.astype(F32).reshape(1, DN_HEAD_DIM))


def _diffattn_kernel(lq1_ref, lk1_ref, lq2_ref, lk2_ref, q_ref, k_ref, v_ref, z_ref, nw_ref,
                     o_ref, m_ref, l_ref, acc_ref):
    tq = q_ref.shape[0]
    tk = TK_DA
    qi = pl.program_id(2)
    chains = [(hd, comp) for hd in range(HEADS_DA) for comp in range(2)]

    lane = lax.broadcasted_iota(jnp.int32, (1, LANES), 1)
    first_half = lane < DA_QK_DIM
    q_chain = []
    for hd in range(HEADS_DA):
        q = q_ref[:, hd * LANES:(hd + 1) * LANES]
        zero = jnp.zeros_like(q)
        q_chain += [jnp.where(first_half, q, zero), jnp.where(first_half, zero, q)]

    def attend(start, row0=0, nrows=None, width=None, masked=False, first=False):
        nrows = tq if nrows is None else nrows
        width = tk if width is None else width
        rows = slice(row0, row0 + nrows)
        kblk = [k_ref[pl.ds(start, width), hd * LANES:(hd + 1) * LANES] for hd in range(HEADS_DA)]
        ones = jnp.ones((width, LANES), BF16)
        vblk = [jnp.concatenate([v_ref[pl.ds(start, width), hd * LANES:(hd + 1) * LANES], ones],
                                axis=1) for hd in range(HEADS_DA)]
        scores = [lax.dot_general(q_chain[c][rows], kblk[hd], NT_DIMS, preferred_element_type=F32)
                  for c, (hd, _) in enumerate(chains)]
        if masked:
            rq = lax.broadcasted_iota(jnp.int32, (nrows, width), 0) + row0
            ck = lax.broadcasted_iota(jnp.int32, (nrows, width), 1)
            scores = [jnp.where(ck <= rq, s, NEG_INF) for s in scores]
        probs = []
        for c, s in enumerate(scores):
            m_new = jnp.max(s, axis=-1, keepdims=True)
            if first:
                m_new = jnp.broadcast_to(m_new, (nrows, LANES))
            else:
                m_prev = m_ref[c, rows, :]
                m_new = jnp.maximum(m_prev, m_new)
                alpha = jnp.exp2(m_prev - m_new)
                l_ref[c, rows, :] = alpha * l_ref[c, rows, :]
                acc_ref[c, rows, :] = alpha * acc_ref[c, rows, :]
            probs.append(jnp.exp2(
                (s - jnp.concatenate([m_new] * (width // LANES), axis=1)).astype(BF16)))
            m_ref[c, rows, :] = m_new
        for c, (hd, _) in enumerate(chains):
            pv = jnp.dot(probs[c], vblk[hd], preferred_element_type=F32)
            if first:
                acc_ref[c, rows, :] = pv[:, :DA_V_DIM]
                l_ref[c, rows, :] = pv[:, DA_V_DIM:]
            else:
                acc_ref[c, rows, :] += pv[:, :DA_V_DIM]
                l_ref[c, rows, :] += pv[:, DA_V_DIM:]

    diag0 = pl.multiple_of(qi * tq, tq)
    for r0 in range(0, tq, DIAG_BAND):
        attend(diag0, row0=r0, nrows=DIAG_BAND, width=r0 + DIAG_BAND, masked=True, first=True)

    def body(j, carry):
        attend(pl.multiple_of(j * tk, tk))
        return carry

    lax.fori_loop(0, qi * (tq // tk), body, 0)

    lam = (jnp.exp(jnp.sum(lq1_ref[...] * lk1_ref[...], axis=-1, keepdims=True))
           - jnp.exp(jnp.sum(lq2_ref[...] * lk2_ref[...], axis=-1, keepdims=True))
           + LAMBDA_INIT)
    for hd in range(HEADS_DA):
        hs = slice(hd * LANES, (hd + 1) * LANES)
        o = acc_ref[2 * hd] / l_ref[2 * hd] - lam * (acc_ref[2 * hd + 1] / l_ref[2 * hd + 1])
        ms = jnp.mean(o * o, axis=-1, keepdims=True)
        o = o * lax.rsqrt(ms + EPS) * nw_ref[...] * (1.0 - LAMBDA_INIT)
        o_ref[:, hs] = (o * z_ref[:, hs].astype(F32)).astype(BF16)


def _diff_attention(qb, kb, vb, zb, lambda_q1, lambda_k1, lambda_q2, lambda_k2, da_norm_w,
                    batch, seq_len):
    tq = TQ_DA
    assert tq % TK_DA == 0 and tq % DIAG_BAND == 0 and seq_len % tq == 0 and DA_HEADS % HEADS_DA == 0
    nq = seq_len // tq
    width = HEADS_DA * LANES
    lam_vec = lambda v: v.astype(F32).reshape(1, DA_QK_DIM)
    q_blk = pl.BlockSpec((tq, width), lambda b, h, i: (b * nq + i, h))
    kv_blk = pl.BlockSpec((seq_len, width), lambda b, h, i: (b, h))
    stat = pltpu.VMEM((2 * HEADS_DA, tq, LANES), F32)
    return pl.pallas_call(
        _diffattn_kernel,
        grid=(batch, DA_HEADS // HEADS_DA, nq),
        in_specs=[_resident((1, DA_QK_DIM))] * 4 + [q_blk, kv_blk, kv_blk, q_blk,
                                                    _resident((1, DA_V_DIM))],
        out_specs=q_blk,
        out_shape=jax.ShapeDtypeStruct((batch * seq_len, DA_WIDTH), BF16),
        scratch_shapes=[stat, stat, pltpu.VMEM((2 * HEADS_DA, tq, DA_V_DIM), F32)],
        compiler_params=pltpu.CompilerParams(
            dimension_semantics=("arbitrary", "arbitrary", "arbitrary"),
            vmem_limit_bytes=VMEM_LIMIT),
        name="diff_attention",
    )(lam_vec(lambda_q1), lam_vec(lambda_k1), lam_vec(lambda_q2), lam_vec(lambda_k2),
      qb, kb, vb, zb, da_norm_w.astype(F32).reshape(1, DA_V_DIM))


def _output_kernel(oa_ref, ob_ref, x_ref, nw_ref, wga_ref, wgb_ref, woa32_ref, wob32_ref, wo32_ref,
                   out_ref, woa_ref, wob_ref, wo_ref):
    @pl.when(pl.program_id(0) == 0)
    def _():
        woa_ref[...] = woa32_ref[...].astype(BF16)
        wob_ref[...] = wob32_ref[...].astype(BF16)
        wo_ref[...] = wo32_ref[...].astype(BF16)

    x = x_ref[...]
    ms = jnp.mean(x * x, axis=-1, keepdims=True)
    h = (x * lax.rsqrt(ms + EPS) * nw_ref[...]).astype(BF16)
    gate_a = _sigmoid(jnp.dot(h, wga_ref[...], preferred_element_type=F32))
    gate_b = _sigmoid(jnp.dot(h, wgb_ref[...], preferred_element_type=F32))
    y_a = jnp.dot(oa_ref[...], woa_ref[...], preferred_element_type=F32)
    y_b = jnp.dot(ob_ref[...], wob_ref[...], preferred_element_type=F32)
    y = gate_a * y_a + gate_b * y_b
    out_ref[...] = x + jnp.dot(y.astype(BF16), wo_ref[...], preferred_element_type=F32)


def _output_projection(oa, ob, x2, norm_w, w_gates, w_out_a, w_out_b, w_out):
    m = x2.shape[0]
    tm = TM_OUT
    assert m % tm == 0
    row_blk = lambda w: pl.BlockSpec((tm, w), lambda i: (i, 0))
    return pl.pallas_call(
        _output_kernel,
        grid=(m // tm,),
        in_specs=[row_blk(DN_WIDTH), row_blk(DA_WIDTH), row_blk(D_MODEL), _resident((1, D_MODEL)),
                  _resident((D_MODEL, D_MODEL)), _resident((D_MODEL, D_MODEL)),
                  _resident((DN_WIDTH, D_MODEL)), _resident((DA_WIDTH, D_MODEL)),
                  _resident((D_MODEL, D_MODEL))],
        out_specs=row_blk(D_MODEL),
        out_shape=jax.ShapeDtypeStruct((m, D_MODEL), F32),
        scratch_shapes=[pltpu.VMEM((DN_WIDTH, D_MODEL), BF16), pltpu.VMEM((DA_WIDTH, D_MODEL), BF16),
                        pltpu.VMEM((D_MODEL, D_MODEL), BF16)],
        compiler_params=pltpu.CompilerParams(
            dimension_semantics=("arbitrary",), vmem_limit_bytes=VMEM_LIMIT),
        name="output_projection",
    )(oa, ob, x2, norm_w.astype(F32).reshape(1, D_MODEL), *w_gates,
      w_out_a.astype(F32), w_out_b.astype(F32), w_out.astype(F32))


def kernel(x, norm_w, w_in, conv_w, a_log, dt_bias, dn_norm_w, q_norm_w, k_norm_w,
           lambda_q1, lambda_k1, lambda_q2, lambda_k2, da_norm_w, w_out_a, w_out_b, w_out):
    batch, seq_len, d_model = x.shape
    assert d_model == D_MODEL
    x2 = x.reshape(batch * seq_len, d_model)
    (qa, ka, va, za, gcol, grow, qb, kb, vb, zb), w_gates = _in_projection(
        x2, norm_w, w_in, conv_w, a_log, dt_bias, q_norm_w, k_norm_w, seq_len)
    oa = _deltanet(qa, ka, va, za, gcol, grow, dn_norm_w, batch, seq_len)
    ob = _diff_attention(qb, kb, vb, zb, lambda_q1, lambda_k1, lambda_q2, lambda_k2, da_norm_w,
                         batch, seq_len)
    out = _output_projection(oa, ob, x2, norm_w, w_gates, w_out_a, w_out_b, w_out)
    return out.reshape(batch, seq_len, d_model)
```
